```python
import jax, jax.numpy as jnp
from jax import lax
import numpy as np

D_MODEL = 1024
BATCH = 16
SEQ = 2048
DEPTH = 2

CHUNK = 64
N_META = 16
GLA_HEADS = 4
GLA_KEY = D_MODEL // 2
GLA_VAL = D_MODEL
GLA_DK = GLA_KEY // GLA_HEADS
GLA_DV = GLA_VAL // GLA_HEADS
GATE_RANK = 16
GATE_NORMALIZER = 16.0
POOL_WINDOWS = (2, 4, 8, 16)
POOL_GROUPS = 4
POOL_WIDTH = D_MODEL
POOL_GDIM = POOL_WIDTH // POOL_GROUPS
N_BRANCH = 2
IN_COLS = 2 * GLA_KEY + 2 * GLA_VAL + GATE_RANK + POOL_WIDTH + N_BRANCH * D_MODEL
N_GROUPS = 4
EXPERTS_PER_GROUP = 4
N_EXPERTS = N_GROUPS * EXPERTS_PER_GROUP
TOP_K = 2
D_EXPERT = D_MODEL // 2
EPS = 1e-6

kernel_name = "chunk_causal_gla_pool_hmoe_hybrid"


def rms_norm(x, w):
    xf = x.astype(jnp.float32)
    y = xf * lax.rsqrt(jnp.mean(xf * xf, axis=-1, keepdims=True) + EPS)
    return (y * w.astype(jnp.float32)).astype(x.dtype)


def gla_mixer(q, k, v, og, g_low, w_gate_up, b_gate, norm_w):
    B, L, _ = q.shape
    dt = q.dtype
    pad = (-L) % CHUNK
    n_chunks = (L + pad) // CHUNK
    g = jax.nn.log_sigmoid((g_low @ w_gate_up + b_gate).astype(jnp.float32)) / GATE_NORMALIZER

    def to_chunks(t, hd):
        t = jnp.pad(t, ((0, 0), (pad, 0), (0, 0)))
        return t.reshape(B, n_chunks, CHUNK, GLA_HEADS, hd)

    qc = to_chunks(q * (GLA_DK ** -0.5), GLA_DK)
    kc = to_chunks(k, GLA_DK)
    vc = to_chunks(v, GLA_DV)
    gc = to_chunks(g, GLA_DK)
    b = jnp.cumsum(gc, axis=2)
    gam = b[:, :, -1]
    eb = jnp.exp(b).astype(dt)
    inv_eb = jnp.exp(-b).astype(dt)
    q_eb = qc * eb
    a_lo = jnp.einsum('bnihk,bnjhk->bnhij', q_eb, kc * inv_eb)
    a_up = jnp.einsum('bnihk,bnjhk->bnhij', qc * inv_eb, kc * eb)
    pos = jnp.arange(CHUNK)
    attn = jnp.where(pos[:, None] >= pos[None, :], a_lo, a_up)
    o_intra = jnp.einsum('bnhij,bnjhv->bnihv', attn, vc)
    k_dec = kc * jnp.exp(gam[:, :, None] - b).astype(dt)
    dec = jnp.exp(gam).astype(dt)

    def step(state, inp):
        q_t, k_t, v_t, d_t = inp
        o_t = jnp.einsum('bihk,bhkv->bihv', q_t, state)
        state = d_t[..., None] * state + jnp.einsum('bjhk,bjhv->bhkv', k_t, v_t)
        return state, o_t

    xs = (jnp.moveaxis(q_eb, 1, 0), jnp.moveaxis(k_dec, 1, 0),
          jnp.moveaxis(vc, 1, 0), jnp.moveaxis(dec, 1, 0))
    s0 = jnp.zeros((B, GLA_HEADS, GLA_DK, GLA_DV), dt)
    _, o_inter = lax.scan(step, s0, xs)
    o = o_intra + jnp.moveaxis(o_inter, 0, 1)
    o = rms_norm(o, norm_w)
    o = o.reshape(B, n_chunks * CHUNK, GLA_VAL)[:, pad:]
    return o * jax.nn.silu(og)


def pool_mixer(u, w_grp, scale):
    B, L, _ = u.shape
    uf = u.astype(jnp.float32)
    cs = jnp.concatenate([jnp.zeros((B, 1, POOL_WIDTH), jnp.float32),
                          jnp.cumsum(uf, axis=1)], axis=1)
    t = jnp.arange(L)
    outs = []
    for gi, w in enumerate(POOL_WINDOWS):
        sl = slice(gi * POOL_GDIM, (gi + 1) * POOL_GDIM)
        c = cs[:, :, sl]
        upper = c[:, 1:]
        lower = jnp.pad(c, ((0, 0), (w - 1, 0), (0, 0)))[:, :L]
        count = jnp.minimum(t + 1, w).astype(jnp.float32)[None, :, None]
        outs.append((upper - lower) / count - uf[:, :, sl])
    pooled = jnp.stack(outs, axis=2).astype(u.dtype)
    mixed = jnp.einsum('blgc,gcd->blgd', pooled, w_grp).reshape(B, L, POOL_WIDTH)
    return mixed * scale


def hier_moe(h, w_rg, w_re, w_eg, w_eu, w_ed):
    B, L, D = h.shape
    xt = h.reshape(-1, D)
    T = xt.shape[0]
    pg = jax.nn.softmax((xt @ w_rg).astype(jnp.float32), axis=-1)
    pg_top, g_idx = lax.top_k(pg, 1)
    le = (xt @ w_re).astype(jnp.float32).reshape(T, N_GROUPS, EXPERTS_PER_GROUP)
    le_sel = jnp.take_along_axis(le, g_idx[:, :, None], axis=1)[:, 0]
    pe = jax.nn.softmax(le_sel, axis=-1)
    pe_top, e_idx = lax.top_k(pe, TOP_K)
    wts = pg_top * pe_top / jnp.sum(pe_top, axis=-1, keepdims=True)
    gid = g_idx * EXPERTS_PER_GROUP + e_idx
    comb = jnp.einsum('tk,tke->te', wts,
                      jax.nn.one_hot(gid, N_EXPERTS, dtype=jnp.float32)).astype(h.dtype)
    out = jnp.zeros_like(xt)
    for e in range(N_EXPERTS):
        hid = jax.nn.silu(xt @ w_eg[e]) * (xt @ w_eu[e])
        out = out + comb[:, e:e + 1] * (hid @ w_ed[e])
    return out.reshape(B, L, D)


def setup_inputs(seed: int = 0) -> dict:
    key = jax.random.key(seed)
    ks = jax.random.split(key, 20)
    f32 = jnp.float32

    def nrm(k, shape, scale):
        return jax.random.normal(k, shape, f32) * scale

    return {
        "x": nrm(ks[0], (BATCH, SEQ, D_MODEL), 1.0),
        "meta_tokens": nrm(ks[1], (N_META, D_MODEL), 1.0),
        "norm1_w": 1.0 + nrm(ks[2], (DEPTH, D_MODEL), 0.05),
        "w_in": nrm(ks[3], (DEPTH, D_MODEL, IN_COLS), D_MODEL ** -0.5),
        "w_gate_up": nrm(ks[4], (DEPTH, GATE_RANK, GLA_KEY), GATE_RANK ** -0.5),
        "b_gate": nrm(ks[5], (DEPTH, GLA_KEY), 0.1),
        "gla_norm_w": 1.0 + nrm(ks[6], (DEPTH, GLA_DV), 0.05),
        "w_pool_grp": nrm(ks[7], (DEPTH, POOL_GROUPS, POOL_GDIM, POOL_GDIM), POOL_GDIM ** -0.5),
        "pool_scale": 1.0 + nrm(ks[8], (DEPTH, POOL_WIDTH), 0.1),
        "w_br_gla": nrm(ks[9], (DEPTH, GLA_VAL, D_MODEL), GLA_VAL ** -0.5),
        "w_br_pool": nrm(ks[10], (DEPTH, POOL_WIDTH, D_MODEL), POOL_WIDTH ** -0.5),
        "w_out": nrm(ks[11], (DEPTH, D_MODEL, D_MODEL), D_MODEL ** -0.5),
        "norm2_w": 1.0 + nrm(ks[12], (DEPTH, D_MODEL), 0.05),
        "w_router_group": nrm(ks[13], (DEPTH, D_MODEL, N_GROUPS), D_MODEL ** -0.5),
        "w_router_expert": nrm(ks[14], (DEPTH, D_MODEL, N_EXPERTS), D_MODEL ** -0.5),
        "w_exp_gate": nrm(ks[15], (DEPTH, N_EXPERTS, D_MODEL, D_EXPERT), D_MODEL ** -0.5),
        "w_exp_up": nrm(ks[16], (DEPTH, N_EXPERTS, D_MODEL, D_EXPERT), D_MODEL ** -0.5),
        "w_exp_down": nrm(ks[17], (DEPTH, N_EXPERTS, D_EXPERT, D_MODEL), D_EXPERT ** -0.5),
        "final_norm_w": 1.0 + nrm(ks[18], (D_MODEL,), 0.05),
    }


def reference(x, meta_tokens, norm1_w, w_in, w_gate_up, b_gate, gla_norm_w, w_pool_grp,
              pool_scale, w_br_gla, w_br_pool, w_out, norm2_w, w_router_group,
              w_router_expert, w_exp_gate, w_exp_up, w_exp_down, final_norm_w):
    B = x.shape[0]
    meta = jnp.broadcast_to(meta_tokens[None].astype(x.dtype), (B, N_META, D_MODEL))
    h = jnp.concatenate([meta, x], axis=1)
    splits = [GLA_KEY, 2 * GLA_KEY, 2 * GLA_KEY + GLA_VAL, 2 * GLA_KEY + 2 * GLA_VAL,
              2 * GLA_KEY + 2 * GLA_VAL + GATE_RANK,
              2 * GLA_KEY + 2 * GLA_VAL + GATE_RANK + POOL_WIDTH]
    for l in range(DEPTH):
        hn = rms_norm(h, norm1_w[l])
        z = hn @ w_in[l]
        q, k, v, og, g_low, u, gate_cols = jnp.split(z, splits, axis=-1)
        y_gla = gla_mixer(q, k, v, og, g_low, w_gate_up[l], b_gate[l], gla_norm_w[l])
        y_pool = pool_mixer(u, w_pool_grp[l], pool_scale[l])
        gates = jax.nn.sigmoid(gate_cols.reshape(B, -1, N_BRANCH, D_MODEL))
        merged = (gates[:, :, 0] * (y_gla @ w_br_gla[l])
                  + gates[:, :, 1] * (y_pool @ w_br_pool[l]))
        h = h + merged @ w_out[l]
        h = h + hier_moe(rms_norm(h, norm2_w[l]), w_router_group[l], w_router_expert[l],
                         w_exp_gate[l], w_exp_up[l], w_exp_down[l])
    h = rms_norm(h, final_norm_w)
    return h[:, N_META:]
```

```python
import functools

import jax
import jax.numpy as jnp
from jax import lax
from jax.experimental import pallas as pl
from jax.experimental.pallas import tpu as pltpu

F32 = jnp.float32
BF16 = jnp.bfloat16

EPS = 1e-6
CHUNK = 64
N_META = 16
PAD = CHUNK - N_META
GLA_HEADS = 4
GATE_NORMALIZER = 16.0
POOL_WINDOWS = (2, 4, 8, 16)
N_GROUPS = 4
EXPERTS_PER_GROUP = 4
N_EXPERTS = N_GROUPS * EXPERTS_PER_GROUP
PAIRS = ((0, 1), (0, 2), (0, 3), (1, 2), (1, 3), (2, 3))
N_CLASSES = N_GROUPS * len(PAIRS)
CLASS_ROWS = 32
LANES = 128
HALO = 32
ROUTE_LANES = 128
VMEM_LIMIT = 56 * 1024 * 1024


def _cparams(sem):
    return pltpu.CompilerParams(dimension_semantics=sem, vmem_limit_bytes=VMEM_LIMIT)


def _resident(shape):
    nd = len(shape)
    return pl.BlockSpec(shape, lambda *_: (0,) * nd, pipeline_mode=pl.Buffered(1))


def _split(x):
    hi = x.astype(BF16)
    lo = (x - hi.astype(F32)).astype(BF16)
    return hi, lo


def _dot(a, b):
    return jnp.dot(a, b, preferred_element_type=F32)


def _dot_nt(a, b):
    return lax.dot_general(a, b, (((1,), (1,)), ((), ())), preferred_element_type=F32)


def _dot_tn(a, b):
    return lax.dot_general(a, b, (((0,), (0,)), ((), ())), preferred_element_type=F32)


def _sigmoid(x):
    return 1.0 / (1.0 + jnp.exp(-x))


def _inproj_kernel(h_ref, nw_ref, w_ref, z_ref, zg_ref, xn_ref, *, n_main, col_blk):
    x = h_ref[...]
    ms = jnp.mean(x * x, axis=-1, keepdims=True)
    xn_ref[...] = ((x * lax.rsqrt(ms + EPS)) * nw_ref[...]).astype(BF16)
    for j in range(n_main // col_blk):
        sl = slice(j * col_blk, (j + 1) * col_blk)
        z_ref[:, sl] = _dot(xn_ref[...], w_ref[:, sl]).astype(BF16)
    zg_ref[...] = _dot(xn_ref[...], w_ref[:, n_main:])


def _inproj(h, norm_w, w_all, *, tm, n_main):
    T, D = h.shape
    n_all = w_all.shape[1]
    return pl.pallas_call(
        functools.partial(_inproj_kernel, n_main=n_main, col_blk=512),
        grid=(T // tm,),
        in_specs=[pl.BlockSpec((tm, D), lambda i: (i, 0)),
                  _resident((1, D)),
                  _resident((D, n_all))],
        out_specs=[pl.BlockSpec((tm, n_main), lambda i: (i, 0)),
                   pl.BlockSpec((tm, n_all - n_main), lambda i: (i, 0))],
        out_shape=[jax.ShapeDtypeStruct((T, n_main), BF16),
                   jax.ShapeDtypeStruct((T, n_all - n_main), F32)],
        scratch_shapes=[pltpu.VMEM((tm, D), BF16)],
        compiler_params=_cparams(("parallel",)),
        name="inproj",
    )(h, norm_w, w_all)


def _gla_kernel(q_ref, k_ref, v_ref, og_ref, zg_ref, wgu_ref, bg_ref, nw_ref, y_ref, st_ref,
                *, dk, dv):
    n = pl.program_id(1)

    @pl.when(n == 0)
    def _():
        st_ref[...] = jnp.zeros_like(st_ref)

    gl_hi, gl_lo = _split(zg_ref[...])
    wg_hi, wg_lo = _split(wgu_ref[...])
    gpre = _dot(gl_hi, wg_hi) + _dot(gl_hi, wg_lo) + _dot(gl_lo, wg_hi) + bg_ref[...]
    g = (jnp.minimum(gpre, 0.0) - jnp.log(1.0 + jnp.exp(-jnp.abs(gpre)))) * (1.0 / GATE_NORMALIZER)
    row = lax.broadcasted_iota(jnp.int32, (CHUNK, 1), 0)
    g = jnp.where(jnp.logical_and(n == 0, row < PAD), 0.0, g)

    ri = lax.broadcasted_iota(jnp.int32, (CHUNK, CHUNK), 0)
    ci = lax.broadcasted_iota(jnp.int32, (CHUNK, CHUNK), 1)
    lower = ri >= ci
    tri = jnp.where(lower, 1.0, 0.0).astype(BF16)
    g_hi, g_lo = _split(g)
    bcum = _dot(tri, g_hi) + _dot(tri, g_lo)
    gam = bcum[CHUNK - 1:CHUNK, :]
    eb = jnp.exp(bcum)
    ieb = jnp.exp(-bcum)
    dec = jnp.exp(gam)
    q = q_ref[...].astype(F32) * (dk ** -0.5)
    k = k_ref[...].astype(F32)
    q_eb = (q * eb).astype(BF16)
    q_ieb = (q * ieb).astype(BF16)
    k_eb = (k * eb).astype(BF16)
    k_ieb = (k * ieb).astype(BF16)
    k_dec = (k * jnp.exp(gam - bcum)).astype(BF16)

    for hd in range(GLA_HEADS):
        ks = slice(hd * dk, (hd + 1) * dk)
        vs = slice(hd * dv, (hd + 1) * dv)
        a_lo = _dot_nt(q_eb[:, ks], k_ieb[:, ks])
        a_up = _dot_nt(q_ieb[:, ks], k_eb[:, ks])
        attn = jnp.where(lower, a_lo, a_up).astype(BF16)
        v_h = v_ref[:, vs]
        st = st_ref[hd]
        o = _dot(attn, v_h) + _dot_nt(q_eb[:, ks], st.astype(BF16))
        st_ref[hd] = st * dec[:, ks] + _dot_tn(v_h, k_dec[:, ks])
        o = o * lax.rsqrt(jnp.mean(o * o, axis=-1, keepdims=True) + EPS) * nw_ref[...]
        og = og_ref[:, vs].astype(F32)
        y_ref[:, vs] = (o * (og * _sigmoid(og))).astype(BF16)


def _gla(z, zg, wgu, bg, norm_w, *, batch, n_chunks, dk, dv):
    T = z.shape[0]
    key = GLA_HEADS * dk
    val = GLA_HEADS * dv
    rb = lambda b, n: b * n_chunks + n
    return pl.pallas_call(
        functools.partial(_gla_kernel, dk=dk, dv=dv),
        grid=(batch, n_chunks),
        in_specs=[pl.BlockSpec((CHUNK, key), lambda b, n: (rb(b, n), 0)),
                  pl.BlockSpec((CHUNK, key), lambda b, n: (rb(b, n), 1)),
                  pl.BlockSpec((CHUNK, val), lambda b, n: (rb(b, n), 1)),
                  pl.BlockSpec((CHUNK, val), lambda b, n: (rb(b, n), 2)),
                  pl.BlockSpec((CHUNK, LANES), lambda b, n: (rb(b, n), 0)),
                  _resident(wgu.shape), _resident(bg.shape), _resident(norm_w.shape)],
        out_specs=pl.BlockSpec((CHUNK, val), lambda b, n: (rb(b, n), 0)),
        out_shape=jax.ShapeDtypeStruct((T, val), BF16),
        scratch_shapes=[pltpu.VMEM((GLA_HEADS, dv, dk), F32)],
        compiler_params=_cparams(("parallel", "arbitrary")),
        name="gla",
    )(z, z, z, z, zg, wgu, bg, norm_w)


def _mix_kernel(ygla_ref, u_ref, halo_ref, g0_ref, g1_ref, h_ref,
                wgrp_ref, pscale_ref, wbg_ref, wbp_ref, wout_ref, n2_ref, wr_ref,
                hx_ref, info_ref, cnt_ref,
                ue_ref, p2_ref, p4_ref, p8_ref, p16_ref, tri_ref, carry_ref,
                *, tm, seq_rows, d_model):
    i = pl.program_id(0)
    gd = d_model // len(POOL_WINDOWS)

    @pl.when(i == 0)
    def _():
        r = lax.broadcasted_iota(jnp.int32, (tm, tm), 0)
        c = lax.broadcasted_iota(jnp.int32, (tm, tm), 1)
        tri_ref[...] = jnp.where(r < c, 1.0, 0.0).astype(BF16)
        carry_ref[...] = jnp.zeros_like(carry_ref)

    rowf = (i * tm + lax.broadcasted_iota(jnp.int32, (tm, 1), 0)).astype(F32)
    lp = rowf - jnp.floor(rowf / float(seq_rows)) * float(seq_rows)
    is_pad = lp < float(PAD)
    seqpos1 = lp - float(PAD - 1)

    ue_ref[0:HALO, :] = halo_ref[...].astype(F32)
    ue_ref[HALO:, :] = u_ref[...].astype(F32)
    n8 = tm + HALO - 8
    p2_ref[8:, :] = ue_ref[8:, :] + ue_ref[pl.ds(7, n8), :]
    p4_ref[16:, :] = p2_ref[16:, gd:] + p2_ref[pl.ds(14, n8 - 8), gd:]
    p8_ref[24:, :] = p4_ref[24:, gd:] + p4_ref[pl.ds(20, n8 - 16), gd:]
    p16_ref[HALO:, :] = p8_ref[HALO:, gd:] + p8_ref[pl.ds(24, tm), gd:]
    sums = (p2_ref[HALO:, 0:gd], p4_ref[HALO:, 0:gd], p8_ref[HALO:, 0:gd], p16_ref[HALO:, :])

    ypool = []
    for gi, w in enumerate(POOL_WINDOWS):
        cs = slice(gi * gd, (gi + 1) * gd)
        cnt = jnp.clip(seqpos1, 1.0, float(w))
        pooled = sums[gi] / cnt - ue_ref[HALO:, cs]
        mixed = _dot(pooled.astype(BF16), wgrp_ref[gi]) * pscale_ref[:, cs]
        ypool.append(mixed.astype(BF16))
    ypool = jnp.concatenate(ypool, axis=1)

    br_g = _dot(ygla_ref[...], wbg_ref[...])
    br_p = _dot(ypool, wbp_ref[...])
    merged = (_sigmoid(g0_ref[...].astype(F32)) * br_g + _sigmoid(g1_ref[...].astype(F32)) * br_p)
    delta = _dot(merged.astype(BF16), wout_ref[...])
    h_new = h_ref[...] + jnp.where(is_pad, 0.0, delta)
    hx_ref[:, 0:d_model] = h_new

    ms = jnp.mean(h_new * h_new, axis=-1, keepdims=True)
    xn = (h_new * lax.rsqrt(ms + EPS)) * n2_ref[...]
    x_hi, x_lo = _split(xn)
    w_hi, w_lo = _split(wr_ref[...])
    lt = _dot_nt(w_hi, x_hi) + _dot_nt(w_hi, x_lo) + _dot_nt(w_lo, x_hi)

    lg = [lt[j:j + 1, :] for j in range(N_GROUPS)]
    m = jnp.maximum(jnp.maximum(lg[0], lg[1]), jnp.maximum(lg[2], lg[3]))
    gidx = jnp.where(lg[0] == m, 0, jnp.where(lg[1] == m, 1, jnp.where(lg[2] == m, 2, 3)))
    pg_top = 1.0 / (jnp.exp(lg[0] - m) + jnp.exp(lg[1] - m) + jnp.exp(lg[2] - m) + jnp.exp(lg[3] - m))
    sel = []
    for e in range(EXPERTS_PER_GROUP):
        rows = [lt[N_GROUPS + g * EXPERTS_PER_GROUP + e:N_GROUPS + g * EXPERTS_PER_GROUP + e + 1, :]
                for g in range(N_GROUPS)]
        sel.append(jnp.where(gidx == 0, rows[0], jnp.where(gidx == 1, rows[1],
                                                           jnp.where(gidx == 2, rows[2], rows[3]))))
    m1 = jnp.maximum(jnp.maximum(sel[0], sel[1]), jnp.maximum(sel[2], sel[3]))
    i1 = jnp.where(sel[0] == m1, 0, jnp.where(sel[1] == m1, 1, jnp.where(sel[2] == m1, 2, 3)))
    neg = jnp.float32(-jnp.inf)
    rest = [jnp.where(i1 == e, neg, sel[e]) for e in range(EXPERTS_PER_GROUP)]
    m2 = jnp.maximum(jnp.maximum(rest[0], rest[1]), jnp.maximum(rest[2], rest[3]))
    i2 = jnp.where(rest[0] == m2, 0, jnp.where(rest[1] == m2, 1, jnp.where(rest[2] == m2, 2, 3)))
    r21 = jnp.exp(m2 - m1)
    w1 = pg_top / (1.0 + r21)
    w2 = pg_top * r21 / (1.0 + r21)
    lo = jnp.minimum(i1, i2)
    hi = jnp.maximum(i1, i2)
    w_lo = jnp.where(i1 < i2, w1, w2)
    w_hi = jnp.where(i1 < i2, w2, w1)
    pidx = jnp.where(lo == 0, hi - 1, jnp.where(lo == 1, hi + 1, 5))
    cls = gidx * len(PAIRS) + pidx

    crow = lax.broadcasted_iota(jnp.int32, (CLASS_ROWS, tm), 0)
    onehot = jnp.where(crow == cls, 1.0, 0.0)
    prefix = _dot(onehot.astype(BF16), tri_ref[...]) + carry_ref[:, 0:1]
    rank = jnp.sum(onehot * prefix, axis=0, keepdims=True)
    carry_ref[...] = carry_ref[...] + jnp.sum(onehot, axis=1, keepdims=True)
    cnt_ref[...] = carry_ref[...]

    irow = lax.broadcasted_iota(jnp.int32, (8, tm), 0)
    info_ref[...] = jnp.where(irow == 0, cls.astype(F32), jnp.where(irow == 1, rank, 0.0))
    wrow = lax.broadcasted_iota(jnp.int32, (ROUTE_LANES, tm), 0)
    wrows = jnp.where(wrow == 0, w_lo, jnp.where(wrow == 1, w_hi, 0.0))
    hx_ref[:, d_model:] = wrows.T


def _mix(ygla, z, h, wgrp, pscale, wbg, wbp, wout, n2, wr, *, tm, seq_rows):
    T, D = h.shape
    nblk = tm // HALO
    return pl.pallas_call(
        functools.partial(_mix_kernel, tm=tm, seq_rows=seq_rows, d_model=D),
        grid=(T // tm,),
        in_specs=[pl.BlockSpec((tm, D), lambda i: (i, 0)),
                  pl.BlockSpec((tm, D), lambda i: (i, 3)),
                  pl.BlockSpec((HALO, D), lambda i: (jnp.maximum(i * nblk - 1, 0), 3)),
                  pl.BlockSpec((tm, D), lambda i: (i, 4)),
                  pl.BlockSpec((tm, D), lambda i: (i, 5)),
                  pl.BlockSpec((tm, D), lambda i: (i, 0)),
                  _resident(wgrp.shape), _resident(pscale.shape), _resident(wbg.shape),
                  _resident(wbp.shape), _resident(wout.shape), _resident(n2.shape),
                  _resident(wr.shape)],
        out_specs=[pl.BlockSpec((tm, D + ROUTE_LANES), lambda i: (i, 0)),
                   pl.BlockSpec((8, tm), lambda i: (0, i)),
                   pl.BlockSpec((CLASS_ROWS, LANES), lambda i: (0, 0))],
        out_shape=[jax.ShapeDtypeStruct((T, D + ROUTE_LANES), F32),
                   jax.ShapeDtypeStruct((8, T), F32),
                   jax.ShapeDtypeStruct((CLASS_ROWS, LANES), F32)],
        scratch_shapes=[pltpu.VMEM((tm + HALO, D), F32),
                        pltpu.VMEM((tm + HALO, D), F32),
                        pltpu.VMEM((tm + HALO, D - D // 4), F32),
                        pltpu.VMEM((tm + HALO, D - 2 * (D // 4)), F32),
                        pltpu.VMEM((tm + HALO, D // 4), F32),
                        pltpu.VMEM((tm, tm), BF16),
                        pltpu.VMEM((CLASS_ROWS, LANES), F32)],
        compiler_params=_cparams(("arbitrary",)),
        name="mix",
    )(ygla, z, z, z, z, h, wgrp, pscale, wbg, wbp, wout, n2, wr)


def _dispatch_kernel(tend_ref, nu_ref, pos_ref, hx_ref, xs_ref, zero_ref, sem, zsem,
                     *, tm, tm_exp, n_tiles):
    def fill(j):
        return pltpu.make_async_copy(zero_ref, xs_ref.at[pl.ds(j * tm_exp, tm_exp)], zsem)

    @pl.when(pl.program_id(0) == 0)
    def _():
        zero_ref[...] = jnp.zeros_like(zero_ref)
        for wait in (False, True):
            for c in range(N_CLASSES):
                prev = tend_ref[c - 1] if c else 0

                @pl.when(tend_ref[c] > prev)
                def _():
                    cp = fill(tend_ref[c] - 1)
                    cp.wait() if wait else cp.start()

            def tail(j, carry):
                cp = fill(j)
                cp.wait() if wait else cp.start()
                return carry
            lax.fori_loop(nu_ref[0], n_tiles, tail, 0)

    def issue(r, carry):
        pltpu.make_async_copy(hx_ref.at[pl.ds(r, 1)], xs_ref.at[pl.ds(pos_ref[r], 1)], sem).start()
        return carry
    lax.fori_loop(0, tm, issue, 0, unroll=8)
    pltpu.make_async_copy(hx_ref, xs_ref.at[pl.ds(0, tm)], sem).wait()


def _dispatch(hx, pos, tile_end, n_used, *, tm, tm_exp, n_sorted):
    T, W = hx.shape
    grid_spec = pltpu.PrefetchScalarGridSpec(
        num_scalar_prefetch=2,
        grid=(T // tm,),
        in_specs=[pl.BlockSpec((tm,), lambda i, te, nu: (i,), memory_space=pltpu.SMEM),
                  pl.BlockSpec((tm, W), lambda i, te, nu: (i, 0))],
        out_specs=pl.BlockSpec(memory_space=pl.ANY),
        scratch_shapes=[pltpu.VMEM((tm_exp, W), F32),
                        pltpu.SemaphoreType.DMA(()), pltpu.SemaphoreType.DMA(())],
    )
    return pl.pallas_call(
        functools.partial(_dispatch_kernel, tm=tm, tm_exp=tm_exp, n_tiles=n_sorted // tm_exp),
        grid_spec=grid_spec,
        out_shape=jax.ShapeDtypeStruct((n_sorted, W), F32),
        compiler_params=_cparams(("arbitrary",)),
        name="dispatch",
    )(tile_end, n_used, pos, hx)


def _collect_kernel(pos_ref, ys_ref, nw_ref, o_ref, buf_ref, sem, *, tm, final_norm):
    def issue(r, carry):
        pltpu.make_async_copy(ys_ref.at[pl.ds(pos_ref[r], 1)], buf_ref.at[pl.ds(r, 1)], sem).start()
        return carry
    lax.fori_loop(0, tm, issue, 0, unroll=8)
    pltpu.make_async_copy(ys_ref.at[pl.ds(0, tm)], buf_ref, sem).wait()
    x = buf_ref[...]
    if final_norm:
        ms = jnp.mean(x * x, axis=-1, keepdims=True)
        x = (x * lax.rsqrt(ms + EPS)) * nw_ref[...]
    o_ref[...] = x


def _collect(ys, pos, norm_w, *, tm, final_norm):
    n_out = pos.shape[0]
    D = ys.shape[1]
    return pl.pallas_call(
        functools.partial(_collect_kernel, tm=tm, final_norm=final_norm),
        grid=(n_out // tm,),
        in_specs=[pl.BlockSpec((tm,), lambda i: (i,), memory_space=pltpu.SMEM),
                  pl.BlockSpec(memory_space=pl.ANY),
                  _resident(norm_w.shape)],
        out_specs=pl.BlockSpec((tm, D), lambda i: (i, 0)),
        out_shape=jax.ShapeDtypeStruct((n_out, D), F32),
        scratch_shapes=[pltpu.VMEM((tm, D), F32), pltpu.SemaphoreType.DMA(())],
        compiler_params=_cparams(("arbitrary",)),
        name="collect_norm" if final_norm else "collect",
    )(pos, ys, norm_w)


def _moe_kernel(ea_ref, eb_ref, nu_ref, xs_ref, n2_ref, wga_ref, wua_ref, wda_ref,
                wgb_ref, wub_ref, wdb_ref, ys_ref, *, d_model):
    i = pl.program_id(0)

    @pl.when(i < nu_ref[0])
    def _():
        x = xs_ref[:, 0:d_model]
        ms = jnp.mean(x * x, axis=-1, keepdims=True)
        xn = ((x * lax.rsqrt(ms + EPS)) * n2_ref[...]).astype(BF16)
        y = x
        for col, (wg, wu, wd) in enumerate(((wga_ref, wua_ref, wda_ref), (wgb_ref, wub_ref, wdb_ref))):
            gate = _dot(xn, wg[...])
            up = _dot(xn, wu[...])
            hid = (gate * _sigmoid(gate) * up).astype(BF16)
            y = y + xs_ref[:, d_model + col:d_model + col + 1] * _dot(hid, wd[...])
        ys_ref[...] = y

    @pl.when(i >= nu_ref[0])
    def _():
        ys_ref[...] = jnp.zeros_like(ys_ref)


def _moe(xs, tile_ea, tile_eb, n_used, n2, weg, weu, wed, *, tm):
    n_sorted, W = xs.shape
    D = n2.shape[1]
    de = weg.shape[2]
    row = lambda i, ea, eb, nu: (jnp.minimum(i, nu[0] - 1), 0)
    wa = lambda i, ea, eb, nu: (ea[i], 0, 0)
    wb = lambda i, ea, eb, nu: (eb[i], 0, 0)
    grid_spec = pltpu.PrefetchScalarGridSpec(
        num_scalar_prefetch=3,
        grid=(n_sorted // tm,),
        in_specs=[pl.BlockSpec((tm, W), row),
                  pl.BlockSpec((1, D), lambda i, ea, eb, nu: (0, 0)),
                  pl.BlockSpec((None, D, de), wa), pl.BlockSpec((None, D, de), wa),
                  pl.BlockSpec((None, de, D), wa),
                  pl.BlockSpec((None, D, de), wb), pl.BlockSpec((None, D, de), wb),
                  pl.BlockSpec((None, de, D), wb)],
        out_specs=pl.BlockSpec((tm, D), lambda i, ea, eb, nu: (i, 0)),
    )
    return pl.pallas_call(
        functools.partial(_moe_kernel, d_model=D),
        grid_spec=grid_spec,
        out_shape=jax.ShapeDtypeStruct((n_sorted, D), F32),
        compiler_params=_cparams(("arbitrary",)),
        name="experts",
    )(tile_ea, tile_eb, n_used, xs, n2, weg, weu, wed, weg, weu, wed)


def _row_tile(T, cap):
    t = cap
    while T % t:
        t //= 2
    return t


def kernel(x, meta_tokens, norm1_w, w_in, w_gate_up, b_gate, gla_norm_w, w_pool_grp, pool_scale,
           w_br_gla, w_br_pool, w_out, norm2_w, w_router_group, w_router_expert, w_exp_gate,
           w_exp_up, w_exp_down, final_norm_w):
    B, S, D = x.shape
    depth = w_in.shape[0]
    key = w_gate_up.shape[2]
    rank = w_gate_up.shape[1]
    dv = gla_norm_w.shape[1]
    val = GLA_HEADS * dv
    dk = key // GLA_HEADS
    assert S % CHUNK == 0 and key * 2 == D and val == D
    assert w_pool_grp.shape[2] * len(POOL_WINDOWS) == D
    LP = PAD + N_META + S
    n_chunks = LP // CHUNK
    T = B * LP
    tm_proj = _row_tile(T, 512)
    tm_mix = _row_tile(T, 512)
    tm_row = _row_tile(T, 1024)
    tm_exp = 256
    n_sorted = -(-(T + N_CLASSES * (tm_exp - 1)) // tm_exp) * tm_exp
    n_tiles = n_sorted // tm_exp

    meta = jnp.broadcast_to(meta_tokens[None].astype(F32), (B, N_META, D))
    h = jnp.concatenate([jnp.zeros((B, PAD, D), F32), meta, x.astype(F32)], axis=1).reshape(T, D)

    c_q, c_k, c_v, c_og = 0, key, 2 * key, 2 * key + val
    c_gl = c_og + val
    c_u = c_gl + rank
    c_gt = c_u + D
    n_main = 2 * key + 2 * val + D + 2 * D

    ys = pos = None
    for l in range(depth):
        if l > 0:
            h = _collect(ys, pos, final_norm_w[None], tm=tm_row, final_norm=False)
        wl = w_in[l]
        w_all = jnp.concatenate(
            [wl[:, c_q:c_gl], wl[:, c_u:], wl[:, c_gl:c_u],
             jnp.zeros((D, LANES - rank), F32)], axis=1).astype(BF16)
        z, zg = _inproj(h, norm1_w[l][None], w_all, tm=tm_proj, n_main=n_main)

        wgu = jnp.concatenate([w_gate_up[l], jnp.zeros((LANES - rank, key), F32)], axis=0)
        ygla = _gla(z, zg, wgu, b_gate[l][None], gla_norm_w[l][None],
                    batch=B, n_chunks=n_chunks, dk=dk, dv=dv)

        wr = jnp.concatenate([w_router_group[l], w_router_expert[l],
                              jnp.zeros((D, CLASS_ROWS - N_GROUPS - N_EXPERTS), F32)], axis=1).T
        hx, info, counts = _mix(
            ygla, z, h, w_pool_grp[l].astype(BF16), pool_scale[l][None],
            w_br_gla[l].astype(BF16), w_br_pool[l].astype(BF16), w_out[l].astype(BF16),
            norm2_w[l][None], wr, tm=tm_mix, seq_rows=LP)

        cnt = counts[:N_CLASSES, 0].astype(jnp.int32)
        seg_tiles = (cnt + tm_exp - 1) // tm_exp
        tile_end = jnp.cumsum(seg_tiles)
        start = (tile_end - seg_tiles) * tm_exp
        n_used = tile_end[-1:]
        tile_cls = jnp.minimum(
            jnp.sum(jnp.arange(n_tiles)[:, None] >= tile_end[None, :], axis=1), N_CLASSES - 1)
        tile_cls = jnp.where(jnp.arange(n_tiles) < n_used[0], tile_cls,
                             jnp.take(tile_cls, jnp.maximum(n_used[0] - 1, 0)))
        pair_lo = jnp.array([p[0] for p in PAIRS], jnp.int32)
        pair_hi = jnp.array([p[1] for p in PAIRS], jnp.int32)
        grp = tile_cls // len(PAIRS)
        tile_ea = (grp * EXPERTS_PER_GROUP + jnp.take(pair_lo, tile_cls % len(PAIRS))).astype(jnp.int32)
        tile_eb = (grp * EXPERTS_PER_GROUP + jnp.take(pair_hi, tile_cls % len(PAIRS))).astype(jnp.int32)
        cls_t = info[0].astype(jnp.int32)
        pos = jnp.take(start, cls_t) + info[1].astype(jnp.int32)

        xs = _dispatch(hx, pos, tile_end.astype(jnp.int32), n_used.astype(jnp.int32),
                       tm=tm_row, tm_exp=tm_exp, n_sorted=n_sorted)
        ys = _moe(xs, tile_ea, tile_eb, n_used.astype(jnp.int32), norm2_w[l][None],
                  w_exp_gate[l].astype(BF16), w_exp_up[l].astype(BF16), w_exp_down[l].astype(BF16),
                  tm=tm_exp)

    pos_out = pos.reshape(B, LP)[:, PAD + N_META:].reshape(B * S)
    out = _collect(ys, pos_out, final_norm_w[None], tm=_row_tile(B * S, 1024), final_norm=True)
    return out.reshape(B, S, D)
```

```python
import functools

import jax
import jax.numpy as jnp
from jax import lax
from jax.experimental import pallas as pl
from jax.experimental.pallas import tpu as pltpu

F32 = jnp.float32
BF16 = jnp.bfloat16

EPS = 1e-6
CHUNK = 64
N_META = 16
PAD = CHUNK - N_META
GLA_HEADS = 4
GATE_NORMALIZER = 16.0
POOL_WINDOWS = (2, 4, 8, 16)
N_GROUPS = 4
EXPERTS_PER_GROUP = 4
N_EXPERTS = N_GROUPS * EXPERTS_PER_GROUP
PAIRS = ((0, 1), (0, 2), (0, 3), (1, 2), (1, 3), (2, 3))
N_CLASSES = N_GROUPS * len(PAIRS)
CLASS_ROWS = 32
LANES = 128
HALO = 32
ROUTE_LANES = 128
VMEM_LIMIT = 56 * 1024 * 1024


def _cparams(sem):
    return pltpu.CompilerParams(dimension_semantics=sem, vmem_limit_bytes=VMEM_LIMIT)


def _resident(shape):
    nd = len(shape)
    return pl.BlockSpec(shape, lambda *_: (0,) * nd, pipeline_mode=pl.Buffered(1))


def _split(x):
    hi = x.astype(BF16)
    lo = (x - hi.astype(F32)).astype(BF16)
    return hi, lo


def _dot(a, b):
    return jnp.dot(a, b, preferred_element_type=F32)


def _dot_nt(a, b):
    return lax.dot_general(a, b, (((1,), (1,)), ((), ())), preferred_element_type=F32)


def _dot_tn(a, b):
    return lax.dot_general(a, b, (((0,), (0,)), ((), ())), preferred_element_type=F32)


def _sigmoid(x):
    return 1.0 / (1.0 + jnp.exp(-x))


def _inproj_kernel(h_ref, nw_ref, w_ref, z_ref, zg_ref, xn_ref, *, n_main, col_blk):
    x = h_ref[...]
    ms = jnp.mean(x * x, axis=-1, keepdims=True)
    xn_ref[...] = ((x * lax.rsqrt(ms + EPS)) * nw_ref[...]).astype(BF16)
    for j in range(n_main // col_blk):
        sl = slice(j * col_blk, (j + 1) * col_blk)
        z_ref[:, sl] = _dot(xn_ref[...], w_ref[:, sl]).astype(BF16)
    zg_ref[...] = _dot(xn_ref[...], w_ref[:, n_main:])


def _inproj(h, norm_w, w_all, *, tm, n_main):
    T, D = h.shape
    n_all = w_all.shape[1]
    return pl.pallas_call(
        functools.partial(_inproj_kernel, n_main=n_main, col_blk=512),
        grid=(T // tm,),
        in_specs=[pl.BlockSpec((tm, D), lambda i: (i, 0)),
                  _resident((1, D)),
                  _resident((D, n_all))],
        out_specs=[pl.BlockSpec((tm, n_main), lambda i: (i, 0)),
                   pl.BlockSpec((tm, n_all - n_main), lambda i: (i, 0))],
        out_shape=[jax.ShapeDtypeStruct((T, n_main), BF16),
                   jax.ShapeDtypeStruct((T, n_all - n_main), F32)],
        scratch_shapes=[pltpu.VMEM((tm, D), BF16)],
        compiler_params=_cparams(("parallel",)),
        name="inproj",
    )(h, norm_w, w_all)


def _row_gather(src_ref, pos_ref, dst_ref, sem, tm):
    for r in range(tm):
        pltpu.make_async_copy(src_ref.at[pl.ds(pos_ref[r], 1)], dst_ref.at[pl.ds(r, 1)], sem).start()


def _row_gather_wait(src_ref, dst_ref, sem, tm):
    pltpu.make_async_copy(src_ref.at[pl.ds(0, tm)], dst_ref, sem).wait()


def _inproj_gather_kernel(pos0_ref, posn_ref, ys_ref, nw_ref, w_ref, z_ref, zg_ref, h_ref,
                          buf_ref, xn_ref, sem, *, tm, n_main, col_blk):
    i = pl.program_id(0)
    last = pl.num_programs(0) - 1
    slot = i % 2

    @pl.when(i == 0)
    def _():
        _row_gather(ys_ref, pos0_ref, buf_ref.at[0], sem.at[0], tm)

    _row_gather_wait(ys_ref, buf_ref.at[slot], sem.at[slot], tm)
    x = buf_ref[slot]
    h_ref[...] = x
    ms = jnp.mean(x * x, axis=-1, keepdims=True)
    xn_ref[...] = ((x * lax.rsqrt(ms + EPS)) * nw_ref[...]).astype(BF16)
    _row_gather(ys_ref, posn_ref, buf_ref.at[1 - slot], sem.at[1 - slot], tm)
    for j in range(n_main // col_blk):
        sl = slice(j * col_blk, (j + 1) * col_blk)
        z_ref[:, sl] = _dot(xn_ref[...], w_ref[:, sl]).astype(BF16)
    zg_ref[...] = _dot(xn_ref[...], w_ref[:, n_main:])

    @pl.when(i == last)
    def _():
        _row_gather_wait(ys_ref, buf_ref.at[1 - slot], sem.at[1 - slot], tm)


def _inproj_gather(ys, pos, norm_w, w_all, *, tm, n_main):
    T = pos.shape[0]
    D = ys.shape[1]
    n_all = w_all.shape[1]
    n_steps = T // tm
    return pl.pallas_call(
        functools.partial(_inproj_gather_kernel, tm=tm, n_main=n_main, col_blk=512),
        grid=(n_steps,),
        in_specs=[pl.BlockSpec((tm,), lambda i: (0,), memory_space=pltpu.SMEM),
                  pl.BlockSpec((tm,), lambda i: (jnp.minimum(i + 1, n_steps - 1),),
                               memory_space=pltpu.SMEM),
                  pl.BlockSpec(memory_space=pl.ANY),
                  _resident((1, D)),
                  _resident((D, n_all))],
        out_specs=[pl.BlockSpec((tm, n_main), lambda i: (i, 0)),
                   pl.BlockSpec((tm, n_all - n_main), lambda i: (i, 0)),
                   pl.BlockSpec((tm, D), lambda i: (i, 0))],
        out_shape=[jax.ShapeDtypeStruct((T, n_main), BF16),
                   jax.ShapeDtypeStruct((T, n_all - n_main), F32),
                   jax.ShapeDtypeStruct((T, D), F32)],
        scratch_shapes=[pltpu.VMEM((2, tm, D), F32), pltpu.VMEM((tm, D), BF16),
                        pltpu.SemaphoreType.DMA((2,))],
        compiler_params=_cparams(("arbitrary",)),
        name="inproj_gather",
    )(pos, pos, ys, norm_w, w_all)


def _gla_prep(q_ref, k_ref, zg_ref, wgh_ref, wgl_ref, bg_ref, scaled_ref, dec_ref, slot, first,
              *, bb, dk):
    rows = bb * CHUNK
    key = GLA_HEADS * dk
    gl_hi, gl_lo = _split(zg_ref[...].reshape(rows, LANES))
    gpre = (_dot(gl_hi, wgh_ref[...]) + _dot(gl_hi, wgl_ref[...]) + _dot(gl_lo, wgh_ref[...])
            + bg_ref[...])
    g = (jnp.minimum(gpre, 0.0) - jnp.log(1.0 + jnp.exp(-jnp.abs(gpre)))) * (1.0 / GATE_NORMALIZER)
    if first:
        row = lax.broadcasted_iota(jnp.int32, (rows, 1), 0)
        g = jnp.where((row & (CHUNK - 1)) < PAD, 0.0, g)

    ri = lax.broadcasted_iota(jnp.int32, (rows, rows), 0)
    ci = lax.broadcasted_iota(jnp.int32, (rows, rows), 1)
    tri = jnp.where(jnp.logical_and((ri // CHUNK) == (ci // CHUNK), ri >= ci), 1.0, 0.0).astype(BF16)
    g_hi, g_lo = _split(g)
    bcum = _dot(tri, g_hi) + _dot(tri, g_lo)
    gam = jnp.concatenate(
        [jnp.broadcast_to(bcum[(c + 1) * CHUNK - 1:(c + 1) * CHUNK, :], (CHUNK, key)) for c in range(bb)],
        axis=0)
    eb = jnp.exp(bcum)
    ieb = jnp.exp(-bcum)
    q = q_ref[...].reshape(rows, key).astype(F32) * (dk ** -0.5)
    k = k_ref[...].reshape(rows, key).astype(F32)
    scaled_ref[slot, 0] = (q * eb).astype(BF16)
    scaled_ref[slot, 1] = (q * ieb).astype(BF16)
    scaled_ref[slot, 2] = (k * eb).astype(BF16)
    scaled_ref[slot, 3] = (k * ieb).astype(BF16)
    scaled_ref[slot, 4] = (k * jnp.exp(gam - bcum)).astype(BF16)
    for c in range(bb):
        dec_ref[slot, c] = eb[(c + 1) * CHUNK - 1:(c + 1) * CHUNK, :]


def _gla_kernel(q0_ref, k0_ref, zg0_ref, qn_ref, kn_ref, zgn_ref, v_ref, wgh_ref, wgl_ref, bg_ref,
                o_ref, st_ref, scaled_ref, dec_ref, *, bb, dk, dv):
    n = pl.program_id(1)
    prep = functools.partial(_gla_prep, wgh_ref=wgh_ref, wgl_ref=wgl_ref, bg_ref=bg_ref,
                             scaled_ref=scaled_ref, dec_ref=dec_ref, bb=bb, dk=dk)

    @pl.when(n == 0)
    def _():
        st_ref[...] = jnp.zeros_like(st_ref)
        prep(q0_ref, k0_ref, zg0_ref, slot=0, first=True)

    li = lax.broadcasted_iota(jnp.int32, (CHUNK, CHUNK), 0)
    lj = lax.broadcasted_iota(jnp.int32, (CHUNK, CHUNK), 1)
    lower = li >= lj
    groups = [(c, hd) for c in range(bb) for hd in range(GLA_HEADS)]

    def step(slot):
        prep(qn_ref, kn_ref, zgn_ref, slot=1 - slot, first=False)
        attn = {}
        for c, hd in groups:
            rs = slice(c * CHUNK, (c + 1) * CHUNK)
            ks = slice(hd * dk, (hd + 1) * dk)
            a_lo = _dot_nt(scaled_ref[slot, 0, rs, ks], scaled_ref[slot, 3, rs, ks])
            a_up = _dot_nt(scaled_ref[slot, 1, rs, ks], scaled_ref[slot, 2, rs, ks])
            attn[c, hd] = jnp.where(lower, a_lo, a_up).astype(BF16)
        for c, hd in groups:
            rs = slice(c * CHUNK, (c + 1) * CHUNK)
            ks = slice(hd * dk, (hd + 1) * dk)
            vs = slice(hd * dv, (hd + 1) * dv)
            o_ref[c, :, vs] = (_dot(attn[c, hd], v_ref[c, :, vs])
                               + _dot(scaled_ref[slot, 0, rs, ks], st_ref[c, hd].astype(BF16))
                               ).astype(BF16)
        for c, hd in groups:
            rs = slice(c * CHUNK, (c + 1) * CHUNK)
            ks = slice(hd * dk, (hd + 1) * dk)
            vs = slice(hd * dv, (hd + 1) * dv)
            dec = jnp.broadcast_to(dec_ref[slot, c, :, ks], (dk, dk)).T
            st_ref[c, hd] = (st_ref[c, hd] * jnp.concatenate([dec] * (dv // dk), axis=1)
                             + _dot_tn(scaled_ref[slot, 4, rs, ks], v_ref[c, :, vs]))

    for slot in (0, 1):
        pl.when(n % 2 == slot)(functools.partial(step, slot))


def _gla(z, zg, wgu, bg, *, batch, n_chunks, dk, dv):
    T = z.shape[0]
    key = GLA_HEADS * dk
    val = GLA_HEADS * dv
    bb = 4 if batch % 4 == 0 else 1
    z3 = z.reshape(batch, n_chunks * CHUNK, z.shape[1])
    zg3 = zg.reshape(batch, n_chunks * CHUNK, zg.shape[1])
    wg_hi, wg_lo = _split(wgu)
    nxt = lambda n: jnp.minimum(n + 1, n_chunks - 1)
    o = pl.pallas_call(
        functools.partial(_gla_kernel, bb=bb, dk=dk, dv=dv),
        grid=(batch // bb, n_chunks),
        in_specs=[pl.BlockSpec((bb, CHUNK, key), lambda b, n: (b, 0, 0)),
                  pl.BlockSpec((bb, CHUNK, key), lambda b, n: (b, 0, 1)),
                  pl.BlockSpec((bb, CHUNK, LANES), lambda b, n: (b, 0, 0)),
                  pl.BlockSpec((bb, CHUNK, key), lambda b, n: (b, nxt(n), 0)),
                  pl.BlockSpec((bb, CHUNK, key), lambda b, n: (b, nxt(n), 1)),
                  pl.BlockSpec((bb, CHUNK, LANES), lambda b, n: (b, nxt(n), 0)),
                  pl.BlockSpec((bb, CHUNK, val), lambda b, n: (b, n, 1)),
                  _resident(wg_hi.shape), _resident(wg_lo.shape), _resident(bg.shape)],
        out_specs=pl.BlockSpec((bb, CHUNK, val), lambda b, n: (b, n, 0)),
        out_shape=jax.ShapeDtypeStruct((batch, n_chunks * CHUNK, val), BF16),
        scratch_shapes=[pltpu.VMEM((bb, GLA_HEADS, dk, dv), F32),
                        pltpu.VMEM((2, 5, bb * CHUNK, key), BF16),
                        pltpu.VMEM((2, bb, 1, key), F32)],
        compiler_params=_cparams(("parallel", "arbitrary")),
        name="gla",
    )(z3, z3, zg3, z3, z3, zg3, z3, wg_hi, wg_lo, bg)
    return o.reshape(T, val)


def _mix_kernel(ogla_ref, og_ref, u_ref, halo_ref, g0_ref, g1_ref, h_ref, gnw_ref,
                wgrp_ref, pscale_ref, wbg_ref, wbp_ref, wout_ref, n2_ref, wr_ref,
                hx_ref, info_ref, cnt_ref,
                ue_ref, p2_ref, p4_ref, p8_ref, p16_ref, tri_ref, carry_ref,
                *, tm, seq_rows, d_model):
    i = pl.program_id(0)
    gd = d_model // len(POOL_WINDOWS)

    @pl.when(i == 0)
    def _():
        r = lax.broadcasted_iota(jnp.int32, (tm, tm), 0)
        c = lax.broadcasted_iota(jnp.int32, (tm, tm), 1)
        tri_ref[...] = jnp.where(r < c, 1.0, 0.0).astype(BF16)
        carry_ref[...] = jnp.zeros_like(carry_ref)

    rowf = (i * tm + lax.broadcasted_iota(jnp.int32, (tm, 1), 0)).astype(F32)
    lp = rowf - jnp.floor(rowf / float(seq_rows)) * float(seq_rows)
    is_pad = lp < float(PAD)
    seqpos1 = lp - float(PAD - 1)

    ue_ref[0:HALO, :] = halo_ref[...].astype(F32)
    ue_ref[HALO:, :] = u_ref[...].astype(F32)
    n8 = tm + HALO - 8
    p2_ref[8:, :] = ue_ref[8:, :] + ue_ref[pl.ds(7, n8), :]
    p4_ref[16:, :] = p2_ref[16:, gd:] + p2_ref[pl.ds(14, n8 - 8), gd:]
    p8_ref[24:, :] = p4_ref[24:, gd:] + p4_ref[pl.ds(20, n8 - 16), gd:]
    p16_ref[HALO:, :] = p8_ref[HALO:, gd:] + p8_ref[pl.ds(24, tm), gd:]
    sums = (p2_ref[HALO:, 0:gd], p4_ref[HALO:, 0:gd], p8_ref[HALO:, 0:gd], p16_ref[HALO:, :])

    ypool = []
    for gi, w in enumerate(POOL_WINDOWS):
        cs = slice(gi * gd, (gi + 1) * gd)
        cnt = jnp.clip(seqpos1, 1.0, float(w))
        pooled = sums[gi] / cnt - ue_ref[HALO:, cs]
        mixed = _dot(pooled.astype(BF16), wgrp_ref[gi]) * pscale_ref[:, cs]
        ypool.append(mixed.astype(BF16))
    ypool = jnp.concatenate(ypool, axis=1)

    dv = gnw_ref.shape[1]
    ygla = []
    for hd in range(GLA_HEADS):
        vs = slice(hd * dv, (hd + 1) * dv)
        o = ogla_ref[:, vs].astype(F32)
        o = o * lax.rsqrt(jnp.mean(o * o, axis=-1, keepdims=True) + EPS) * gnw_ref[...]
        og = og_ref[:, vs].astype(F32)
        ygla.append((o * (og * _sigmoid(og))).astype(BF16))
    br_g = _dot(jnp.concatenate(ygla, axis=1), wbg_ref[...])
    br_p = _dot(ypool, wbp_ref[...])
    merged = (_sigmoid(g0_ref[...].astype(F32)) * br_g + _sigmoid(g1_ref[...].astype(F32)) * br_p)
    delta = _dot(merged.astype(BF16), wout_ref[...])
    h_new = h_ref[...] + jnp.where(is_pad, 0.0, delta)
    hx_ref[:, 0:d_model] = h_new

    ms = jnp.mean(h_new * h_new, axis=-1, keepdims=True)
    xn = (h_new * lax.rsqrt(ms + EPS)) * n2_ref[...]
    x_hi, x_lo = _split(xn)
    w_hi, w_lo = _split(wr_ref[...])
    lt = _dot_nt(w_hi, x_hi) + _dot_nt(w_hi, x_lo) + _dot_nt(w_lo, x_hi)

    lg = [lt[j:j + 1, :] for j in range(N_GROUPS)]
    m = jnp.maximum(jnp.maximum(lg[0], lg[1]), jnp.maximum(lg[2], lg[3]))
    gidx = jnp.where(lg[0] == m, 0, jnp.where(lg[1] == m, 1, jnp.where(lg[2] == m, 2, 3)))
    pg_top = 1.0 / (jnp.exp(lg[0] - m) + jnp.exp(lg[1] - m) + jnp.exp(lg[2] - m) + jnp.exp(lg[3] - m))
    sel = []
    for e in range(EXPERTS_PER_GROUP):
        rows = [lt[N_GROUPS + g * EXPERTS_PER_GROUP + e:N_GROUPS + g * EXPERTS_PER_GROUP + e + 1, :]
                for g in range(N_GROUPS)]
        sel.append(jnp.where(gidx == 0, rows[0], jnp.where(gidx == 1, rows[1],
                                                           jnp.where(gidx == 2, rows[2], rows[3]))))
    m1 = jnp.maximum(jnp.maximum(sel[0], sel[1]), jnp.maximum(sel[2], sel[3]))
    i1 = jnp.where(sel[0] == m1, 0, jnp.where(sel[1] == m1, 1, jnp.where(sel[2] == m1, 2, 3)))
    neg = jnp.float32(-jnp.inf)
    rest = [jnp.where(i1 == e, neg, sel[e]) for e in range(EXPERTS_PER_GROUP)]
    m2 = jnp.maximum(jnp.maximum(rest[0], rest[1]), jnp.maximum(rest[2], rest[3]))
    i2 = jnp.where(rest[0] == m2, 0, jnp.where(rest[1] == m2, 1, jnp.where(rest[2] == m2, 2, 3)))
    r21 = jnp.exp(m2 - m1)
    w1 = pg_top / (1.0 + r21)
    w2 = pg_top * r21 / (1.0 + r21)
    lo = jnp.minimum(i1, i2)
    hi = jnp.maximum(i1, i2)
    w_lo = jnp.where(i1 < i2, w1, w2)
    w_hi = jnp.where(i1 < i2, w2, w1)
    pidx = jnp.where(lo == 0, hi - 1, jnp.where(lo == 1, hi + 1, 5))
    cls = gidx * len(PAIRS) + pidx

    crow = lax.broadcasted_iota(jnp.int32, (CLASS_ROWS, tm), 0)
    onehot = jnp.where(crow == cls, 1.0, 0.0)
    prefix = _dot(onehot.astype(BF16), tri_ref[...]) + carry_ref[:, 0:1]
    rank = jnp.sum(onehot * prefix, axis=0, keepdims=True)
    carry_ref[...] = carry_ref[...] + jnp.sum(onehot, axis=1, keepdims=True)
    cnt_ref[...] = carry_ref[...]

    irow = lax.broadcasted_iota(jnp.int32, (8, tm), 0)
    info_ref[...] = jnp.where(irow == 0, cls.astype(F32), jnp.where(irow == 1, rank, 0.0))
    wrow = lax.broadcasted_iota(jnp.int32, (ROUTE_LANES, tm), 0)
    wrows = jnp.where(wrow == 0, w_lo, jnp.where(wrow == 1, w_hi, 0.0))
    hx_ref[:, d_model:] = wrows.T


def _mix(ogla, z, h, gnw, wgrp, pscale, wbg, wbp, wout, n2, wr, *, tm, seq_rows):
    T, D = h.shape
    nblk = tm // HALO
    return pl.pallas_call(
        functools.partial(_mix_kernel, tm=tm, seq_rows=seq_rows, d_model=D),
        grid=(T // tm,),
        in_specs=[pl.BlockSpec((tm, D), lambda i: (i, 0)),
                  pl.BlockSpec((tm, D), lambda i: (i, 2)),
                  pl.BlockSpec((tm, D), lambda i: (i, 3)),
                  pl.BlockSpec((HALO, D), lambda i: (jnp.maximum(i * nblk - 1, 0), 3)),
                  pl.BlockSpec((tm, D), lambda i: (i, 4)),
                  pl.BlockSpec((tm, D), lambda i: (i, 5)),
                  pl.BlockSpec((tm, D), lambda i: (i, 0)),
                  _resident(gnw.shape),
                  _resident(wgrp.shape), _resident(pscale.shape), _resident(wbg.shape),
                  _resident(wbp.shape), _resident(wout.shape), _resident(n2.shape),
                  _resident(wr.shape)],
        out_specs=[pl.BlockSpec((tm, D + ROUTE_LANES), lambda i: (i, 0)),
                   pl.BlockSpec((8, tm), lambda i: (0, i)),
                   pl.BlockSpec((CLASS_ROWS, LANES), lambda i: (0, 0))],
        out_shape=[jax.ShapeDtypeStruct((T, D + ROUTE_LANES), F32),
                   jax.ShapeDtypeStruct((8, T), F32),
                   jax.ShapeDtypeStruct((CLASS_ROWS, LANES), F32)],
        scratch_shapes=[pltpu.VMEM((tm + HALO, D), F32),
                        pltpu.VMEM((tm + HALO, D), F32),
                        pltpu.VMEM((tm + HALO, D - D // 4), F32),
                        pltpu.VMEM((tm + HALO, D - 2 * (D // 4)), F32),
                        pltpu.VMEM((tm + HALO, D // 4), F32),
                        pltpu.VMEM((tm, tm), BF16),
                        pltpu.VMEM((CLASS_ROWS, LANES), F32)],
        compiler_params=_cparams(("arbitrary",)),
        name="mix",
    )(ogla, z, z, z, z, z, h, gnw, wgrp, pscale, wbg, wbp, wout, n2, wr)


def _dispatch_kernel(tend_ref, nu_ref, pos_ref, hx_ref, xs_ref, zero_ref, sem, zsem,
                     *, tm, tm_exp, n_tiles):
    def fill(j):
        return pltpu.make_async_copy(zero_ref, xs_ref.at[pl.ds(j * tm_exp, tm_exp)], zsem)

    @pl.when(pl.program_id(0) == 0)
    def _():
        zero_ref[...] = jnp.zeros_like(zero_ref)
        for wait in (False, True):
            for c in range(N_CLASSES):
                prev = tend_ref[c - 1] if c else 0

                @pl.when(tend_ref[c] > prev)
                def _():
                    cp = fill(tend_ref[c] - 1)
                    cp.wait() if wait else cp.start()

            def tail(j, carry):
                cp = fill(j)
                cp.wait() if wait else cp.start()
                return carry
            lax.fori_loop(nu_ref[0], n_tiles, tail, 0)

    for r in range(tm):
        pltpu.make_async_copy(hx_ref.at[pl.ds(r, 1)], xs_ref.at[pl.ds(pos_ref[r], 1)], sem).start()
    pltpu.make_async_copy(hx_ref, xs_ref.at[pl.ds(0, tm)], sem).wait()


def _dispatch(hx, pos, tile_end, n_used, *, tm, tm_exp, n_sorted):
    T, W = hx.shape
    grid_spec = pltpu.PrefetchScalarGridSpec(
        num_scalar_prefetch=2,
        grid=(T // tm,),
        in_specs=[pl.BlockSpec((tm,), lambda i, te, nu: (i,), memory_space=pltpu.SMEM),
                  pl.BlockSpec((tm, W), lambda i, te, nu: (i, 0))],
        out_specs=pl.BlockSpec(memory_space=pl.ANY),
        scratch_shapes=[pltpu.VMEM((tm_exp, W), F32),
                        pltpu.SemaphoreType.DMA(()), pltpu.SemaphoreType.DMA(())],
    )
    return pl.pallas_call(
        functools.partial(_dispatch_kernel, tm=tm, tm_exp=tm_exp, n_tiles=n_sorted // tm_exp),
        grid_spec=grid_spec,
        out_shape=jax.ShapeDtypeStruct((n_sorted, W), F32),
        compiler_params=_cparams(("arbitrary",)),
        name="dispatch",
    )(tile_end, n_used, pos, hx)


def _collect_norm_kernel(pos_ref, ys_ref, nw_ref, o_ref, buf_ref, sem, *, tm):
    _row_gather(ys_ref, pos_ref, buf_ref, sem, tm)
    _row_gather_wait(ys_ref, buf_ref, sem, tm)
    x = buf_ref[...]
    ms = jnp.mean(x * x, axis=-1, keepdims=True)
    o_ref[...] = (x * lax.rsqrt(ms + EPS)) * nw_ref[...]


def _collect_norm(ys, pos, norm_w, *, tm):
    n_out = pos.shape[0]
    D = ys.shape[1]
    return pl.pallas_call(
        functools.partial(_collect_norm_kernel, tm=tm),
        grid=(n_out // tm,),
        in_specs=[pl.BlockSpec((tm,), lambda i: (i,), memory_space=pltpu.SMEM),
                  pl.BlockSpec(memory_space=pl.ANY),
                  _resident(norm_w.shape)],
        out_specs=pl.BlockSpec((tm, D), lambda i: (i, 0)),
        out_shape=jax.ShapeDtypeStruct((n_out, D), F32),
        scratch_shapes=[pltpu.VMEM((tm, D), F32), pltpu.SemaphoreType.DMA(())],
        compiler_params=_cparams(("arbitrary",)),
        name="collect_norm",
    )(pos, ys, norm_w)


def _moe_kernel(ea_ref, eb_ref, nu_ref, xs_ref, n2_ref, wga_ref, wua_ref, wda_ref,
                wgb_ref, wub_ref, wdb_ref, ys_ref, *, d_model):
    i = pl.program_id(0)

    @pl.when(i < nu_ref[0])
    def _():
        x = xs_ref[:, 0:d_model]
        ms = jnp.mean(x * x, axis=-1, keepdims=True)
        xn = ((x * lax.rsqrt(ms + EPS)) * n2_ref[...]).astype(BF16)
        y = x
        for col, (wg, wu, wd) in enumerate(((wga_ref, wua_ref, wda_ref), (wgb_ref, wub_ref, wdb_ref))):
            gate = _dot(xn, wg[...])
            up = _dot(xn, wu[...])
            hid = (gate * _sigmoid(gate) * up).astype(BF16)
            y = y + xs_ref[:, d_model + col:d_model + col + 1] * _dot(hid, wd[...])
        ys_ref[...] = y

    @pl.when(i >= nu_ref[0])
    def _():
        ys_ref[...] = jnp.zeros_like(ys_ref)


def _moe(xs, tile_ea, tile_eb, n_used, n2, weg, weu, wed, *, tm):
    n_sorted, W = xs.shape
    D = n2.shape[1]
    de = weg.shape[2]
    row = lambda i, ea, eb, nu: (jnp.minimum(i, nu[0] - 1), 0)
    wa = lambda i, ea, eb, nu: (ea[i], 0, 0)
    wb = lambda i, ea, eb, nu: (eb[i], 0, 0)
    grid_spec = pltpu.PrefetchScalarGridSpec(
        num_scalar_prefetch=3,
        grid=(n_sorted // tm,),
        in_specs=[pl.BlockSpec((tm, W), row),
                  pl.BlockSpec((1, D), lambda i, ea, eb, nu: (0, 0)),
                  pl.BlockSpec((None, D, de), wa), pl.BlockSpec((None, D, de), wa),
                  pl.BlockSpec((None, de, D), wa),
                  pl.BlockSpec((None, D, de), wb), pl.BlockSpec((None, D, de), wb),
                  pl.BlockSpec((None, de, D), wb)],
        out_specs=pl.BlockSpec((tm, D), lambda i, ea, eb, nu: (i, 0)),
    )
    return pl.pallas_call(
        functools.partial(_moe_kernel, d_model=D),
        grid_spec=grid_spec,
        out_shape=jax.ShapeDtypeStruct((n_sorted, D), F32),
        compiler_params=_cparams(("arbitrary",)),
        name="experts",
    )(tile_ea, tile_eb, n_used, xs, n2, weg, weu, wed, weg, weu, wed)


def _row_tile(T, cap):
    t = cap
    while T % t:
        t //= 2
    return t


def kernel(x, meta_tokens, norm1_w, w_in, w_gate_up, b_gate, gla_norm_w, w_pool_grp, pool_scale,
           w_br_gla, w_br_pool, w_out, norm2_w, w_router_group, w_router_expert, w_exp_gate,
           w_exp_up, w_exp_down, final_norm_w):
    B, S, D = x.shape
    depth = w_in.shape[0]
    key = w_gate_up.shape[2]
    rank = w_gate_up.shape[1]
    dv = gla_norm_w.shape[1]
    val = GLA_HEADS * dv
    dk = key // GLA_HEADS
    assert S % CHUNK == 0 and key * 2 == D and val == D
    assert w_pool_grp.shape[2] * len(POOL_WINDOWS) == D
    LP = PAD + N_META + S
    n_chunks = LP // CHUNK
    T = B * LP
    tm_proj = _row_tile(T, 512)
    tm_mix = _row_tile(T, 512)
    tm_row = _row_tile(T, 512)
    tm_exp = 256
    n_sorted = -(-(T + N_CLASSES * (tm_exp - 1)) // tm_exp) * tm_exp
    n_tiles = n_sorted // tm_exp

    meta = jnp.broadcast_to(meta_tokens[None].astype(F32), (B, N_META, D))
    h = jnp.concatenate([jnp.zeros((B, PAD, D), F32), meta, x.astype(F32)], axis=1).reshape(T, D)

    c_q, c_k, c_v, c_og = 0, key, 2 * key, 2 * key + val
    c_gl = c_og + val
    c_u = c_gl + rank
    c_gt = c_u + D
    n_main = 2 * key + 2 * val + D + 2 * D

    ys = pos = None
    for l in range(depth):
        wl = w_in[l]
        w_all = jnp.concatenate(
            [wl[:, c_q:c_gl], wl[:, c_u:], wl[:, c_gl:c_u],
             jnp.zeros((D, LANES - rank), F32)], axis=1).astype(BF16)
        if l == 0:
            z, zg = _inproj(h, norm1_w[l][None], w_all, tm=tm_proj, n_main=n_main)
        else:
            z, zg, h = _inproj_gather(ys, pos, norm1_w[l][None], w_all, tm=tm_proj, n_main=n_main)

        wgu = jnp.concatenate([w_gate_up[l], jnp.zeros((LANES - rank, key), F32)], axis=0)
        ogla = _gla(z, zg, wgu, b_gate[l][None], batch=B, n_chunks=n_chunks, dk=dk, dv=dv)

        wr = jnp.concatenate([w_router_group[l], w_router_expert[l],
                              jnp.zeros((D, CLASS_ROWS - N_GROUPS - N_EXPERTS), F32)], axis=1).T
        hx, info, counts = _mix(
            ogla, z, h, gla_norm_w[l][None], w_pool_grp[l].astype(BF16), pool_scale[l][None],
            w_br_gla[l].astype(BF16), w_br_pool[l].astype(BF16), w_out[l].astype(BF16),
            norm2_w[l][None], wr, tm=tm_mix, seq_rows=LP)

        cnt = counts[:N_CLASSES, 0].astype(jnp.int32)
        seg_tiles = (cnt + tm_exp - 1) // tm_exp
        tile_end = jnp.cumsum(seg_tiles)
        start = (tile_end - seg_tiles) * tm_exp
        n_used = tile_end[-1:]
        tile_cls = jnp.minimum(
            jnp.sum(jnp.arange(n_tiles)[:, None] >= tile_end[None, :], axis=1), N_CLASSES - 1)
        tile_cls = jnp.where(jnp.arange(n_tiles) < n_used[0], tile_cls,
                             jnp.take(tile_cls, jnp.maximum(n_used[0] - 1, 0)))
        pair_lo = jnp.array([p[0] for p in PAIRS], jnp.int32)
        pair_hi = jnp.array([p[1] for p in PAIRS], jnp.int32)
        grp = tile_cls // len(PAIRS)
        tile_ea = (grp * EXPERTS_PER_GROUP + jnp.take(pair_lo, tile_cls % len(PAIRS))).astype(jnp.int32)
        tile_eb = (grp * EXPERTS_PER_GROUP + jnp.take(pair_hi, tile_cls % len(PAIRS))).astype(jnp.int32)
        cls_t = info[0].astype(jnp.int32)
        pos = jnp.take(start, cls_t) + info[1].astype(jnp.int32)

        xs = _dispatch(hx, pos, tile_end.astype(jnp.int32), n_used.astype(jnp.int32),
                       tm=tm_row, tm_exp=tm_exp, n_sorted=n_sorted)
        ys = _moe(xs, tile_ea, tile_eb, n_used.astype(jnp.int32), norm2_w[l][None],
                  w_exp_gate[l].astype(BF16), w_exp_up[l].astype(BF16), w_exp_down[l].astype(BF16),
                  tm=tm_exp)

    pos_out = pos.reshape(B, LP)[:, PAD + N_META:].reshape(B * S)
    out = _collect_norm(ys, pos_out, final_norm_w[None], tm=_row_tile(B * S, 512))
    return out.reshape(B, S, D)
```

```python
import functools

import jax
import jax.numpy as jnp
from jax import lax
from jax.experimental import pallas as pl
from jax.experimental.pallas import tpu as pltpu

F32 = jnp.float32
BF16 = jnp.bfloat16

EPS = 1e-6
CHUNK = 64
N_META = 16
PAD = CHUNK - N_META
GLA_HEADS = 4
GATE_NORMALIZER = 16.0
POOL_WINDOWS = (2, 4, 8, 16)
N_GROUPS = 4
EXPERTS_PER_GROUP = 4
N_EXPERTS = N_GROUPS * EXPERTS_PER_GROUP
PAIRS = ((0, 1), (0, 2), (0, 3), (1, 2), (1, 3), (2, 3))
N_CLASSES = N_GROUPS * len(PAIRS)
CLASS_ROWS = 32
LANES = 128
HALO = 32
ROUTE_LANES = 128
VMEM_LIMIT = 56 * 1024 * 1024


def _cparams(sem):
    return pltpu.CompilerParams(dimension_semantics=sem, vmem_limit_bytes=VMEM_LIMIT)


def _resident(shape):
    nd = len(shape)
    return pl.BlockSpec(shape, lambda *_: (0,) * nd, pipeline_mode=pl.Buffered(1))


def _split(x):
    hi = x.astype(BF16)
    lo = (x - hi.astype(F32)).astype(BF16)
    return hi, lo


def _dot(a, b):
    return jnp.dot(a, b, preferred_element_type=F32)


def _dot_nt(a, b):
    return lax.dot_general(a, b, (((1,), (1,)), ((), ())), preferred_element_type=F32)


def _dot_tn(a, b):
    return lax.dot_general(a, b, (((0,), (0,)), ((), ())), preferred_element_type=F32)


def _sigmoid(x):
    return 0.5 * jnp.tanh(0.5 * x) + 0.5


def _inproj_kernel(h_ref, nw_ref, w_ref, z_ref, zg_ref, xn_ref, *, n_main, col_blk):
    x = h_ref[...]
    ms = jnp.mean(x * x, axis=-1, keepdims=True)
    xn_ref[...] = ((x * lax.rsqrt(ms + EPS)) * nw_ref[...]).astype(BF16)
    for j in range(n_main // col_blk):
        sl = slice(j * col_blk, (j + 1) * col_blk)
        z_ref[:, sl] = _dot(xn_ref[...], w_ref[:, sl]).astype(BF16)
    zg_ref[...] = _dot(xn_ref[...], w_ref[:, n_main:])


def _inproj(h, norm_w, w_all, *, tm, n_main):
    T, D = h.shape
    n_all = w_all.shape[1]
    return pl.pallas_call(
        functools.partial(_inproj_kernel, n_main=n_main, col_blk=512),
        grid=(T // tm,),
        in_specs=[pl.BlockSpec((tm, D), lambda i: (i, 0)),
                  _resident((1, D)),
                  _resident((D, n_all))],
        out_specs=[pl.BlockSpec((tm, n_main), lambda i: (i, 0)),
                   pl.BlockSpec((tm, n_all - n_main), lambda i: (i, 0))],
        out_shape=[jax.ShapeDtypeStruct((T, n_main), BF16),
                   jax.ShapeDtypeStruct((T, n_all - n_main), F32)],
        scratch_shapes=[pltpu.VMEM((tm, D), BF16)],
        compiler_params=_cparams(("parallel",)),
        name="inproj",
    )(h, norm_w, w_all)


def _row_gather(src_ref, pos_ref, dst_ref, sem, tm):
    for r in range(tm):
        pltpu.make_async_copy(src_ref.at[pl.ds(pos_ref[r], 1)], dst_ref.at[pl.ds(r, 1)], sem).start()


def _row_gather_wait(src_ref, dst_ref, sem, tm):
    pltpu.make_async_copy(src_ref.at[pl.ds(0, tm)], dst_ref, sem).wait()


def _inproj_gather_kernel(pos0_ref, posn_ref, ys_ref, nw_ref, w_ref, z_ref, zg_ref, h_ref,
                          buf_ref, xn_ref, sem, *, tm, n_main, col_blk):
    i = pl.program_id(0)
    last = pl.num_programs(0) - 1
    slot = i % 2

    @pl.when(i == 0)
    def _():
        _row_gather(ys_ref, pos0_ref, buf_ref.at[0], sem.at[0], tm)

    _row_gather_wait(ys_ref, buf_ref.at[slot], sem.at[slot], tm)
    x = buf_ref[slot]
    h_ref[...] = x
    ms = jnp.mean(x * x, axis=-1, keepdims=True)
    xn_ref[...] = ((x * lax.rsqrt(ms + EPS)) * nw_ref[...]).astype(BF16)
    _row_gather(ys_ref, posn_ref, buf_ref.at[1 - slot], sem.at[1 - slot], tm)
    for j in range(n_main // col_blk):
        sl = slice(j * col_blk, (j + 1) * col_blk)
        z_ref[:, sl] = _dot(xn_ref[...], w_ref[:, sl]).astype(BF16)
    zg_ref[...] = _dot(xn_ref[...], w_ref[:, n_main:])

    @pl.when(i == last)
    def _():
        _row_gather_wait(ys_ref, buf_ref.at[1 - slot], sem.at[1 - slot], tm)


def _inproj_gather(ys, pos, norm_w, w_all, *, tm, n_main):
    T = pos.shape[0]
    D = ys.shape[1]
    n_all = w_all.shape[1]
    n_steps = T // tm
    return pl.pallas_call(
        functools.partial(_inproj_gather_kernel, tm=tm, n_main=n_main, col_blk=512),
        grid=(n_steps,),
        in_specs=[pl.BlockSpec((tm,), lambda i: (0,), memory_space=pltpu.SMEM),
                  pl.BlockSpec((tm,), lambda i: (jnp.minimum(i + 1, n_steps - 1),),
                               memory_space=pltpu.SMEM),
                  pl.BlockSpec(memory_space=pl.ANY),
                  _resident((1, D)),
                  _resident((D, n_all))],
        out_specs=[pl.BlockSpec((tm, n_main), lambda i: (i, 0)),
                   pl.BlockSpec((tm, n_all - n_main), lambda i: (i, 0)),
                   pl.BlockSpec((tm, D), lambda i: (i, 0))],
        out_shape=[jax.ShapeDtypeStruct((T, n_main), BF16),
                   jax.ShapeDtypeStruct((T, n_all - n_main), F32),
                   jax.ShapeDtypeStruct((T, D), F32)],
        scratch_shapes=[pltpu.VMEM((2, tm, D), F32), pltpu.VMEM((tm, D), BF16),
                        pltpu.SemaphoreType.DMA((2,))],
        compiler_params=_cparams(("arbitrary",)),
        name="inproj_gather",
    )(pos, pos, ys, norm_w, w_all)


def _gla_prep(q_ref, k_ref, zg_ref, wgh_ref, wgl_ref, bg_ref, scaled_ref, dec_ref, slot, first,
              *, bb, dk):
    rows = bb * CHUNK
    key = GLA_HEADS * dk
    gl_hi, gl_lo = _split(zg_ref[...].reshape(rows, LANES))
    gpre = (_dot(gl_hi, wgh_ref[...]) + _dot(gl_hi, wgl_ref[...]) + _dot(gl_lo, wgh_ref[...])
            + bg_ref[...])
    g = (jnp.minimum(gpre, 0.0) - jnp.log(1.0 + jnp.exp(-jnp.abs(gpre)))) * (1.0 / GATE_NORMALIZER)
    if first:
        row = lax.broadcasted_iota(jnp.int32, (rows, 1), 0)
        g = jnp.where((row & (CHUNK - 1)) < PAD, 0.0, g)

    ri = lax.broadcasted_iota(jnp.int32, (rows, rows), 0)
    ci = lax.broadcasted_iota(jnp.int32, (rows, rows), 1)
    tri = jnp.where(jnp.logical_and((ri // CHUNK) == (ci // CHUNK), ri >= ci), 1.0, 0.0).astype(BF16)
    g_hi, g_lo = _split(g)
    bcum = _dot(tri, g_hi) + _dot(tri, g_lo)
    gam = jnp.concatenate(
        [jnp.broadcast_to(bcum[(c + 1) * CHUNK - 1:(c + 1) * CHUNK, :], (CHUNK, key)) for c in range(bb)],
        axis=0)
    eb = jnp.exp(bcum)
    ieb = jnp.exp(-bcum)
    q = q_ref[...].reshape(rows, key).astype(F32) * (dk ** -0.5)
    k = k_ref[...].reshape(rows, key).astype(F32)
    scaled_ref[slot, 0] = (q * eb).astype(BF16)
    scaled_ref[slot, 1] = (q * ieb).astype(BF16)
    scaled_ref[slot, 2] = (k * eb).astype(BF16)
    scaled_ref[slot, 3] = (k * ieb).astype(BF16)
    scaled_ref[slot, 4] = (k * jnp.exp(gam - bcum)).astype(BF16)
    for c in range(bb):
        dec_ref[slot, c] = eb[(c + 1) * CHUNK - 1:(c + 1) * CHUNK, :]


def _gla_kernel(q0_ref, k0_ref, zg0_ref, qn_ref, kn_ref, zgn_ref, v_ref, wgh_ref, wgl_ref, bg_ref,
                o_ref, st_ref, scaled_ref, dec_ref, *, bb, dk, dv):
    n = pl.program_id(1)
    prep = functools.partial(_gla_prep, wgh_ref=wgh_ref, wgl_ref=wgl_ref, bg_ref=bg_ref,
                             scaled_ref=scaled_ref, dec_ref=dec_ref, bb=bb, dk=dk)

    @pl.when(n == 0)
    def _():
        st_ref[...] = jnp.zeros_like(st_ref)
        prep(q0_ref, k0_ref, zg0_ref, slot=0, first=True)

    li = lax.broadcasted_iota(jnp.int32, (CHUNK, CHUNK), 0)
    lj = lax.broadcasted_iota(jnp.int32, (CHUNK, CHUNK), 1)
    lower = li >= lj
    groups = [(c, hd) for c in range(bb) for hd in range(GLA_HEADS)]

    def step(slot):
        prep(qn_ref, kn_ref, zgn_ref, slot=1 - slot, first=False)
        attn = {}
        for c, hd in groups:
            rs = slice(c * CHUNK, (c + 1) * CHUNK)
            ks = slice(hd * dk, (hd + 1) * dk)
            a_lo = _dot_nt(scaled_ref[slot, 0, rs, ks], scaled_ref[slot, 3, rs, ks])
            a_up = _dot_nt(scaled_ref[slot, 1, rs, ks], scaled_ref[slot, 2, rs, ks])
            attn[c, hd] = jnp.where(lower, a_lo, a_up).astype(BF16)
        for c, hd in groups:
            rs = slice(c * CHUNK, (c + 1) * CHUNK)
            ks = slice(hd * dk, (hd + 1) * dk)
            vs = slice(hd * dv, (hd + 1) * dv)
            o_ref[c, :, vs] = (_dot(attn[c, hd], v_ref[c, :, vs])
                               + _dot(scaled_ref[slot, 0, rs, ks], st_ref[c, hd].astype(BF16))
                               ).astype(BF16)
        for c, hd in groups:
            rs = slice(c * CHUNK, (c + 1) * CHUNK)
            ks = slice(hd * dk, (hd + 1) * dk)
            vs = slice(hd * dv, (hd + 1) * dv)
            dec = jnp.broadcast_to(dec_ref[slot, c, :, ks], (dk, dk)).T
            st_ref[c, hd] = (st_ref[c, hd] * jnp.concatenate([dec] * (dv // dk), axis=1)
                             + _dot_tn(scaled_ref[slot, 4, rs, ks], v_ref[c, :, vs]))

    for slot in (0, 1):
        pl.when(n % 2 == slot)(functools.partial(step, slot))


def _gla(z, zg, wgu, bg, *, batch, n_chunks, dk, dv):
    T = z.shape[0]
    key = GLA_HEADS * dk
    val = GLA_HEADS * dv
    bb = 4 if batch % 4 == 0 else 1
    z3 = z.reshape(batch, n_chunks * CHUNK, z.shape[1])
    zg3 = zg.reshape(batch, n_chunks * CHUNK, zg.shape[1])
    wg_hi, wg_lo = _split(wgu)
    nxt = lambda n: jnp.minimum(n + 1, n_chunks - 1)
    o = pl.pallas_call(
        functools.partial(_gla_kernel, bb=bb, dk=dk, dv=dv),
        grid=(batch // bb, n_chunks),
        in_specs=[pl.BlockSpec((bb, CHUNK, key), lambda b, n: (b, 0, 0)),
                  pl.BlockSpec((bb, CHUNK, key), lambda b, n: (b, 0, 1)),
                  pl.BlockSpec((bb, CHUNK, LANES), lambda b, n: (b, 0, 0)),
                  pl.BlockSpec((bb, CHUNK, key), lambda b, n: (b, nxt(n), 0)),
                  pl.BlockSpec((bb, CHUNK, key), lambda b, n: (b, nxt(n), 1)),
                  pl.BlockSpec((bb, CHUNK, LANES), lambda b, n: (b, nxt(n), 0)),
                  pl.BlockSpec((bb, CHUNK, val), lambda b, n: (b, n, 1)),
                  _resident(wg_hi.shape), _resident(wg_lo.shape), _resident(bg.shape)],
        out_specs=pl.BlockSpec((bb, CHUNK, val), lambda b, n: (b, n, 0)),
        out_shape=jax.ShapeDtypeStruct((batch, n_chunks * CHUNK, val), BF16),
        scratch_shapes=[pltpu.VMEM((bb, GLA_HEADS, dk, dv), F32),
                        pltpu.VMEM((2, 5, bb * CHUNK, key), BF16),
                        pltpu.VMEM((2, bb, 1, key), F32)],
        compiler_params=_cparams(("parallel", "arbitrary")),
        name="gla",
    )(z3, z3, zg3, z3, z3, zg3, z3, wg_hi, wg_lo, bg)
    return o.reshape(T, val)


def _mix_kernel(ogla_ref, og_ref, u_ref, halo_ref, g0_ref, g1_ref, h_ref, gnw_ref,
                wgrp_ref, pscale_ref, wbg_ref, wbp_ref, wout_ref, n2_ref, wr_ref,
                hx_ref, info_ref, cnt_ref,
                ue_ref, p2_ref, p4_ref, p8_ref, p16_ref, tri_ref, carry_ref,
                *, tm, seq_rows, d_model):
    i = pl.program_id(0)
    gd = d_model // len(POOL_WINDOWS)

    @pl.when(i == 0)
    def _():
        r = lax.broadcasted_iota(jnp.int32, (tm, tm), 0)
        c = lax.broadcasted_iota(jnp.int32, (tm, tm), 1)
        tri_ref[...] = jnp.where(r < c, 1.0, 0.0).astype(BF16)
        carry_ref[...] = jnp.zeros_like(carry_ref)

    rowf = (i * tm + lax.broadcasted_iota(jnp.int32, (tm, 1), 0)).astype(F32)
    lp = rowf - jnp.floor(rowf / float(seq_rows)) * float(seq_rows)
    is_pad = lp < float(PAD)
    seqpos1 = lp - float(PAD - 1)

    ue_ref[0:HALO, :] = halo_ref[...].astype(F32)
    ue_ref[HALO:, :] = u_ref[...].astype(F32)
    n8 = tm + HALO - 8
    p2_ref[8:, :] = ue_ref[8:, :] + ue_ref[pl.ds(7, n8), :]
    p4_ref[16:, :] = p2_ref[16:, gd:] + p2_ref[pl.ds(14, n8 - 8), gd:]
    p8_ref[24:, :] = p4_ref[24:, gd:] + p4_ref[pl.ds(20, n8 - 16), gd:]
    p16_ref[HALO:, :] = p8_ref[HALO:, gd:] + p8_ref[pl.ds(24, tm), gd:]
    sums = (p2_ref[HALO:, 0:gd], p4_ref[HALO:, 0:gd], p8_ref[HALO:, 0:gd], p16_ref[HALO:, :])

    ypool = []
    for gi, w in enumerate(POOL_WINDOWS):
        cs = slice(gi * gd, (gi + 1) * gd)
        cnt = jnp.clip(seqpos1, 1.0, float(w))
        pooled = sums[gi] / cnt - ue_ref[HALO:, cs]
        mixed = _dot(pooled.astype(BF16), wgrp_ref[gi]) * pscale_ref[:, cs]
        ypool.append(mixed.astype(BF16))
    ypool = jnp.concatenate(ypool, axis=1)

    dv = gnw_ref.shape[1]
    ygla = []
    for hd in range(GLA_HEADS):
        vs = slice(hd * dv, (hd + 1) * dv)
        o = ogla_ref[:, vs].astype(F32)
        o = o * lax.rsqrt(jnp.mean(o * o, axis=-1, keepdims=True) + EPS) * gnw_ref[...]
        og = og_ref[:, vs].astype(F32)
        ygla.append((o * (og * _sigmoid(og))).astype(BF16))
    br_g = _dot(jnp.concatenate(ygla, axis=1), wbg_ref[...])
    br_p = _dot(ypool, wbp_ref[...])
    merged = (_sigmoid(g0_ref[...].astype(F32)) * br_g + _sigmoid(g1_ref[...].astype(F32)) * br_p)
    delta = _dot(merged.astype(BF16), wout_ref[...])
    h_new = h_ref[...] + jnp.where(is_pad, 0.0, delta)
    hx_ref[:, 0:d_model] = h_new

    ms = jnp.mean(h_new * h_new, axis=-1, keepdims=True)
    xn = (h_new * lax.rsqrt(ms + EPS)) * n2_ref[...]
    x_hi, x_lo = _split(xn)
    w_hi, w_lo = _split(wr_ref[...])
    lt = _dot_nt(w_hi, x_hi) + _dot_nt(w_hi, x_lo) + _dot_nt(w_lo, x_hi)

    lg = [lt[j:j + 1, :] for j in range(N_GROUPS)]
    m = jnp.maximum(jnp.maximum(lg[0], lg[1]), jnp.maximum(lg[2], lg[3]))
    gidx = jnp.where(lg[0] == m, 0, jnp.where(lg[1] == m, 1, jnp.where(lg[2] == m, 2, 3)))
    pg_top = 1.0 / (jnp.exp(lg[0] - m) + jnp.exp(lg[1] - m) + jnp.exp(lg[2] - m) + jnp.exp(lg[3] - m))
    sel = []
    for e in range(EXPERTS_PER_GROUP):
        rows = [lt[N_GROUPS + g * EXPERTS_PER_GROUP + e:N_GROUPS + g * EXPERTS_PER_GROUP + e + 1, :]
                for g in range(N_GROUPS)]
        sel.append(jnp.where(gidx == 0, rows[0], jnp.where(gidx == 1, rows[1],
                                                           jnp.where(gidx == 2, rows[2], rows[3]))))
    m1 = jnp.maximum(jnp.maximum(sel[0], sel[1]), jnp.maximum(sel[2], sel[3]))
    i1 = jnp.where(sel[0] == m1, 0, jnp.where(sel[1] == m1, 1, jnp.where(sel[2] == m1, 2, 3)))
    neg = jnp.float32(-jnp.inf)
    rest = [jnp.where(i1 == e, neg, sel[e]) for e in range(EXPERTS_PER_GROUP)]
    m2 = jnp.maximum(jnp.maximum(rest[0], rest[1]), jnp.maximum(rest[2], rest[3]))
    i2 = jnp.where(rest[0] == m2, 0, jnp.where(rest[1] == m2, 1, jnp.where(rest[2] == m2, 2, 3)))
    r21 = jnp.exp(m2 - m1)
    w1 = pg_top / (1.0 + r21)
    w2 = pg_top * r21 / (1.0 + r21)
    lo = jnp.minimum(i1, i2)
    hi = jnp.maximum(i1, i2)
    w_lo = jnp.where(i1 < i2, w1, w2)
    w_hi = jnp.where(i1 < i2, w2, w1)
    pidx = jnp.where(lo == 0, hi - 1, jnp.where(lo == 1, hi + 1, 5))
    cls = gidx * len(PAIRS) + pidx

    crow = lax.broadcasted_iota(jnp.int32, (CLASS_ROWS, tm), 0)
    onehot = jnp.where(crow == cls, 1.0, 0.0)
    prefix = _dot(onehot.astype(BF16), tri_ref[...]) + carry_ref[:, 0:1]
    rank = jnp.sum(onehot * prefix, axis=0, keepdims=True)
    carry_ref[...] = carry_ref[...] + jnp.sum(onehot, axis=1, keepdims=True)
    cnt_ref[...] = carry_ref[...]

    irow = lax.broadcasted_iota(jnp.int32, (8, tm), 0)
    info_ref[...] = jnp.where(irow == 0, cls.astype(F32), jnp.where(irow == 1, rank, 0.0))
    wrow = lax.broadcasted_iota(jnp.int32, (ROUTE_LANES, tm), 0)
    wrows = jnp.where(wrow == 0, w_lo, jnp.where(wrow == 1, w_hi, 0.0))
    hx_ref[:, d_model:] = wrows.T


def _mix(ogla, z, h, gnw, wgrp, pscale, wbg, wbp, wout, n2, wr, *, tm, seq_rows):
    T, D = h.shape
    nblk = tm // HALO
    return pl.pallas_call(
        functools.partial(_mix_kernel, tm=tm, seq_rows=seq_rows, d_model=D),
        grid=(T // tm,),
        in_specs=[pl.BlockSpec((tm, D), lambda i: (i, 0)),
                  pl.BlockSpec((tm, D), lambda i: (i, 2)),
                  pl.BlockSpec((tm, D), lambda i: (i, 3)),
                  pl.BlockSpec((HALO, D), lambda i: (jnp.maximum(i * nblk - 1, 0), 3)),
                  pl.BlockSpec((tm, D), lambda i: (i, 4)),
                  pl.BlockSpec((tm, D), lambda i: (i, 5)),
                  pl.BlockSpec((tm, D), lambda i: (i, 0)),
                  _resident(gnw.shape),
                  _resident(wgrp.shape), _resident(pscale.shape), _resident(wbg.shape),
                  _resident(wbp.shape), _resident(wout.shape), _resident(n2.shape),
                  _resident(wr.shape)],
        out_specs=[pl.BlockSpec((tm, D + ROUTE_LANES), lambda i: (i, 0)),
                   pl.BlockSpec((8, tm), lambda i: (0, i)),
                   pl.BlockSpec((CLASS_ROWS, LANES), lambda i: (0, 0))],
        out_shape=[jax.ShapeDtypeStruct((T, D + ROUTE_LANES), F32),
                   jax.ShapeDtypeStruct((8, T), F32),
                   jax.ShapeDtypeStruct((CLASS_ROWS, LANES), F32)],
        scratch_shapes=[pltpu.VMEM((tm + HALO, D), F32),
                        pltpu.VMEM((tm + HALO, D), F32),
                        pltpu.VMEM((tm + HALO, D - D // 4), F32),
                        pltpu.VMEM((tm + HALO, D - 2 * (D // 4)), F32),
                        pltpu.VMEM((tm + HALO, D // 4), F32),
                        pltpu.VMEM((tm, tm), BF16),
                        pltpu.VMEM((CLASS_ROWS, LANES), F32)],
        compiler_params=_cparams(("arbitrary",)),
        name="mix",
    )(ogla, z, z, z, z, z, h, gnw, wgrp, pscale, wbg, wbp, wout, n2, wr)


def _invert_kernel(tend_ref, pos_ref, inv_ref, *, n_tok, tm_exp):
    for c in range(N_CLASSES):
        base = (tend_ref[c] - 1) * tm_exp

        @pl.when(tend_ref[c] > (tend_ref[c - 1] if c else 0))
        def _():
            def clear(p, carry):
                inv_ref[base + p] = 0
                return carry
            lax.fori_loop(0, tm_exp, clear, 0, unroll=16)

    def clear_tail(p, carry):
        inv_ref[p] = 0
        return carry
    lax.fori_loop(tend_ref[N_CLASSES - 1] * tm_exp, inv_ref.shape[0], clear_tail, 0)

    def put(t, carry):
        inv_ref[pos_ref[t]] = t
        return carry
    lax.fori_loop(0, n_tok, put, 0, unroll=16)


def _invert(pos, tile_end, *, n_sorted, tm_exp):
    n_tok = pos.shape[0]
    return pl.pallas_call(
        functools.partial(_invert_kernel, n_tok=n_tok, tm_exp=tm_exp),
        in_specs=[pl.BlockSpec(memory_space=pltpu.SMEM), pl.BlockSpec(memory_space=pltpu.SMEM)],
        out_specs=pl.BlockSpec(memory_space=pltpu.SMEM),
        out_shape=jax.ShapeDtypeStruct((n_sorted,), jnp.int32),
        name="invert",
    )(tile_end, pos)


def _collect_norm_kernel(pos_ref, ys_ref, nw_ref, o_ref, buf_ref, sem, *, tm):
    _row_gather(ys_ref, pos_ref, buf_ref, sem, tm)
    _row_gather_wait(ys_ref, buf_ref, sem, tm)
    x = buf_ref[...]
    ms = jnp.mean(x * x, axis=-1, keepdims=True)
    o_ref[...] = (x * lax.rsqrt(ms + EPS)) * nw_ref[...]


def _collect_norm(ys, pos, norm_w, *, tm):
    n_out = pos.shape[0]
    D = ys.shape[1]
    return pl.pallas_call(
        functools.partial(_collect_norm_kernel, tm=tm),
        grid=(n_out // tm,),
        in_specs=[pl.BlockSpec((tm,), lambda i: (i,), memory_space=pltpu.SMEM),
                  pl.BlockSpec(memory_space=pl.ANY),
                  _resident(norm_w.shape)],
        out_specs=pl.BlockSpec((tm, D), lambda i: (i, 0)),
        out_shape=jax.ShapeDtypeStruct((n_out, D), F32),
        scratch_shapes=[pltpu.VMEM((tm, D), F32), pltpu.SemaphoreType.DMA(())],
        compiler_params=_cparams(("arbitrary",)),
        name="collect_norm",
    )(pos, ys, norm_w)


def _moe_kernel(ea_ref, eb_ref, nu_ref, inv0_ref, invn_ref, hx_ref, n2_ref,
                wga_ref, wua_ref, wda_ref, wgb_ref, wub_ref, wdb_ref, ys_ref,
                buf_ref, xn_ref, sem, *, tm, d_model):
    i = pl.program_id(0)
    n_used = nu_ref[0]

    @pl.when(i < n_used)
    def _():
        slot = i % 2

        @pl.when(i == 0)
        def _():
            _row_gather(hx_ref, inv0_ref, buf_ref.at[0], sem.at[0], tm)

        _row_gather_wait(hx_ref, buf_ref.at[slot], sem.at[slot], tm)
        x = buf_ref[slot, :, 0:d_model]
        ms = jnp.mean(x * x, axis=-1, keepdims=True)
        xn_ref[...] = ((x * lax.rsqrt(ms + EPS)) * n2_ref[...]).astype(BF16)
        _row_gather(hx_ref, invn_ref, buf_ref.at[1 - slot], sem.at[1 - slot], tm)
        y = buf_ref[slot, :, 0:d_model]
        for col, (wg, wu, wd) in enumerate(((wga_ref, wua_ref, wda_ref), (wgb_ref, wub_ref, wdb_ref))):
            gate = _dot(xn_ref[...], wg[...])
            up = _dot(xn_ref[...], wu[...])
            hid = (gate * _sigmoid(gate) * up).astype(BF16)
            y = y + buf_ref[slot, :, d_model + col:d_model + col + 1] * _dot(hid, wd[...])
        ys_ref[...] = y

        @pl.when(i == n_used - 1)
        def _():
            _row_gather_wait(hx_ref, buf_ref.at[1 - slot], sem.at[1 - slot], tm)

    @pl.when(i >= n_used)
    def _():
        ys_ref[...] = jnp.zeros_like(ys_ref)


def _moe(hx, inv, tile_ea, tile_eb, n_used, n2, weg, weu, wed, *, tm):
    n_sorted = inv.shape[0]
    W = hx.shape[1]
    D = n2.shape[1]
    de = weg.shape[2]
    wa = lambda i, ea, eb, nu: (ea[i], 0, 0)
    wb = lambda i, ea, eb, nu: (eb[i], 0, 0)
    grid_spec = pltpu.PrefetchScalarGridSpec(
        num_scalar_prefetch=3,
        grid=(n_sorted // tm,),
        in_specs=[pl.BlockSpec((tm,), lambda i, ea, eb, nu: (0,), memory_space=pltpu.SMEM),
                  pl.BlockSpec((tm,), lambda i, ea, eb, nu: (jnp.minimum(i + 1, nu[0] - 1),),
                               memory_space=pltpu.SMEM),
                  pl.BlockSpec(memory_space=pl.ANY),
                  pl.BlockSpec((1, D), lambda i, ea, eb, nu: (0, 0)),
                  pl.BlockSpec((None, D, de), wa), pl.BlockSpec((None, D, de), wa),
                  pl.BlockSpec((None, de, D), wa),
                  pl.BlockSpec((None, D, de), wb), pl.BlockSpec((None, D, de), wb),
                  pl.BlockSpec((None, de, D), wb)],
        out_specs=pl.BlockSpec((tm, D), lambda i, ea, eb, nu: (i, 0)),
        scratch_shapes=[pltpu.VMEM((2, tm, W), F32), pltpu.VMEM((tm, D), BF16),
                        pltpu.SemaphoreType.DMA((2,))],
    )
    return pl.pallas_call(
        functools.partial(_moe_kernel, tm=tm, d_model=D),
        grid_spec=grid_spec,
        out_shape=jax.ShapeDtypeStruct((n_sorted, D), F32),
        compiler_params=_cparams(("arbitrary",)),
        name="experts",
    )(tile_ea, tile_eb, n_used, inv, inv, hx, n2, weg, weu, wed, weg, weu, wed)


def _row_tile(T, cap):
    t = cap
    while T % t:
        t //= 2
    return t


def kernel(x, meta_tokens, norm1_w, w_in, w_gate_up, b_gate, gla_norm_w, w_pool_grp, pool_scale,
           w_br_gla, w_br_pool, w_out, norm2_w, w_router_group, w_router_expert, w_exp_gate,
           w_exp_up, w_exp_down, final_norm_w):
    B, S, D = x.shape
    depth = w_in.shape[0]
    key = w_gate_up.shape[2]
    rank = w_gate_up.shape[1]
    dv = gla_norm_w.shape[1]
    val = GLA_HEADS * dv
    dk = key // GLA_HEADS
    assert S % CHUNK == 0 and key * 2 == D and val == D
    assert w_pool_grp.shape[2] * len(POOL_WINDOWS) == D
    LP = PAD + N_META + S
    n_chunks = LP // CHUNK
    T = B * LP
    tm_proj = _row_tile(T, 512)
    tm_mix = _row_tile(T, 512)
    tm_row = _row_tile(T, 512)
    tm_exp = 256
    n_sorted = -(-(T + N_CLASSES * (tm_exp - 1)) // tm_exp) * tm_exp
    n_tiles = n_sorted // tm_exp

    meta = jnp.broadcast_to(meta_tokens[None].astype(F32), (B, N_META, D))
    h = jnp.concatenate([jnp.zeros((B, PAD, D), F32), meta, x.astype(F32)], axis=1).reshape(T, D)

    c_q, c_k, c_v, c_og = 0, key, 2 * key, 2 * key + val
    c_gl = c_og + val
    c_u = c_gl + rank
    c_gt = c_u + D
    n_main = 2 * key + 2 * val + D + 2 * D

    ys = pos = None
    for l in range(depth):
        wl = w_in[l]
        w_all = jnp.concatenate(
            [wl[:, c_q:c_gl], wl[:, c_u:], wl[:, c_gl:c_u],
             jnp.zeros((D, LANES - rank), F32)], axis=1).astype(BF16)
        if l == 0:
            z, zg = _inproj(h, norm1_w[l][None], w_all, tm=tm_proj, n_main=n_main)
        else:
            z, zg, h = _inproj_gather(ys, pos, norm1_w[l][None], w_all, tm=tm_proj, n_main=n_main)

        wgu = jnp.concatenate([w_gate_up[l], jnp.zeros((LANES - rank, key), F32)], axis=0)
        ogla = _gla(z, zg, wgu, b_gate[l][None], batch=B, n_chunks=n_chunks, dk=dk, dv=dv)

        wr = jnp.concatenate([w_router_group[l], w_router_expert[l],
                              jnp.zeros((D, CLASS_ROWS - N_GROUPS - N_EXPERTS), F32)], axis=1).T
        hx, info, counts = _mix(
            ogla, z, h, gla_norm_w[l][None], w_pool_grp[l].astype(BF16), pool_scale[l][None],
            w_br_gla[l].astype(BF16), w_br_pool[l].astype(BF16), w_out[l].astype(BF16),
            norm2_w[l][None], wr, tm=tm_mix, seq_rows=LP)

        cnt = counts[:N_CLASSES, 0].astype(jnp.int32)
        seg_tiles = (cnt + tm_exp - 1) // tm_exp
        tile_end = jnp.cumsum(seg_tiles)
        start = (tile_end - seg_tiles) * tm_exp
        n_used = tile_end[-1:]
        tile_cls = jnp.minimum(
            jnp.sum(jnp.arange(n_tiles)[:, None] >= tile_end[None, :], axis=1), N_CLASSES - 1)
        tile_cls = jnp.where(jnp.arange(n_tiles) < n_used[0], tile_cls,
                             jnp.take(tile_cls, jnp.maximum(n_used[0] - 1, 0)))
        pair_lo = jnp.array([p[0] for p in PAIRS], jnp.int32)
        pair_hi = jnp.array([p[1] for p in PAIRS], jnp.int32)
        grp = tile_cls // len(PAIRS)
        tile_ea = (grp * EXPERTS_PER_GROUP + jnp.take(pair_lo, tile_cls % len(PAIRS))).astype(jnp.int32)
        tile_eb = (grp * EXPERTS_PER_GROUP + jnp.take(pair_hi, tile_cls % len(PAIRS))).astype(jnp.int32)
        cls_t = info[0].astype(jnp.int32)
        pos = jnp.take(start, cls_t) + info[1].astype(jnp.int32)

        inv = _invert(pos, tile_end.astype(jnp.int32), n_sorted=n_sorted, tm_exp=tm_exp)
        ys = _moe(hx, inv, tile_ea, tile_eb, n_used.astype(jnp.int32), norm2_w[l][None],
                  w_exp_gate[l].astype(BF16), w_exp_up[l].astype(BF16), w_exp_down[l].astype(BF16),
                  tm=tm_exp)

    pos_out = pos.reshape(B, LP)[:, PAD + N_META:].reshape(B * S)
    out = _collect_norm(ys, pos_out, final_norm_w[None], tm=_row_tile(B * S, 512))
    return out.reshape(B, S, D)
```

```python
import functools

import jax
import jax.numpy as jnp
from jax import lax
from jax.experimental import pallas as pl
from jax.experimental.pallas import tpu as pltpu

F32 = jnp.float32
BF16 = jnp.bfloat16

EPS = 1e-6
CHUNK = 64
N_META = 16
PAD = CHUNK - N_META
GLA_HEADS = 4
GATE_NORMALIZER = 16.0
POOL_WINDOWS = (2, 4, 8, 16)
N_GROUPS = 4
EXPERTS_PER_GROUP = 4
N_EXPERTS = N_GROUPS * EXPERTS_PER_GROUP
PAIRS = ((0, 1), (0, 2), (0, 3), (1, 2), (1, 3), (2, 3))
N_CLASSES = N_GROUPS * len(PAIRS)
CLASS_ROWS = 32
LANES = 128
HALO = 32
ROUTE_LANES = 128
VMEM_LIMIT = 56 * 1024 * 1024


def _cparams(sem):
    return pltpu.CompilerParams(dimension_semantics=sem, vmem_limit_bytes=VMEM_LIMIT)


def _resident(shape):
    nd = len(shape)
    return pl.BlockSpec(shape, lambda *_: (0,) * nd, pipeline_mode=pl.Buffered(1))


def _split(x):
    hi = x.astype(BF16)
    lo = (x - hi.astype(F32)).astype(BF16)
    return hi, lo


def _dot(a, b):
    return jnp.dot(a, b, preferred_element_type=F32)


def _dot_nt(a, b):
    return lax.dot_general(a, b, (((1,), (1,)), ((), ())), preferred_element_type=F32)


def _dot_tn(a, b):
    return lax.dot_general(a, b, (((0,), (0,)), ((), ())), preferred_element_type=F32)


def _sigmoid(x):
    return 0.5 * jnp.tanh(0.5 * x) + 0.5


def _inproj_kernel(h_ref, nw_ref, w_ref, z_ref, zg_ref, xn_ref, *, n_main, col_blk):
    x = h_ref[...]
    ms = jnp.mean(x * x, axis=-1, keepdims=True)
    xn_ref[...] = ((x * lax.rsqrt(ms + EPS)) * nw_ref[...]).astype(BF16)
    for j in range(n_main // col_blk):
        sl = slice(j * col_blk, (j + 1) * col_blk)
        z_ref[:, sl] = _dot(xn_ref[...], w_ref[:, sl]).astype(BF16)
    zg_ref[...] = _dot(xn_ref[...], w_ref[:, n_main:])


def _inproj(h, norm_w, w_all, *, tm, n_main):
    T, D = h.shape
    n_all = w_all.shape[1]
    return pl.pallas_call(
        functools.partial(_inproj_kernel, n_main=n_main, col_blk=512),
        grid=(T // tm,),
        in_specs=[pl.BlockSpec((tm, D), lambda i: (i, 0)),
                  _resident((1, D)),
                  _resident((D, n_all))],
        out_specs=[pl.BlockSpec((tm, n_main), lambda i: (i, 0)),
                   pl.BlockSpec((tm, n_all - n_main), lambda i: (i, 0))],
        out_shape=[jax.ShapeDtypeStruct((T, n_main), BF16),
                   jax.ShapeDtypeStruct((T, n_all - n_main), F32)],
        scratch_shapes=[pltpu.VMEM((tm, D), BF16)],
        compiler_params=_cparams(("parallel",)),
        name="inproj",
    )(h, norm_w, w_all)


def _row_gather(src_ref, pos_ref, dst_ref, sem, tm):
    for r in range(tm):
        pltpu.make_async_copy(src_ref.at[pl.ds(pos_ref[r], 1)], dst_ref.at[pl.ds(r, 1)], sem).start()


def _row_gather_wait(src_ref, dst_ref, sem, tm):
    pltpu.make_async_copy(src_ref.at[pl.ds(0, tm)], dst_ref, sem).wait()


def _inproj_gather_kernel(pos0_ref, posn_ref, ys_ref, nw_ref, w_ref, z_ref, zg_ref, h_ref,
                          buf_ref, xn_ref, sem, *, tm, n_main, col_blk):
    i = pl.program_id(0)
    last = pl.num_programs(0) - 1
    slot = i % 2

    @pl.when(i == 0)
    def _():
        _row_gather(ys_ref, pos0_ref, buf_ref.at[0], sem.at[0], tm)

    _row_gather_wait(ys_ref, buf_ref.at[slot], sem.at[slot], tm)
    x = buf_ref[slot]
    h_ref[...] = x
    ms = jnp.mean(x * x, axis=-1, keepdims=True)
    xn_ref[...] = ((x * lax.rsqrt(ms + EPS)) * nw_ref[...]).astype(BF16)
    _row_gather(ys_ref, posn_ref, buf_ref.at[1 - slot], sem.at[1 - slot], tm)
    for j in range(n_main // col_blk):
        sl = slice(j * col_blk, (j + 1) * col_blk)
        z_ref[:, sl] = _dot(xn_ref[...], w_ref[:, sl]).astype(BF16)
    zg_ref[...] = _dot(xn_ref[...], w_ref[:, n_main:])

    @pl.when(i == last)
    def _():
        _row_gather_wait(ys_ref, buf_ref.at[1 - slot], sem.at[1 - slot], tm)


def _inproj_gather(ys, pos, norm_w, w_all, *, tm, n_main):
    T = pos.shape[0]
    D = ys.shape[1]
    n_all = w_all.shape[1]
    n_steps = T // tm
    return pl.pallas_call(
        functools.partial(_inproj_gather_kernel, tm=tm, n_main=n_main, col_blk=512),
        grid=(n_steps,),
        in_specs=[pl.BlockSpec((tm,), lambda i: (0,), memory_space=pltpu.SMEM),
                  pl.BlockSpec((tm,), lambda i: (jnp.minimum(i + 1, n_steps - 1),),
                               memory_space=pltpu.SMEM),
                  pl.BlockSpec(memory_space=pl.ANY),
                  _resident((1, D)),
                  _resident((D, n_all))],
        out_specs=[pl.BlockSpec((tm, n_main), lambda i: (i, 0)),
                   pl.BlockSpec((tm, n_all - n_main), lambda i: (i, 0)),
                   pl.BlockSpec((tm, D), lambda i: (i, 0))],
        out_shape=[jax.ShapeDtypeStruct((T, n_main), BF16),
                   jax.ShapeDtypeStruct((T, n_all - n_main), F32),
                   jax.ShapeDtypeStruct((T, D), F32)],
        scratch_shapes=[pltpu.VMEM((2, tm, D), F32), pltpu.VMEM((tm, D), BF16),
                        pltpu.SemaphoreType.DMA((2,))],
        compiler_params=_cparams(("arbitrary",)),
        name="inproj_gather",
    )(pos, pos, ys, norm_w, w_all)


def _gla_prep(q_ref, k_ref, zg_ref, wgh_ref, wgl_ref, bg_ref, scaled_ref, dec_ref, slot, first,
              *, bb, dk):
    rows = bb * CHUNK
    key = GLA_HEADS * dk
    gl_hi, gl_lo = _split(zg_ref[...].reshape(rows, LANES))
    gpre = (_dot(gl_hi, wgh_ref[...]) + _dot(gl_hi, wgl_ref[...]) + _dot(gl_lo, wgh_ref[...])
            + bg_ref[...])
    g = (jnp.minimum(gpre, 0.0) - jnp.log(1.0 + jnp.exp(-jnp.abs(gpre)))) * (1.0 / GATE_NORMALIZER)
    if first:
        row = lax.broadcasted_iota(jnp.int32, (rows, 1), 0)
        g = jnp.where((row & (CHUNK - 1)) < PAD, 0.0, g)

    ri = lax.broadcasted_iota(jnp.int32, (rows, rows), 0)
    ci = lax.broadcasted_iota(jnp.int32, (rows, rows), 1)
    tri = jnp.where(jnp.logical_and((ri // CHUNK) == (ci // CHUNK), ri >= ci), 1.0, 0.0).astype(BF16)
    g_hi, g_lo = _split(g)
    bcum = _dot(tri, g_hi) + _dot(tri, g_lo)
    gam = jnp.concatenate(
        [jnp.broadcast_to(bcum[(c + 1) * CHUNK - 1:(c + 1) * CHUNK, :], (CHUNK, key)) for c in range(bb)],
        axis=0)
    eb = jnp.exp(bcum)
    ieb = jnp.exp(-bcum)
    q = q_ref[...].reshape(rows, key).astype(F32) * (dk ** -0.5)
    k = k_ref[...].reshape(rows, key).astype(F32)
    scaled_ref[slot, 0] = (q * eb).astype(BF16)
    scaled_ref[slot, 1] = (q * ieb).astype(BF16)
    scaled_ref[slot, 2] = (k * eb).astype(BF16)
    scaled_ref[slot, 3] = (k * ieb).astype(BF16)
    scaled_ref[slot, 4] = (k * jnp.exp(gam - bcum)).astype(BF16)
    for c in range(bb):
        dec_ref[slot, c] = eb[(c + 1) * CHUNK - 1:(c + 1) * CHUNK, :]


def _gla_kernel(q0_ref, k0_ref, zg0_ref, qn_ref, kn_ref, zgn_ref, v_ref, wgh_ref, wgl_ref, bg_ref,
                o_ref, st_ref, scaled_ref, dec_ref, *, bb, dk, dv):
    n = pl.program_id(1)
    prep = functools.partial(_gla_prep, wgh_ref=wgh_ref, wgl_ref=wgl_ref, bg_ref=bg_ref,
                             scaled_ref=scaled_ref, dec_ref=dec_ref, bb=bb, dk=dk)

    @pl.when(n == 0)
    def _():
        st_ref[...] = jnp.zeros_like(st_ref)
        prep(q0_ref, k0_ref, zg0_ref, slot=0, first=True)

    li = lax.broadcasted_iota(jnp.int32, (CHUNK, CHUNK), 0)
    lj = lax.broadcasted_iota(jnp.int32, (CHUNK, CHUNK), 1)
    lower = li >= lj
    groups = [(c, hd) for c in range(bb) for hd in range(GLA_HEADS)]

    def step(slot):
        prep(qn_ref, kn_ref, zgn_ref, slot=1 - slot, first=False)
        attn = {}
        for c, hd in groups:
            rs = slice(c * CHUNK, (c + 1) * CHUNK)
            ks = slice(hd * dk, (hd + 1) * dk)
            a_lo = _dot_nt(scaled_ref[slot, 0, rs, ks], scaled_ref[slot, 3, rs, ks])
            a_up = _dot_nt(scaled_ref[slot, 1, rs, ks], scaled_ref[slot, 2, rs, ks])
            attn[c, hd] = jnp.where(lower, a_lo, a_up).astype(BF16)
        for c, hd in groups:
            rs = slice(c * CHUNK, (c + 1) * CHUNK)
            ks = slice(hd * dk, (hd + 1) * dk)
            vs = slice(hd * dv, (hd + 1) * dv)
            o_ref[c, :, vs] = (_dot(attn[c, hd], v_ref[c, :, vs])
                               + _dot(scaled_ref[slot, 0, rs, ks], st_ref[c, hd].astype(BF16))
                               ).astype(BF16)
        for c, hd in groups:
            rs = slice(c * CHUNK, (c + 1) * CHUNK)
            ks = slice(hd * dk, (hd + 1) * dk)
            vs = slice(hd * dv, (hd + 1) * dv)
            dec = jnp.broadcast_to(dec_ref[slot, c, :, ks], (dk, dk)).T
            st_ref[c, hd] = (st_ref[c, hd] * jnp.concatenate([dec] * (dv // dk), axis=1)
                             + _dot_tn(scaled_ref[slot, 4, rs, ks], v_ref[c, :, vs]))

    for slot in (0, 1):
        pl.when(n % 2 == slot)(functools.partial(step, slot))


def _gla(z, zg, wgu, bg, *, batch, n_chunks, dk, dv):
    T = z.shape[0]
    key = GLA_HEADS * dk
    val = GLA_HEADS * dv
    bb = 4 if batch % 4 == 0 else 1
    z3 = z.reshape(batch, n_chunks * CHUNK, z.shape[1])
    zg3 = zg.reshape(batch, n_chunks * CHUNK, zg.shape[1])
    wg_hi, wg_lo = _split(wgu)
    nxt = lambda n: jnp.minimum(n + 1, n_chunks - 1)
    o = pl.pallas_call(
        functools.partial(_gla_kernel, bb=bb, dk=dk, dv=dv),
        grid=(batch // bb, n_chunks),
        in_specs=[pl.BlockSpec((bb, CHUNK, key), lambda b, n: (b, 0, 0)),
                  pl.BlockSpec((bb, CHUNK, key), lambda b, n: (b, 0, 1)),
                  pl.BlockSpec((bb, CHUNK, LANES), lambda b, n: (b, 0, 0)),
                  pl.BlockSpec((bb, CHUNK, key), lambda b, n: (b, nxt(n), 0)),
                  pl.BlockSpec((bb, CHUNK, key), lambda b, n: (b, nxt(n), 1)),
                  pl.BlockSpec((bb, CHUNK, LANES), lambda b, n: (b, nxt(n), 0)),
                  pl.BlockSpec((bb, CHUNK, val), lambda b, n: (b, n, 1)),
                  _resident(wg_hi.shape), _resident(wg_lo.shape), _resident(bg.shape)],
        out_specs=pl.BlockSpec((bb, CHUNK, val), lambda b, n: (b, n, 0)),
        out_shape=jax.ShapeDtypeStruct((batch, n_chunks * CHUNK, val), BF16),
        scratch_shapes=[pltpu.VMEM((bb, GLA_HEADS, dk, dv), F32),
                        pltpu.VMEM((2, 5, bb * CHUNK, key), BF16),
                        pltpu.VMEM((2, bb, 1, key), F32)],
        compiler_params=_cparams(("parallel", "arbitrary")),
        name="gla",
    )(z3, z3, zg3, z3, z3, zg3, z3, wg_hi, wg_lo, bg)
    return o.reshape(T, val)


def _mix_kernel(ogla_ref, og_ref, u_ref, halo_ref, g0_ref, g1_ref, h_ref, gnw_ref,
                wgrp_ref, pscale_ref, wbg_ref, wbp_ref, wout_ref, n2_ref, wr_ref,
                hx_ref, info_ref, cnt_ref,
                ue_ref, p2_ref, p4_ref, p8_ref, p16_ref, tri_ref, carry_ref,
                *, tm, seq_rows, d_model):
    i = pl.program_id(0)
    gd = d_model // len(POOL_WINDOWS)

    @pl.when(i == 0)
    def _():
        r = lax.broadcasted_iota(jnp.int32, (tm, tm), 0)
        c = lax.broadcasted_iota(jnp.int32, (tm, tm), 1)
        tri_ref[...] = jnp.where(r < c, 1.0, 0.0).astype(BF16)
        carry_ref[...] = jnp.zeros_like(carry_ref)

    rowf = (i * tm + lax.broadcasted_iota(jnp.int32, (tm, 1), 0)).astype(F32)
    lp = rowf - jnp.floor(rowf / float(seq_rows)) * float(seq_rows)
    is_pad = lp < float(PAD)
    seqpos1 = lp - float(PAD - 1)

    ue_ref[0:HALO, :] = halo_ref[...].astype(F32)
    ue_ref[HALO:, :] = u_ref[...].astype(F32)
    n8 = tm + HALO - 8
    p2_ref[8:, :] = ue_ref[8:, :] + ue_ref[pl.ds(7, n8), :]
    p4_ref[16:, :] = p2_ref[16:, gd:] + p2_ref[pl.ds(14, n8 - 8), gd:]
    p8_ref[24:, :] = p4_ref[24:, gd:] + p4_ref[pl.ds(20, n8 - 16), gd:]
    p16_ref[HALO:, :] = p8_ref[HALO:, gd:] + p8_ref[pl.ds(24, tm), gd:]
    sums = (p2_ref[HALO:, 0:gd], p4_ref[HALO:, 0:gd], p8_ref[HALO:, 0:gd], p16_ref[HALO:, :])

    ypool = []
    for gi, w in enumerate(POOL_WINDOWS):
        cs = slice(gi * gd, (gi + 1) * gd)
        cnt = jnp.clip(seqpos1, 1.0, float(w))
        pooled = sums[gi] / cnt - ue_ref[HALO:, cs]
        mixed = _dot(pooled.astype(BF16), wgrp_ref[gi]) * pscale_ref[:, cs]
        ypool.append(mixed.astype(BF16))
    ypool = jnp.concatenate(ypool, axis=1)

    dv = gnw_ref.shape[1]
    ygla = []
    for hd in range(GLA_HEADS):
        vs = slice(hd * dv, (hd + 1) * dv)
        o = ogla_ref[:, vs].astype(F32)
        o = o * lax.rsqrt(jnp.mean(o * o, axis=-1, keepdims=True) + EPS) * gnw_ref[...]
        og = og_ref[:, vs].astype(F32)
        ygla.append((o * (og * _sigmoid(og))).astype(BF16))
    br_g = _dot(jnp.concatenate(ygla, axis=1), wbg_ref[...])
    br_p = _dot(ypool, wbp_ref[...])
    merged = (_sigmoid(g0_ref[...].astype(F32)) * br_g + _sigmoid(g1_ref[...].astype(F32)) * br_p)
    delta = _dot(merged.astype(BF16), wout_ref[...])
    h_new = h_ref[...] + jnp.where(is_pad, 0.0, delta)
    hx_ref[:, 0:d_model] = h_new

    ms = jnp.mean(h_new * h_new, axis=-1, keepdims=True)
    xn = (h_new * lax.rsqrt(ms + EPS)) * n2_ref[...]
    x_hi, x_lo = _split(xn)
    w_hi, w_lo = _split(wr_ref[...])
    lt = _dot_nt(w_hi, x_hi) + _dot_nt(w_hi, x_lo) + _dot_nt(w_lo, x_hi)

    lg = [lt[j:j + 1, :] for j in range(N_GROUPS)]
    m = jnp.maximum(jnp.maximum(lg[0], lg[1]), jnp.maximum(lg[2], lg[3]))
    gidx = jnp.where(lg[0] == m, 0, jnp.where(lg[1] == m, 1, jnp.where(lg[2] == m, 2, 3)))
    pg_top = 1.0 / (jnp.exp(lg[0] - m) + jnp.exp(lg[1] - m) + jnp.exp(lg[2] - m) + jnp.exp(lg[3] - m))
    sel = []
    for e in range(EXPERTS_PER_GROUP):
        rows = [lt[N_GROUPS + g * EXPERTS_PER_GROUP + e:N_GROUPS + g * EXPERTS_PER_GROUP + e + 1, :]
                for g in range(N_GROUPS)]
        sel.append(jnp.where(gidx == 0, rows[0], jnp.where(gidx == 1, rows[1],
                                                           jnp.where(gidx == 2, rows[2], rows[3]))))
    m1 = jnp.maximum(jnp.maximum(sel[0], sel[1]), jnp.maximum(sel[2], sel[3]))
    i1 = jnp.where(sel[0] == m1, 0, jnp.where(sel[1] == m1, 1, jnp.where(sel[2] == m1, 2, 3)))
    neg = jnp.float32(-jnp.inf)
    rest = [jnp.where(i1 == e, neg, sel[e]) for e in range(EXPERTS_PER_GROUP)]
    m2 = jnp.maximum(jnp.maximum(rest[0], rest[1]), jnp.maximum(rest[2], rest[3]))
    i2 = jnp.where(rest[0] == m2, 0, jnp.where(rest[1] == m2, 1, jnp.where(rest[2] == m2, 2, 3)))
    r21 = jnp.exp(m2 - m1)
    w1 = pg_top / (1.0 + r21)
    w2 = pg_top * r21 / (1.0 + r21)
    lo = jnp.minimum(i1, i2)
    hi = jnp.maximum(i1, i2)
    w_lo = jnp.where(i1 < i2, w1, w2)
    w_hi = jnp.where(i1 < i2, w2, w1)
    pidx = jnp.where(lo == 0, hi - 1, jnp.where(lo == 1, hi + 1, 5))
    cls = gidx * len(PAIRS) + pidx

    crow = lax.broadcasted_iota(jnp.int32, (CLASS_ROWS, tm), 0)
    onehot = jnp.where(crow == cls, 1.0, 0.0)
    prefix = _dot(onehot.astype(BF16), tri_ref[...]) + carry_ref[:, 0:1]
    rank = jnp.sum(onehot * prefix, axis=0, keepdims=True)
    carry_ref[...] = carry_ref[...] + jnp.sum(onehot, axis=1, keepdims=True)
    cnt_ref[...] = carry_ref[...]

    irow = lax.broadcasted_iota(jnp.int32, (8, tm), 0)
    info_ref[...] = jnp.where(irow == 0, cls.astype(F32), jnp.where(irow == 1, rank, 0.0))
    wrow = lax.broadcasted_iota(jnp.int32, (ROUTE_LANES, tm), 0)
    wrows = jnp.where(wrow == 0, w_lo, jnp.where(wrow == 1, w_hi, 0.0))
    hx_ref[:, d_model:] = wrows.T


def _mix(ogla, z, h, gnw, wgrp, pscale, wbg, wbp, wout, n2, wr, *, tm, seq_rows):
    T, D = h.shape
    nblk = tm // HALO
    return pl.pallas_call(
        functools.partial(_mix_kernel, tm=tm, seq_rows=seq_rows, d_model=D),
        grid=(T // tm,),
        in_specs=[pl.BlockSpec((tm, D), lambda i: (i, 0)),
                  pl.BlockSpec((tm, D), lambda i: (i, 2)),
                  pl.BlockSpec((tm, D), lambda i: (i, 3)),
                  pl.BlockSpec((HALO, D), lambda i: (jnp.maximum(i * nblk - 1, 0), 3)),
                  pl.BlockSpec((tm, D), lambda i: (i, 4)),
                  pl.BlockSpec((tm, D), lambda i: (i, 5)),
                  pl.BlockSpec((tm, D), lambda i: (i, 0)),
                  _resident(gnw.shape),
                  _resident(wgrp.shape), _resident(pscale.shape), _resident(wbg.shape),
                  _resident(wbp.shape), _resident(wout.shape), _resident(n2.shape),
                  _resident(wr.shape)],
        out_specs=[pl.BlockSpec((tm, D + ROUTE_LANES), lambda i: (i, 0)),
                   pl.BlockSpec((8, tm), lambda i: (0, i)),
                   pl.BlockSpec((CLASS_ROWS, LANES), lambda i: (0, 0))],
        out_shape=[jax.ShapeDtypeStruct((T, D + ROUTE_LANES), F32),
                   jax.ShapeDtypeStruct((8, T), F32),
                   jax.ShapeDtypeStruct((CLASS_ROWS, LANES), F32)],
        scratch_shapes=[pltpu.VMEM((tm + HALO, D), F32),
                        pltpu.VMEM((tm + HALO, D), F32),
                        pltpu.VMEM((tm + HALO, D - D // 4), F32),
                        pltpu.VMEM((tm + HALO, D - 2 * (D // 4)), F32),
                        pltpu.VMEM((tm + HALO, D // 4), F32),
                        pltpu.VMEM((tm, tm), BF16),
                        pltpu.VMEM((CLASS_ROWS, LANES), F32)],
        compiler_params=_cparams(("arbitrary",)),
        name="mix",
    )(ogla, z, z, z, z, z, h, gnw, wgrp, pscale, wbg, wbp, wout, n2, wr)


def _invert_kernel(tend_ref, pos_ref, inv_ref, *, n_tok, tm_exp):
    for c in range(N_CLASSES):
        base = (tend_ref[c] - 1) * tm_exp

        @pl.when(tend_ref[c] > (tend_ref[c - 1] if c else 0))
        def _():
            def clear(p, carry):
                inv_ref[base + p] = 0
                return carry
            lax.fori_loop(0, tm_exp, clear, 0, unroll=16)

    def clear_tail(p, carry):
        inv_ref[p] = 0
        return carry
    lax.fori_loop(tend_ref[N_CLASSES - 1] * tm_exp, inv_ref.shape[0], clear_tail, 0)

    def put(t, carry):
        inv_ref[pos_ref[t]] = t
        return carry
    lax.fori_loop(0, n_tok, put, 0, unroll=16)


def _invert(pos, tile_end, *, n_sorted, tm_exp):
    n_tok = pos.shape[0]
    return pl.pallas_call(
        functools.partial(_invert_kernel, n_tok=n_tok, tm_exp=tm_exp),
        in_specs=[pl.BlockSpec(memory_space=pltpu.SMEM), pl.BlockSpec(memory_space=pltpu.SMEM)],
        out_specs=pl.BlockSpec(memory_space=pltpu.SMEM),
        out_shape=jax.ShapeDtypeStruct((n_sorted,), jnp.int32),
        name="invert",
    )(tile_end, pos)


def _collect_norm_kernel(pos_ref, ys_ref, nw_ref, o_ref, buf_ref, sem, *, tm):
    _row_gather(ys_ref, pos_ref, buf_ref, sem, tm)
    _row_gather_wait(ys_ref, buf_ref, sem, tm)
    x = buf_ref[...]
    ms = jnp.mean(x * x, axis=-1, keepdims=True)
    o_ref[...] = (x * lax.rsqrt(ms + EPS)) * nw_ref[...]


def _collect_norm(ys, pos, norm_w, *, tm):
    n_out = pos.shape[0]
    D = ys.shape[1]
    return pl.pallas_call(
        functools.partial(_collect_norm_kernel, tm=tm),
        grid=(n_out // tm,),
        in_specs=[pl.BlockSpec((tm,), lambda i: (i,), memory_space=pltpu.SMEM),
                  pl.BlockSpec(memory_space=pl.ANY),
                  _resident(norm_w.shape)],
        out_specs=pl.BlockSpec((tm, D), lambda i: (i, 0)),
        out_shape=jax.ShapeDtypeStruct((n_out, D), F32),
        scratch_shapes=[pltpu.VMEM((tm, D), F32), pltpu.SemaphoreType.DMA(())],
        compiler_params=_cparams(("arbitrary",)),
        name="collect_norm",
    )(pos, ys, norm_w)


def _moe_kernel(ea_ref, eb_ref, nu_ref, inv0_ref, inv1_ref, invn_ref, hx_ref, n2_ref,
                wga_ref, wua_ref, wda_ref, wgb_ref, wub_ref, wdb_ref, ys_ref,
                buf_ref, xn_ref, sem, *, tm, d_model):
    i = pl.program_id(0)
    n_used = nu_ref[0]

    @pl.when(i < n_used)
    def _():
        slot = i % 3
        ahead = (i + 2) % 3

        @pl.when(i == 0)
        def _():
            _row_gather(hx_ref, inv0_ref, buf_ref.at[0], sem.at[0], tm)
            _row_gather(hx_ref, inv1_ref, buf_ref.at[1], sem.at[1], tm)

        _row_gather_wait(hx_ref, buf_ref.at[slot], sem.at[slot], tm)
        x = buf_ref[slot, :, 0:d_model]
        ms = jnp.mean(x * x, axis=-1, keepdims=True)
        xn_ref[...] = ((x * lax.rsqrt(ms + EPS)) * n2_ref[...]).astype(BF16)
        _row_gather(hx_ref, invn_ref, buf_ref.at[ahead], sem.at[ahead], tm)
        y = buf_ref[slot, :, 0:d_model]
        for col, (wg, wu, wd) in enumerate(((wga_ref, wua_ref, wda_ref), (wgb_ref, wub_ref, wdb_ref))):
            gate = _dot(xn_ref[...], wg[...])
            up = _dot(xn_ref[...], wu[...])
            hid = (gate * _sigmoid(gate) * up).astype(BF16)
            y = y + buf_ref[slot, :, d_model + col:d_model + col + 1] * _dot(hid, wd[...])
        ys_ref[...] = y

        @pl.when(i == n_used - 1)
        def _():
            for s in ((i + 1) % 3, ahead):
                _row_gather_wait(hx_ref, buf_ref.at[s], sem.at[s], tm)

    @pl.when(i >= n_used)
    def _():
        ys_ref[...] = jnp.zeros_like(ys_ref)


def _moe(hx, inv, tile_ea, tile_eb, n_used, n2, weg, weu, wed, *, tm):
    n_sorted = inv.shape[0]
    W = hx.shape[1]
    D = n2.shape[1]
    de = weg.shape[2]
    wa = lambda i, ea, eb, nu: (ea[i], 0, 0)
    wb = lambda i, ea, eb, nu: (eb[i], 0, 0)
    grid_spec = pltpu.PrefetchScalarGridSpec(
        num_scalar_prefetch=3,
        grid=(n_sorted // tm,),
        in_specs=[pl.BlockSpec((tm,), lambda i, ea, eb, nu: (0,), memory_space=pltpu.SMEM),
                  pl.BlockSpec((tm,), lambda i, ea, eb, nu: (jnp.minimum(1, nu[0] - 1),),
                               memory_space=pltpu.SMEM),
                  pl.BlockSpec((tm,), lambda i, ea, eb, nu: (jnp.minimum(i + 2, nu[0] - 1),),
                               memory_space=pltpu.SMEM),
                  pl.BlockSpec(memory_space=pl.ANY),
                  pl.BlockSpec((1, D), lambda i, ea, eb, nu: (0, 0)),
                  pl.BlockSpec((None, D, de), wa), pl.BlockSpec((None, D, de), wa),
                  pl.BlockSpec((None, de, D), wa),
                  pl.BlockSpec((None, D, de), wb), pl.BlockSpec((None, D, de), wb),
                  pl.BlockSpec((None, de, D), wb)],
        out_specs=pl.BlockSpec((tm, D), lambda i, ea, eb, nu: (i, 0)),
        scratch_shapes=[pltpu.VMEM((3, tm, W), F32), pltpu.VMEM((tm, D), BF16),
                        pltpu.SemaphoreType.DMA((3,))],
    )
    return pl.pallas_call(
        functools.partial(_moe_kernel, tm=tm, d_model=D),
        grid_spec=grid_spec,
        out_shape=jax.ShapeDtypeStruct((n_sorted, D), F32),
        compiler_params=_cparams(("arbitrary",)),
        name="experts",
    )(tile_ea, tile_eb, n_used, inv, inv, inv, hx, n2, weg, weu, wed, weg, weu, wed)


def _row_tile(T, cap):
    t = cap
    while T % t:
        t //= 2
    return t


def kernel(x, meta_tokens, norm1_w, w_in, w_gate_up, b_gate, gla_norm_w, w_pool_grp, pool_scale,
           w_br_gla, w_br_pool, w_out, norm2_w, w_router_group, w_router_expert, w_exp_gate,
           w_exp_up, w_exp_down, final_norm_w):
    B, S, D = x.shape
    depth = w_in.shape[0]
    key = w_gate_up.shape[2]
    rank = w_gate_up.shape[1]
    dv = gla_norm_w.shape[1]
    val = GLA_HEADS * dv
    dk = key // GLA_HEADS
    assert S % CHUNK == 0 and key * 2 == D and val == D
    assert w_pool_grp.shape[2] * len(POOL_WINDOWS) == D
    LP = PAD + N_META + S
    n_chunks = LP // CHUNK
    T = B * LP
    tm_proj = _row_tile(T, 512)
    tm_mix = _row_tile(T, 512)
    tm_row = _row_tile(T, 512)
    tm_exp = 256
    n_sorted = -(-(T + N_CLASSES * (tm_exp - 1)) // tm_exp) * tm_exp
    n_tiles = n_sorted // tm_exp

    meta = jnp.broadcast_to(meta_tokens[None].astype(F32), (B, N_META, D))
    h = jnp.concatenate([jnp.zeros((B, PAD, D), F32), meta, x.astype(F32)], axis=1).reshape(T, D)

    c_q, c_k, c_v, c_og = 0, key, 2 * key, 2 * key + val
    c_gl = c_og + val
    c_u = c_gl + rank
    c_gt = c_u + D
    n_main = 2 * key + 2 * val + D + 2 * D

    ys = pos = None
    for l in range(depth):
        wl = w_in[l]
        w_all = jnp.concatenate(
            [wl[:, c_q:c_gl], wl[:, c_u:], wl[:, c_gl:c_u],
             jnp.zeros((D, LANES - rank), F32)], axis=1).astype(BF16)
        if l == 0:
            z, zg = _inproj(h, norm1_w[l][None], w_all, tm=tm_proj, n_main=n_main)
        else:
            z, zg, h = _inproj_gather(ys, pos, norm1_w[l][None], w_all, tm=tm_proj, n_main=n_main)

        wgu = jnp.concatenate([w_gate_up[l], jnp.zeros((LANES - rank, key), F32)], axis=0)
        ogla = _gla(z, zg, wgu, b_gate[l][None], batch=B, n_chunks=n_chunks, dk=dk, dv=dv)

        wr = jnp.concatenate([w_router_group[l], w_router_expert[l],
                              jnp.zeros((D, CLASS_ROWS - N_GROUPS - N_EXPERTS), F32)], axis=1).T
        hx, info, counts = _mix(
            ogla, z, h, gla_norm_w[l][None], w_pool_grp[l].astype(BF16), pool_scale[l][None],
            w_br_gla[l].astype(BF16), w_br_pool[l].astype(BF16), w_out[l].astype(BF16),
            norm2_w[l][None], wr, tm=tm_mix, seq_rows=LP)

        cnt = counts[:N_CLASSES, 0].astype(jnp.int32)
        seg_tiles = (cnt + tm_exp - 1) // tm_exp
        tile_end = jnp.cumsum(seg_tiles)
        start = (tile_end - seg_tiles) * tm_exp
        n_used = tile_end[-1:]
        tile_cls = jnp.minimum(
            jnp.sum(jnp.arange(n_tiles)[:, None] >= tile_end[None, :], axis=1), N_CLASSES - 1)
        tile_cls = jnp.where(jnp.arange(n_tiles) < n_used[0], tile_cls,
                             jnp.take(tile_cls, jnp.maximum(n_used[0] - 1, 0)))
        pair_lo = jnp.array([p[0] for p in PAIRS], jnp.int32)
        pair_hi = jnp.array([p[1] for p in PAIRS], jnp.int32)
        grp = tile_cls // len(PAIRS)
        tile_ea = (grp * EXPERTS_PER_GROUP + jnp.take(pair_lo, tile_cls % len(PAIRS))).astype(jnp.int32)
        tile_eb = (grp * EXPERTS_PER_GROUP + jnp.take(pair_hi, tile_cls % len(PAIRS))).astype(jnp.int32)
        cls_t = info[0].astype(jnp.int32)
        pos = jnp.take(start, cls_t) + info[1].astype(jnp.int32)

        inv = _invert(pos, tile_end.astype(jnp.int32), n_sorted=n_sorted, tm_exp=tm_exp)
        ys = _moe(hx, inv, tile_ea, tile_eb, n_used.astype(jnp.int32), norm2_w[l][None],
                  w_exp_gate[l].astype(BF16), w_exp_up[l].astype(BF16), w_exp_down[l].astype(BF16),
                  tm=tm_exp)

    pos_out = pos.reshape(B, LP)[:, PAD + N_META:].reshape(B * S)
    out = _collect_norm(ys, pos_out, final_norm_w[None], tm=_row_tile(B * S, 512))
    return out.reshape(B, S, D)
```

```python
import functools

import jax
import jax.numpy as jnp
import numpy as np
from jax import lax
from jax.experimental import pallas as pl
from jax.experimental.pallas import tpu as pltpu

F32 = jnp.float32
BF16 = jnp.bfloat16

EPS = 1e-6
CHUNK = 64
N_META = 16
PAD = CHUNK - N_META
GLA_HEADS = 4
GATE_NORMALIZER = 16.0
POOL_WINDOWS = (2, 4, 8, 16)
N_GROUPS = 4
EXPERTS_PER_GROUP = 4
N_EXPERTS = N_GROUPS * EXPERTS_PER_GROUP
PAIRS = ((0, 1), (0, 2), (0, 3), (1, 2), (1, 3), (2, 3))
N_CLASSES = N_GROUPS * len(PAIRS)
CLASS_ROWS = 32
LANES = 128
HALO = 32
ROUTE_LANES = 128
VMEM_LIMIT = 56 * 1024 * 1024
DMA_THREADS = 2


def _cparams(sem):
    return pltpu.CompilerParams(dimension_semantics=sem, vmem_limit_bytes=VMEM_LIMIT)


def _resident(shape):
    nd = len(shape)
    return pl.BlockSpec(shape, lambda *_: (0,) * nd, pipeline_mode=pl.Buffered(1))


def _split(x):
    hi = x.astype(BF16)
    lo = (x - hi.astype(F32)).astype(BF16)
    return hi, lo


def _dot(a, b):
    return jnp.dot(a, b, preferred_element_type=F32)


def _dot_nt(a, b):
    return lax.dot_general(a, b, (((1,), (1,)), ((), ())), preferred_element_type=F32)


def _dot_tn(a, b):
    return lax.dot_general(a, b, (((0,), (0,)), ((), ())), preferred_element_type=F32)


def _sigmoid(x):
    return 0.5 * jnp.tanh(0.5 * x) + 0.5


def _row_gather(src_ref, pos_ref, dst_ref, sem, tm):
    for r in range(tm):
        pltpu.make_async_copy(src_ref.at[pl.ds(pos_ref[r], 1)], dst_ref.at[pl.ds(r, 1)],
                              sem).start(priority=r % DMA_THREADS)


def _row_gather_wait(src_ref, dst_ref, sem, tm):
    pltpu.make_async_copy(src_ref.at[pl.ds(0, tm)], dst_ref, sem).wait()


def _inproj_gather_kernel(pos0_ref, posn_ref, ys_ref, head_ref, nw_ref, w_ref, z_ref, zg_ref, h_ref,
                          buf_ref, xn_ref, sem, *, tm, n_main, col_blk, seq_rows, n_batch):
    i = pl.program_id(0)
    last = pl.num_programs(0) - 1
    slot = i % 2

    @pl.when(i == 0)
    def _():
        _row_gather(ys_ref, pos0_ref, buf_ref.at[0], sem.at[0], tm)

    _row_gather_wait(ys_ref, buf_ref.at[slot], sem.at[slot], tm)
    if seq_rows:
        b0 = (i * tm + seq_rows - 1) // seq_rows
        for k in range(tm // seq_rows + 1):
            off = (b0 + k) * seq_rows - i * tm

            @pl.when(jnp.logical_and(off < tm, b0 + k < n_batch))
            def _():
                buf_ref[slot, pl.ds(pl.multiple_of(off, CHUNK), CHUNK), :] = head_ref[...]

    x = buf_ref[slot]
    h_ref[...] = x
    ms = jnp.mean(x * x, axis=-1, keepdims=True)
    xn_ref[...] = ((x * lax.rsqrt(ms + EPS)) * nw_ref[...]).astype(BF16)
    _row_gather(ys_ref, posn_ref, buf_ref.at[1 - slot], sem.at[1 - slot], tm)
    for j in range(n_main // col_blk):
        sl = slice(j * col_blk, (j + 1) * col_blk)
        z_ref[:, sl] = _dot(xn_ref[...], w_ref[:, sl]).astype(BF16)
    zg_ref[...] = _dot(xn_ref[...], w_ref[:, n_main:])

    @pl.when(i == last)
    def _():
        _row_gather_wait(ys_ref, buf_ref.at[1 - slot], sem.at[1 - slot], tm)


def _inproj_gather(ys, pos, head, norm_w, w_all, *, tm, n_main, seq_rows=0, n_batch=0):
    T = pos.shape[0]
    D = ys.shape[1]
    n_all = w_all.shape[1]
    n_steps = T // tm
    assert tm % CHUNK == 0 and seq_rows % CHUNK == 0
    return pl.pallas_call(
        functools.partial(_inproj_gather_kernel, tm=tm, n_main=n_main, col_blk=512,
                          seq_rows=seq_rows, n_batch=n_batch),
        grid=(n_steps,),
        in_specs=[pl.BlockSpec((tm,), lambda i: (0,), memory_space=pltpu.SMEM),
                  pl.BlockSpec((tm,), lambda i: (jnp.minimum(i + 1, n_steps - 1),),
                               memory_space=pltpu.SMEM),
                  pl.BlockSpec(memory_space=pl.ANY),
                  _resident(head.shape),
                  _resident((1, D)),
                  _resident((D, n_all))],
        out_specs=[pl.BlockSpec((tm, n_main), lambda i: (i, 0)),
                   pl.BlockSpec((tm, n_all - n_main), lambda i: (i, 0)),
                   pl.BlockSpec((tm, D), lambda i: (i, 0))],
        out_shape=[jax.ShapeDtypeStruct((T, n_main), BF16),
                   jax.ShapeDtypeStruct((T, n_all - n_main), F32),
                   jax.ShapeDtypeStruct((T, D), F32)],
        scratch_shapes=[pltpu.VMEM((2, tm, D), F32), pltpu.VMEM((tm, D), BF16),
                        pltpu.SemaphoreType.DMA((2,))],
        compiler_params=_cparams(("arbitrary",)),
        name="inproj_gather",
    )(pos, pos, ys, head, norm_w, w_all)


def _gla_prep(q_ref, k_ref, zg_ref, wgh_ref, wgl_ref, bg_ref, scaled_ref, dec_ref, slot, first,
              *, bb, dk):
    rows = bb * CHUNK
    key = GLA_HEADS * dk
    gl_hi, gl_lo = _split(zg_ref[...].reshape(rows, LANES))
    gpre = (_dot(gl_hi, wgh_ref[...]) + _dot(gl_hi, wgl_ref[...]) + _dot(gl_lo, wgh_ref[...])
            + bg_ref[...])
    g = (jnp.minimum(gpre, 0.0) - jnp.log(1.0 + jnp.exp(-jnp.abs(gpre)))) * (1.0 / GATE_NORMALIZER)
    if first:
        row = lax.broadcasted_iota(jnp.int32, (rows, 1), 0)
        g = jnp.where((row & (CHUNK - 1)) < PAD, 0.0, g)

    ri = lax.broadcasted_iota(jnp.int32, (rows, rows), 0)
    ci = lax.broadcasted_iota(jnp.int32, (rows, rows), 1)
    tri = jnp.where(jnp.logical_and((ri // CHUNK) == (ci // CHUNK), ri >= ci), 1.0, 0.0).astype(BF16)
    g_hi, g_lo = _split(g)
    bcum = _dot(tri, g_hi) + _dot(tri, g_lo)
    gam = jnp.concatenate(
        [jnp.broadcast_to(bcum[(c + 1) * CHUNK - 1:(c + 1) * CHUNK, :], (CHUNK, key)) for c in range(bb)],
        axis=0)
    eb = jnp.exp(bcum)
    ieb = jnp.exp(-bcum)
    q = q_ref[...].reshape(rows, key).astype(F32) * (dk ** -0.5)
    k = k_ref[...].reshape(rows, key).astype(F32)
    scaled_ref[slot, 0] = (q * eb).astype(BF16)
    scaled_ref[slot, 1] = (q * ieb).astype(BF16)
    scaled_ref[slot, 2] = (k * eb).astype(BF16)
    scaled_ref[slot, 3] = (k * ieb).astype(BF16)
    scaled_ref[slot, 4] = (k * jnp.exp(gam - bcum)).astype(BF16)
    for c in range(bb):
        dec_ref[slot, c] = eb[(c + 1) * CHUNK - 1:(c + 1) * CHUNK, :]


def _gla_kernel(q0_ref, k0_ref, zg0_ref, qn_ref, kn_ref, zgn_ref, v_ref, wgh_ref, wgl_ref, bg_ref,
                o_ref, st_ref, scaled_ref, dec_ref, *, bb, dk, dv):
    n = pl.program_id(1)
    prep = functools.partial(_gla_prep, wgh_ref=wgh_ref, wgl_ref=wgl_ref, bg_ref=bg_ref,
                             scaled_ref=scaled_ref, dec_ref=dec_ref, bb=bb, dk=dk)

    @pl.when(n == 0)
    def _():
        st_ref[...] = jnp.zeros_like(st_ref)
        prep(q0_ref, k0_ref, zg0_ref, slot=0, first=True)

    li = lax.broadcasted_iota(jnp.int32, (CHUNK, CHUNK), 0)
    lj = lax.broadcasted_iota(jnp.int32, (CHUNK, CHUNK), 1)
    lower = li >= lj
    groups = [(c, hd) for c in range(bb) for hd in range(GLA_HEADS)]

    def step(slot):
        prep(qn_ref, kn_ref, zgn_ref, slot=1 - slot, first=False)
        attn = {}
        for c, hd in groups:
            rs = slice(c * CHUNK, (c + 1) * CHUNK)
            ks = slice(hd * dk, (hd + 1) * dk)
            a_lo = _dot_nt(scaled_ref[slot, 0, rs, ks], scaled_ref[slot, 3, rs, ks])
            a_up = _dot_nt(scaled_ref[slot, 1, rs, ks], scaled_ref[slot, 2, rs, ks])
            attn[c, hd] = jnp.where(lower, a_lo, a_up).astype(BF16)
        for c, hd in groups:
            rs = slice(c * CHUNK, (c + 1) * CHUNK)
            ks = slice(hd * dk, (hd + 1) * dk)
            vs = slice(hd * dv, (hd + 1) * dv)
            o_ref[c, :, vs] = (_dot(attn[c, hd], v_ref[c, :, vs])
                               + _dot(scaled_ref[slot, 0, rs, ks], st_ref[c, hd].astype(BF16))
                               ).astype(BF16)
        for c, hd in groups:
            rs = slice(c * CHUNK, (c + 1) * CHUNK)
            ks = slice(hd * dk, (hd + 1) * dk)
            vs = slice(hd * dv, (hd + 1) * dv)
            dec = jnp.broadcast_to(dec_ref[slot, c, :, ks], (dk, dk)).T
            st_ref[c, hd] = (st_ref[c, hd] * jnp.concatenate([dec] * (dv // dk), axis=1)
                             + _dot_tn(scaled_ref[slot, 4, rs, ks], v_ref[c, :, vs]))

    for slot in (0, 1):
        pl.when(n % 2 == slot)(functools.partial(step, slot))


def _gla(z, zg, wgu, bg, *, batch, n_chunks, dk, dv):
    T = z.shape[0]
    key = GLA_HEADS * dk
    val = GLA_HEADS * dv
    bb = 4 if batch % 4 == 0 else 1
    z3 = z.reshape(batch, n_chunks * CHUNK, z.shape[1])
    zg3 = zg.reshape(batch, n_chunks * CHUNK, zg.shape[1])
    wg_hi, wg_lo = _split(wgu)
    nxt = lambda n: jnp.minimum(n + 1, n_chunks - 1)
    o = pl.pallas_call(
        functools.partial(_gla_kernel, bb=bb, dk=dk, dv=dv),
        grid=(batch // bb, n_chunks),
        in_specs=[pl.BlockSpec((bb, CHUNK, key), lambda b, n: (b, 0, 0)),
                  pl.BlockSpec((bb, CHUNK, key), lambda b, n: (b, 0, 1)),
                  pl.BlockSpec((bb, CHUNK, LANES), lambda b, n: (b, 0, 0)),
                  pl.BlockSpec((bb, CHUNK, key), lambda b, n: (b, nxt(n), 0)),
                  pl.BlockSpec((bb, CHUNK, key), lambda b, n: (b, nxt(n), 1)),
                  pl.BlockSpec((bb, CHUNK, LANES), lambda b, n: (b, nxt(n), 0)),
                  pl.BlockSpec((bb, CHUNK, val), lambda b, n: (b, n, 1)),
                  _resident(wg_hi.shape), _resident(wg_lo.shape), _resident(bg.shape)],
        out_specs=pl.BlockSpec((bb, CHUNK, val), lambda b, n: (b, n, 0)),
        out_shape=jax.ShapeDtypeStruct((batch, n_chunks * CHUNK, val), BF16),
        scratch_shapes=[pltpu.VMEM((bb, GLA_HEADS, dk, dv), F32),
                        pltpu.VMEM((2, 5, bb * CHUNK, key), BF16),
                        pltpu.VMEM((2, bb, 1, key), F32)],
        compiler_params=_cparams(("parallel", "arbitrary")),
        name="gla",
    )(z3, z3, zg3, z3, z3, zg3, z3, wg_hi, wg_lo, bg)
    return o.reshape(T, val)


def _mix_kernel(ogla_ref, og_ref, u_ref, halo_ref, g0_ref, g1_ref, h_ref, gnw_ref,
                wgrp_ref, pscale_ref, wbg_ref, wbp_ref, wout_ref, n2_ref, wr_ref,
                hx_ref, info_ref, cnt_ref,
                ue_ref, p2_ref, p4_ref, p8_ref, p16_ref, tri_ref, carry_ref,
                *, tm, seq_rows, d_model):
    i = pl.program_id(0)
    gd = d_model // len(POOL_WINDOWS)

    @pl.when(i == 0)
    def _():
        r = lax.broadcasted_iota(jnp.int32, (tm, tm), 0)
        c = lax.broadcasted_iota(jnp.int32, (tm, tm), 1)
        tri_ref[...] = jnp.where(r < c, 1.0, 0.0).astype(BF16)
        carry_ref[...] = jnp.zeros_like(carry_ref)

    rowf = (i * tm + lax.broadcasted_iota(jnp.int32, (tm, 1), 0)).astype(F32)
    lp = rowf - jnp.floor(rowf / float(seq_rows)) * float(seq_rows)
    is_pad = lp < float(PAD)
    seqpos1 = lp - float(PAD - 1)

    ue_ref[0:HALO, :] = halo_ref[...].astype(F32)
    ue_ref[HALO:, :] = u_ref[...].astype(F32)
    n8 = tm + HALO - 8
    p2_ref[8:, :] = ue_ref[8:, :] + ue_ref[pl.ds(7, n8), :]
    p4_ref[16:, :] = p2_ref[16:, gd:] + p2_ref[pl.ds(14, n8 - 8), gd:]
    p8_ref[24:, :] = p4_ref[24:, gd:] + p4_ref[pl.ds(20, n8 - 16), gd:]
    p16_ref[HALO:, :] = p8_ref[HALO:, gd:] + p8_ref[pl.ds(24, tm), gd:]
    sums = (p2_ref[HALO:, 0:gd], p4_ref[HALO:, 0:gd], p8_ref[HALO:, 0:gd], p16_ref[HALO:, :])

    ypool = []
    for gi, w in enumerate(POOL_WINDOWS):
        cs = slice(gi * gd, (gi + 1) * gd)
        cnt = jnp.clip(seqpos1, 1.0, float(w))
        pooled = sums[gi] / cnt - ue_ref[HALO:, cs]
        mixed = _dot(pooled.astype(BF16), wgrp_ref[gi]) * pscale_ref[:, cs]
        ypool.append(mixed.astype(BF16))
    ypool = jnp.concatenate(ypool, axis=1)

    dv = gnw_ref.shape[1]
    ygla = []
    for hd in range(GLA_HEADS):
        vs = slice(hd * dv, (hd + 1) * dv)
        o = ogla_ref[:, vs].astype(F32)
        o = o * lax.rsqrt(jnp.mean(o * o, axis=-1, keepdims=True) + EPS) * gnw_ref[...]
        og = og_ref[:, vs].astype(F32)
        ygla.append((o * (og * _sigmoid(og))).astype(BF16))
    br_g = _dot(jnp.concatenate(ygla, axis=1), wbg_ref[...])
    br_p = _dot(ypool, wbp_ref[...])
    merged = (_sigmoid(g0_ref[...].astype(F32)) * br_g + _sigmoid(g1_ref[...].astype(F32)) * br_p)
    delta = _dot(merged.astype(BF16), wout_ref[...])
    h_new = h_ref[...] + jnp.where(is_pad, 0.0, delta)
    hx_ref[:, 0:d_model] = h_new

    ms = jnp.mean(h_new * h_new, axis=-1, keepdims=True)
    xn = (h_new * lax.rsqrt(ms + EPS)) * n2_ref[...]
    x_hi, x_lo = _split(xn)
    w_hi, w_lo = _split(wr_ref[...])
    lt = _dot_nt(w_hi, x_hi) + _dot_nt(w_hi, x_lo) + _dot_nt(w_lo, x_hi)

    lg = [lt[j:j + 1, :] for j in range(N_GROUPS)]
    m = jnp.maximum(jnp.maximum(lg[0], lg[1]), jnp.maximum(lg[2], lg[3]))
    gidx = jnp.where(lg[0] == m, 0, jnp.where(lg[1] == m, 1, jnp.where(lg[2] == m, 2, 3)))
    pg_top = 1.0 / (jnp.exp(lg[0] - m) + jnp.exp(lg[1] - m) + jnp.exp(lg[2] - m) + jnp.exp(lg[3] - m))
    sel = []
    for e in range(EXPERTS_PER_GROUP):
        rows = [lt[N_GROUPS + g * EXPERTS_PER_GROUP + e:N_GROUPS + g * EXPERTS_PER_GROUP + e + 1, :]
                for g in range(N_GROUPS)]
        sel.append(jnp.where(gidx == 0, rows[0], jnp.where(gidx == 1, rows[1],
                                                           jnp.where(gidx == 2, rows[2], rows[3]))))
    m1 = jnp.maximum(jnp.maximum(sel[0], sel[1]), jnp.maximum(sel[2], sel[3]))
    i1 = jnp.where(sel[0] == m1, 0, jnp.where(sel[1] == m1, 1, jnp.where(sel[2] == m1, 2, 3)))
    neg = jnp.float32(-jnp.inf)
    rest = [jnp.where(i1 == e, neg, sel[e]) for e in range(EXPERTS_PER_GROUP)]
    m2 = jnp.maximum(jnp.maximum(rest[0], rest[1]), jnp.maximum(rest[2], rest[3]))
    i2 = jnp.where(rest[0] == m2, 0, jnp.where(rest[1] == m2, 1, jnp.where(rest[2] == m2, 2, 3)))
    r21 = jnp.exp(m2 - m1)
    w1 = pg_top / (1.0 + r21)
    w2 = pg_top * r21 / (1.0 + r21)
    lo = jnp.minimum(i1, i2)
    hi = jnp.maximum(i1, i2)
    w_lo = jnp.where(i1 < i2, w1, w2)
    w_hi = jnp.where(i1 < i2, w2, w1)
    pidx = jnp.where(lo == 0, hi - 1, jnp.where(lo == 1, hi + 1, 5))
    cls = gidx * len(PAIRS) + pidx

    crow = lax.broadcasted_iota(jnp.int32, (CLASS_ROWS, tm), 0)
    onehot = jnp.where(crow == cls, 1.0, 0.0)
    prefix = _dot(onehot.astype(BF16), tri_ref[...]) + carry_ref[:, 0:1]
    rank = jnp.sum(onehot * prefix, axis=0, keepdims=True)
    carry_ref[...] = carry_ref[...] + jnp.sum(onehot, axis=1, keepdims=True)
    cnt_ref[...] = carry_ref[...]

    irow = lax.broadcasted_iota(jnp.int32, (8, tm), 0)
    info_ref[...] = jnp.where(irow == 0, cls.astype(F32), jnp.where(irow == 1, rank, 0.0))
    wrow = lax.broadcasted_iota(jnp.int32, (ROUTE_LANES, tm), 0)
    wrows = jnp.where(wrow == 0, w_lo, jnp.where(wrow == 1, w_hi, 0.0))
    hx_ref[:, d_model:] = wrows.T


def _mix(ogla, z, h, gnw, wgrp, pscale, wbg, wbp, wout, n2, wr, *, tm, seq_rows):
    T, D = h.shape
    nblk = tm // HALO
    return pl.pallas_call(
        functools.partial(_mix_kernel, tm=tm, seq_rows=seq_rows, d_model=D),
        grid=(T // tm,),
        in_specs=[pl.BlockSpec((tm, D), lambda i: (i, 0)),
                  pl.BlockSpec((tm, D), lambda i: (i, 2)),
                  pl.BlockSpec((tm, D), lambda i: (i, 3)),
                  pl.BlockSpec((HALO, D), lambda i: (jnp.maximum(i * nblk - 1, 0), 3)),
                  pl.BlockSpec((tm, D), lambda i: (i, 4)),
                  pl.BlockSpec((tm, D), lambda i: (i, 5)),
                  pl.BlockSpec((tm, D), lambda i: (i, 0)),
                  _resident(gnw.shape),
                  _resident(wgrp.shape), _resident(pscale.shape), _resident(wbg.shape),
                  _resident(wbp.shape), _resident(wout.shape), _resident(n2.shape),
                  _resident(wr.shape)],
        out_specs=[pl.BlockSpec((tm, D + ROUTE_LANES), lambda i: (i, 0)),
                   pl.BlockSpec((8, tm), lambda i: (0, i)),
                   pl.BlockSpec((CLASS_ROWS, LANES), lambda i: (0, 0))],
        out_shape=[jax.ShapeDtypeStruct((T, D + ROUTE_LANES), F32),
                   jax.ShapeDtypeStruct((8, T), F32),
                   jax.ShapeDtypeStruct((CLASS_ROWS, LANES), F32)],
        scratch_shapes=[pltpu.VMEM((tm + HALO, D), F32),
                        pltpu.VMEM((tm + HALO, D), F32),
                        pltpu.VMEM((tm + HALO, D - D // 4), F32),
                        pltpu.VMEM((tm + HALO, D - 2 * (D // 4)), F32),
                        pltpu.VMEM((tm + HALO, D // 4), F32),
                        pltpu.VMEM((tm, tm), BF16),
                        pltpu.VMEM((CLASS_ROWS, LANES), F32)],
        compiler_params=_cparams(("arbitrary",)),
        name="mix",
    )(ogla, z, z, z, z, z, h, gnw, wgrp, pscale, wbg, wbp, wout, n2, wr)


def _dispatch_kernel(tend_ref, nu_ref, pos_ref, hx_ref, xs_ref, zero_ref, sem, zsem,
                     *, tm, tm_exp, n_tiles):
    def fill(j):
        return pltpu.make_async_copy(zero_ref, xs_ref.at[pl.ds(j * tm_exp, tm_exp)], zsem)

    @pl.when(pl.program_id(0) == 0)
    def _():
        zero_ref[...] = jnp.zeros_like(zero_ref)
        for wait in (False, True):
            for c in range(N_CLASSES):
                prev = tend_ref[c - 1] if c else 0

                @pl.when(tend_ref[c] > prev)
                def _():
                    cp = fill(tend_ref[c] - 1)
                    cp.wait() if wait else cp.start()

            def tail(j, carry):
                cp = fill(j)
                cp.wait() if wait else cp.start()
                return carry
            lax.fori_loop(nu_ref[0], n_tiles, tail, 0)

    for r in range(tm):
        pltpu.make_async_copy(hx_ref.at[pl.ds(r, 1)], xs_ref.at[pl.ds(pos_ref[r], 1)],
                              sem).start(priority=r % DMA_THREADS)
    pltpu.make_async_copy(hx_ref, xs_ref.at[pl.ds(0, tm)], sem).wait()


def _dispatch(hx, pos, tile_end, n_used, *, tm, tm_exp, n_sorted):
    T, W = hx.shape
    grid_spec = pltpu.PrefetchScalarGridSpec(
        num_scalar_prefetch=2,
        grid=(T // tm,),
        in_specs=[pl.BlockSpec((tm,), lambda i, te, nu: (i,), memory_space=pltpu.SMEM),
                  pl.BlockSpec((tm, W), lambda i, te, nu: (i, 0))],
        out_specs=pl.BlockSpec(memory_space=pl.ANY),
        scratch_shapes=[pltpu.VMEM((tm_exp, W), F32),
                        pltpu.SemaphoreType.DMA(()), pltpu.SemaphoreType.DMA(())],
    )
    return pl.pallas_call(
        functools.partial(_dispatch_kernel, tm=tm, tm_exp=tm_exp, n_tiles=n_sorted // tm_exp),
        grid_spec=grid_spec,
        out_shape=jax.ShapeDtypeStruct((n_sorted, W), F32),
        compiler_params=_cparams(("arbitrary",)),
        name="dispatch",
    )(tile_end, n_used, pos, hx)


def _collect_norm_kernel(pos_ref, ys_ref, nw_ref, o_ref, buf_ref, sem, *, tm):
    _row_gather(ys_ref, pos_ref, buf_ref, sem, tm)
    _row_gather_wait(ys_ref, buf_ref, sem, tm)
    x = buf_ref[...]
    ms = jnp.mean(x * x, axis=-1, keepdims=True)
    o_ref[...] = (x * lax.rsqrt(ms + EPS)) * nw_ref[...]


def _collect_norm(ys, pos, norm_w, *, tm):
    n_out = pos.shape[0]
    D = ys.shape[1]
    return pl.pallas_call(
        functools.partial(_collect_norm_kernel, tm=tm),
        grid=(n_out // tm,),
        in_specs=[pl.BlockSpec((tm,), lambda i: (i,), memory_space=pltpu.SMEM),
                  pl.BlockSpec(memory_space=pl.ANY),
                  _resident(norm_w.shape)],
        out_specs=pl.BlockSpec((tm, D), lambda i: (i, 0)),
        out_shape=jax.ShapeDtypeStruct((n_out, D), F32),
        scratch_shapes=[pltpu.VMEM((tm, D), F32), pltpu.SemaphoreType.DMA(())],
        compiler_params=_cparams(("arbitrary",)),
        name="collect_norm",
    )(pos, ys, norm_w)


def _moe_kernel(ea_ref, eb_ref, nu_ref, xs_ref, n2_ref, wga_ref, wua_ref, wda_ref,
                wgb_ref, wub_ref, wdb_ref, ys_ref, *, d_model):
    i = pl.program_id(0)

    @pl.when(i < nu_ref[0])
    def _():
        x = xs_ref[:, 0:d_model]
        ms = jnp.mean(x * x, axis=-1, keepdims=True)
        xn = ((x * lax.rsqrt(ms + EPS)) * n2_ref[...]).astype(BF16)
        y = x
        for col, (wg, wu, wd) in enumerate(((wga_ref, wua_ref, wda_ref), (wgb_ref, wub_ref, wdb_ref))):
            gate = _dot(xn, wg[...])
            up = _dot(xn, wu[...])
            hid = (gate * _sigmoid(gate) * up).astype(BF16)
            y = y + xs_ref[:, d_model + col:d_model + col + 1] * _dot(hid, wd[...])
        ys_ref[...] = y

    @pl.when(i >= nu_ref[0])
    def _():
        ys_ref[...] = jnp.zeros_like(ys_ref)


def _moe(xs, tile_ea, tile_eb, n_used, n2, weg, weu, wed, *, tm):
    n_sorted, W = xs.shape
    D = n2.shape[1]
    de = weg.shape[2]
    row = lambda i, ea, eb, nu: (jnp.minimum(i, nu[0] - 1), 0)
    wa = lambda i, ea, eb, nu: (ea[i], 0, 0)
    wb = lambda i, ea, eb, nu: (eb[i], 0, 0)
    grid_spec = pltpu.PrefetchScalarGridSpec(
        num_scalar_prefetch=3,
        grid=(n_sorted // tm,),
        in_specs=[pl.BlockSpec((tm, W), row),
                  pl.BlockSpec((1, D), lambda i, ea, eb, nu: (0, 0)),
                  pl.BlockSpec((None, D, de), wa), pl.BlockSpec((None, D, de), wa),
                  pl.BlockSpec((None, de, D), wa),
                  pl.BlockSpec((None, D, de), wb), pl.BlockSpec((None, D, de), wb),
                  pl.BlockSpec((None, de, D), wb)],
        out_specs=pl.BlockSpec((tm, D), lambda i, ea, eb, nu: (i, 0)),
    )
    return pl.pallas_call(
        functools.partial(_moe_kernel, d_model=D),
        grid_spec=grid_spec,
        out_shape=jax.ShapeDtypeStruct((n_sorted, D), F32),
        compiler_params=_cparams(("arbitrary",)),
        name="experts",
    )(tile_ea, tile_eb, n_used, xs, n2, weg, weu, wed, weg, weu, wed)


def _row_tile(T, cap):
    t = cap
    while T % t:
        t //= 2
    return t


def kernel(x, meta_tokens, norm1_w, w_in, w_gate_up, b_gate, gla_norm_w, w_pool_grp, pool_scale,
           w_br_gla, w_br_pool, w_out, norm2_w, w_router_group, w_router_expert, w_exp_gate,
           w_exp_up, w_exp_down, final_norm_w):
    B, S, D = x.shape
    depth = w_in.shape[0]
    key = w_gate_up.shape[2]
    rank = w_gate_up.shape[1]
    dv = gla_norm_w.shape[1]
    val = GLA_HEADS * dv
    dk = key // GLA_HEADS
    assert S % CHUNK == 0 and key * 2 == D and val == D
    assert w_pool_grp.shape[2] * len(POOL_WINDOWS) == D
    LP = PAD + N_META + S
    n_chunks = LP // CHUNK
    T = B * LP
    tm_proj = _row_tile(T, 512)
    tm_mix = _row_tile(T, 512)
    tm_row = _row_tile(T, 512)
    tm_exp = 256
    n_sorted = -(-(T + N_CLASSES * (tm_exp - 1)) // tm_exp) * tm_exp
    n_tiles = n_sorted // tm_exp

    head = jnp.concatenate([jnp.zeros((PAD, D), F32), meta_tokens.astype(F32)], axis=0)
    lp = np.arange(T) % LP
    ys = x.astype(F32).reshape(B * S, D)
    pos = jnp.asarray(np.where(lp >= CHUNK, (np.arange(T) // LP) * S + lp - CHUNK, 0), jnp.int32)

    c_q, c_k, c_v, c_og = 0, key, 2 * key, 2 * key + val
    c_gl = c_og + val
    c_u = c_gl + rank
    c_gt = c_u + D
    n_main = 2 * key + 2 * val + D + 2 * D

    for l in range(depth):
        wl = w_in[l]
        w_all = jnp.concatenate(
            [wl[:, c_q:c_gl], wl[:, c_u:], wl[:, c_gl:c_u],
             jnp.zeros((D, LANES - rank), F32)], axis=1).astype(BF16)
        z, zg, h = _inproj_gather(ys, pos, head, norm1_w[l][None], w_all, tm=tm_proj, n_main=n_main,
                                  seq_rows=LP if l == 0 else 0, n_batch=B)

        wgu = jnp.concatenate([w_gate_up[l], jnp.zeros((LANES - rank, key), F32)], axis=0)
        ogla = _gla(z, zg, wgu, b_gate[l][None], batch=B, n_chunks=n_chunks, dk=dk, dv=dv)

        wr = jnp.concatenate([w_router_group[l], w_router_expert[l],
                              jnp.zeros((D, CLASS_ROWS - N_GROUPS - N_EXPERTS), F32)], axis=1).T
        hx, info, counts = _mix(
            ogla, z, h, gla_norm_w[l][None], w_pool_grp[l].astype(BF16), pool_scale[l][None],
            w_br_gla[l].astype(BF16), w_br_pool[l].astype(BF16), w_out[l].astype(BF16),
            norm2_w[l][None], wr, tm=tm_mix, seq_rows=LP)

        cnt = counts[:N_CLASSES, 0].astype(jnp.int32)
        seg_tiles = (cnt + tm_exp - 1) // tm_exp
        tile_end = jnp.cumsum(seg_tiles)
        start = (tile_end - seg_tiles) * tm_exp
        n_used = tile_end[-1:]
        tile_cls = jnp.minimum(
            jnp.sum(jnp.arange(n_tiles)[:, None] >= tile_end[None, :], axis=1), N_CLASSES - 1)
        tile_cls = jnp.where(jnp.arange(n_tiles) < n_used[0], tile_cls,
                             jnp.take(tile_cls, jnp.maximum(n_used[0] - 1, 0)))
        pair_lo = jnp.array([p[0] for p in PAIRS], jnp.int32)
        pair_hi = jnp.array([p[1] for p in PAIRS], jnp.int32)
        grp = tile_cls // len(PAIRS)
        tile_ea = (grp * EXPERTS_PER_GROUP + jnp.take(pair_lo, tile_cls % len(PAIRS))).astype(jnp.int32)
        tile_eb = (grp * EXPERTS_PER_GROUP + jnp.take(pair_hi, tile_cls % len(PAIRS))).astype(jnp.int32)
        cls_t = info[0].astype(jnp.int32)
        pos = jnp.take(start, cls_t) + info[1].astype(jnp.int32)

        xs = _dispatch(hx, pos, tile_end.astype(jnp.int32), n_used.astype(jnp.int32),
                       tm=tm_row, tm_exp=tm_exp, n_sorted=n_sorted)
        ys = _moe(xs, tile_ea, tile_eb, n_used.astype(jnp.int32), norm2_w[l][None],
                  w_exp_gate[l].astype(BF16), w_exp_up[l].astype(BF16), w_exp_down[l].astype(BF16),
                  tm=tm_exp)

    pos_out = pos.reshape(B, LP)[:, PAD + N_META:].reshape(B * S)
    out = _collect_norm(ys, pos_out, final_norm_w[None], tm=_row_tile(B * S, 512))
    return out.reshape(B, S, D)
```

```python
import functools

import jax
import jax.numpy as jnp
import numpy as np
from jax import lax
from jax.experimental import pallas as pl
from jax.experimental.pallas import tpu as pltpu

F32 = jnp.float32
BF16 = jnp.bfloat16

EPS = 1e-6
CHUNK = 64
N_META = 16
PAD = CHUNK - N_META
GLA_HEADS = 4
GATE_NORMALIZER = 16.0
POOL_WINDOWS = (2, 4, 8, 16)
N_GROUPS = 4
EXPERTS_PER_GROUP = 4
N_EXPERTS = N_GROUPS * EXPERTS_PER_GROUP
PAIRS = ((0, 1), (0, 2), (0, 3), (1, 2), (1, 3), (2, 3))
N_CLASSES = N_GROUPS * len(PAIRS)
CLASS_ROWS = 32
LANES = 128
HALO = 32
ROUTE_LANES = 128
VMEM_LIMIT = 56 * 1024 * 1024
DMA_THREADS = 2
MIX_SUBTILES = 1


def _cparams(sem):
    return pltpu.CompilerParams(dimension_semantics=sem, vmem_limit_bytes=VMEM_LIMIT)


def _resident(shape):
    nd = len(shape)
    return pl.BlockSpec(shape, lambda *_: (0,) * nd, pipeline_mode=pl.Buffered(1))


def _layer_block(arr, layer):
    nd = arr.ndim
    return pl.BlockSpec((None,) + arr.shape[1:], lambda *_: (layer,) + (0,) * (nd - 1),
                        pipeline_mode=pl.Buffered(1))


def _split(x):
    hi = x.astype(BF16)
    lo = (x - hi.astype(F32)).astype(BF16)
    return hi, lo


def _dot(a, b):
    return jnp.dot(a, b, preferred_element_type=F32)


def _dot_nt(a, b):
    return lax.dot_general(a, b, (((1,), (1,)), ((), ())), preferred_element_type=F32)


def _dot_tn(a, b):
    return lax.dot_general(a, b, (((0,), (0,)), ((), ())), preferred_element_type=F32)


def _sigmoid(x):
    return 0.5 * jnp.tanh(0.5 * x) + 0.5


def _row_gather(src_ref, pos_ref, dst_ref, sem, tm):
    for r in range(tm):
        pltpu.make_async_copy(src_ref.at[pl.ds(pos_ref[r], 1)], dst_ref.at[pl.ds(r, 1)],
                              sem).start(priority=r % DMA_THREADS)


def _row_gather_wait(src_ref, dst_ref, sem, tm):
    pltpu.make_async_copy(src_ref.at[pl.ds(0, tm)], dst_ref, sem).wait()


def _inproj_gather_kernel(pos0_ref, posn_ref, ys_ref, head_ref, nw_ref, w_ref, z_ref, zg_ref, h_ref,
                          buf_ref, xn_ref, sem, *, tm, n_main, col_blk, seq_rows, n_batch):
    i = pl.program_id(0)
    last = pl.num_programs(0) - 1
    slot = i % 2

    @pl.when(i == 0)
    def _():
        _row_gather(ys_ref, pos0_ref, buf_ref.at[0], sem.at[0], tm)

    _row_gather_wait(ys_ref, buf_ref.at[slot], sem.at[slot], tm)
    if seq_rows:
        b0 = (i * tm + seq_rows - 1) // seq_rows
        for k in range(tm // seq_rows + 1):
            off = (b0 + k) * seq_rows - i * tm

            @pl.when(jnp.logical_and(off < tm, b0 + k < n_batch))
            def _():
                buf_ref[slot, pl.ds(pl.multiple_of(off, CHUNK), CHUNK), :] = head_ref[...]

    x = buf_ref[slot]
    h_ref[...] = x
    ms = jnp.mean(x * x, axis=-1, keepdims=True)
    xn_ref[...] = ((x * lax.rsqrt(ms + EPS)) * nw_ref[...]).astype(BF16)
    _row_gather(ys_ref, posn_ref, buf_ref.at[1 - slot], sem.at[1 - slot], tm)
    for j in range(n_main // col_blk):
        sl = slice(j * col_blk, (j + 1) * col_blk)
        z_ref[:, sl] = _dot(xn_ref[...], w_ref[:, sl]).astype(BF16)
    zg_ref[...] = _dot(xn_ref[...], w_ref[:, n_main:])

    @pl.when(i == last)
    def _():
        _row_gather_wait(ys_ref, buf_ref.at[1 - slot], sem.at[1 - slot], tm)


def _inproj_gather(ys, pos, head, norm_w, w_all, layer, *, tm, n_main, seq_rows=0, n_batch=0):
    T = pos.shape[0]
    D = ys.shape[1]
    n_all = w_all.shape[2]
    n_steps = T // tm
    assert tm % CHUNK == 0 and seq_rows % CHUNK == 0
    return pl.pallas_call(
        functools.partial(_inproj_gather_kernel, tm=tm, n_main=n_main, col_blk=512,
                          seq_rows=seq_rows, n_batch=n_batch),
        grid=(n_steps,),
        in_specs=[pl.BlockSpec((tm,), lambda i: (0,), memory_space=pltpu.SMEM),
                  pl.BlockSpec((tm,), lambda i: (jnp.minimum(i + 1, n_steps - 1),),
                               memory_space=pltpu.SMEM),
                  pl.BlockSpec(memory_space=pl.ANY),
                  _resident(head.shape),
                  _layer_block(norm_w, layer),
                  _layer_block(w_all, layer)],
        out_specs=[pl.BlockSpec((tm, n_main), lambda i: (i, 0)),
                   pl.BlockSpec((tm, n_all - n_main), lambda i: (i, 0)),
                   pl.BlockSpec((tm, D), lambda i: (i, 0))],
        out_shape=[jax.ShapeDtypeStruct((T, n_main), BF16),
                   jax.ShapeDtypeStruct((T, n_all - n_main), F32),
                   jax.ShapeDtypeStruct((T, D), F32)],
        scratch_shapes=[pltpu.VMEM((2, tm, D), F32), pltpu.VMEM((tm, D), BF16),
                        pltpu.SemaphoreType.DMA((2,))],
        compiler_params=_cparams(("arbitrary",)),
        name="inproj_gather",
    )(pos, pos, ys, head, norm_w, w_all)


def _gla_prep(q_ref, k_ref, zg_ref, wgh_ref, wgl_ref, bg_ref, scaled_ref, dec_ref, slot, first,
              *, bb, dk):
    rows = bb * CHUNK
    key = GLA_HEADS * dk
    gl_hi, gl_lo = _split(zg_ref[...].reshape(rows, LANES))
    gpre = (_dot(gl_hi, wgh_ref[...]) + _dot(gl_hi, wgl_ref[...]) + _dot(gl_lo, wgh_ref[...])
            + bg_ref[...])
    g = (jnp.minimum(gpre, 0.0) - jnp.log(1.0 + jnp.exp(-jnp.abs(gpre)))) * (1.0 / GATE_NORMALIZER)
    if first:
        row = lax.broadcasted_iota(jnp.int32, (rows, 1), 0)
        g = jnp.where((row & (CHUNK - 1)) < PAD, 0.0, g)

    ri = lax.broadcasted_iota(jnp.int32, (rows, rows), 0)
    ci = lax.broadcasted_iota(jnp.int32, (rows, rows), 1)
    tri = jnp.where(jnp.logical_and((ri // CHUNK) == (ci // CHUNK), ri >= ci), 1.0, 0.0).astype(BF16)
    g_hi, g_lo = _split(g)
    bcum = _dot(tri, g_hi) + _dot(tri, g_lo)
    gam = jnp.concatenate(
        [jnp.broadcast_to(bcum[(c + 1) * CHUNK - 1:(c + 1) * CHUNK, :], (CHUNK, key)) for c in range(bb)],
        axis=0)
    eb = jnp.exp(bcum)
    ieb = jnp.exp(-bcum)
    q = q_ref[...].reshape(rows, key).astype(F32) * (dk ** -0.5)
    k = k_ref[...].reshape(rows, key).astype(F32)
    scaled_ref[slot, 0] = (q * eb).astype(BF16)
    scaled_ref[slot, 1] = (q * ieb).astype(BF16)
    scaled_ref[slot, 2] = (k * eb).astype(BF16)
    scaled_ref[slot, 3] = (k * ieb).astype(BF16)
    scaled_ref[slot, 4] = (k * jnp.exp(gam - bcum)).astype(BF16)
    for c in range(bb):
        dec_ref[slot, c] = eb[(c + 1) * CHUNK - 1:(c + 1) * CHUNK, :]


def _gla_kernel(q0_ref, k0_ref, zg0_ref, qn_ref, kn_ref, zgn_ref, v_ref, wgh_ref, wgl_ref, bg_ref,
                o_ref, st_ref, scaled_ref, dec_ref, *, bb, dk, dv):
    n = pl.program_id(1)
    prep = functools.partial(_gla_prep, wgh_ref=wgh_ref, wgl_ref=wgl_ref, bg_ref=bg_ref,
                             scaled_ref=scaled_ref, dec_ref=dec_ref, bb=bb, dk=dk)

    @pl.when(n == 0)
    def _():
        st_ref[...] = jnp.zeros_like(st_ref)
        prep(q0_ref, k0_ref, zg0_ref, slot=0, first=True)

    li = lax.broadcasted_iota(jnp.int32, (CHUNK, CHUNK), 0)
    lj = lax.broadcasted_iota(jnp.int32, (CHUNK, CHUNK), 1)
    lower = li >= lj
    groups = [(c, hd) for c in range(bb) for hd in range(GLA_HEADS)]

    def step(slot):
        prep(qn_ref, kn_ref, zgn_ref, slot=1 - slot, first=False)
        attn = {}
        for c, hd in groups:
            rs = slice(c * CHUNK, (c + 1) * CHUNK)
            ks = slice(hd * dk, (hd + 1) * dk)
            a_lo = _dot_nt(scaled_ref[slot, 0, rs, ks], scaled_ref[slot, 3, rs, ks])
            a_up = _dot_nt(scaled_ref[slot, 1, rs, ks], scaled_ref[slot, 2, rs, ks])
            attn[c, hd] = jnp.where(lower, a_lo, a_up).astype(BF16)
        for c, hd in groups:
            rs = slice(c * CHUNK, (c + 1) * CHUNK)
            ks = slice(hd * dk, (hd + 1) * dk)
            vs = slice(hd * dv, (hd + 1) * dv)
            o_ref[c, :, vs] = (_dot(attn[c, hd], v_ref[c, :, vs])
                               + _dot(scaled_ref[slot, 0, rs, ks], st_ref[c, hd].astype(BF16))
                               ).astype(BF16)
        for c, hd in groups:
            rs = slice(c * CHUNK, (c + 1) * CHUNK)
            ks = slice(hd * dk, (hd + 1) * dk)
            vs = slice(hd * dv, (hd + 1) * dv)
            dec = jnp.broadcast_to(dec_ref[slot, c, :, ks], (dk, dk)).T
            st_ref[c, hd] = (st_ref[c, hd] * jnp.concatenate([dec] * (dv // dk), axis=1)
                             + _dot_tn(scaled_ref[slot, 4, rs, ks], v_ref[c, :, vs]))

    for slot in (0, 1):
        pl.when(n % 2 == slot)(functools.partial(step, slot))


def _gla(z, zg, wg_hi, wg_lo, bg, layer, *, batch, n_chunks, dk, dv):
    T = z.shape[0]
    key = GLA_HEADS * dk
    val = GLA_HEADS * dv
    bb = 4 if batch % 4 == 0 else 1
    z3 = z.reshape(batch, n_chunks * CHUNK, z.shape[1])
    zg3 = zg.reshape(batch, n_chunks * CHUNK, zg.shape[1])
    nxt = lambda n: jnp.minimum(n + 1, n_chunks - 1)
    o = pl.pallas_call(
        functools.partial(_gla_kernel, bb=bb, dk=dk, dv=dv),
        grid=(batch // bb, n_chunks),
        in_specs=[pl.BlockSpec((bb, CHUNK, key), lambda b, n: (b, 0, 0)),
                  pl.BlockSpec((bb, CHUNK, key), lambda b, n: (b, 0, 1)),
                  pl.BlockSpec((bb, CHUNK, LANES), lambda b, n: (b, 0, 0)),
                  pl.BlockSpec((bb, CHUNK, key), lambda b, n: (b, nxt(n), 0)),
                  pl.BlockSpec((bb, CHUNK, key), lambda b, n: (b, nxt(n), 1)),
                  pl.BlockSpec((bb, CHUNK, LANES), lambda b, n: (b, nxt(n), 0)),
                  pl.BlockSpec((bb, CHUNK, val), lambda b, n: (b, n, 1)),
                  _layer_block(wg_hi, layer), _layer_block(wg_lo, layer), _layer_block(bg, layer)],
        out_specs=pl.BlockSpec((bb, CHUNK, val), lambda b, n: (b, n, 0)),
        out_shape=jax.ShapeDtypeStruct((batch, n_chunks * CHUNK, val), BF16),
        scratch_shapes=[pltpu.VMEM((bb, GLA_HEADS, dk, dv), F32),
                        pltpu.VMEM((2, 5, bb * CHUNK, key), BF16),
                        pltpu.VMEM((2, bb, 1, key), F32)],
        compiler_params=_cparams(("parallel", "arbitrary")),
        name="gla",
    )(z3, z3, zg3, z3, z3, zg3, z3, wg_hi, wg_lo, bg)
    return o.reshape(T, val)


def _mix_kernel(ogla_ref, og_ref, u_ref, halo_ref, g0_ref, g1_ref, h_ref, gnw_ref,
                wgrp_ref, pscale_ref, wbg_ref, wbp_ref, wout_ref, n2_ref, wr_ref,
                hx_ref, info_ref, cnt_ref,
                ue_ref, p2_ref, p4_ref, p8_ref, p16_ref, tri_ref, carry_ref,
                *, tm, seq_rows, d_model):
    i = pl.program_id(0)
    gd = d_model // len(POOL_WINDOWS)

    @pl.when(i == 0)
    def _():
        r = lax.broadcasted_iota(jnp.int32, (tm, tm), 0)
        c = lax.broadcasted_iota(jnp.int32, (tm, tm), 1)
        tri_ref[...] = jnp.where(r < c, 1.0, 0.0).astype(BF16)
        carry_ref[...] = jnp.zeros_like(carry_ref)

    ue_ref[0:HALO, :] = halo_ref[...].astype(F32)
    ue_ref[HALO:, :] = u_ref[...].astype(F32)
    n8 = tm + HALO - 8
    p2_ref[8:, :] = ue_ref[8:, :] + ue_ref[pl.ds(7, n8), :]
    p4_ref[16:, :] = p2_ref[16:, gd:] + p2_ref[pl.ds(14, n8 - 8), gd:]
    p8_ref[24:, :] = p4_ref[24:, gd:] + p4_ref[pl.ds(20, n8 - 16), gd:]
    p16_ref[HALO:, :] = p8_ref[HALO:, gd:] + p8_ref[pl.ds(24, tm), gd:]
    sum_refs = (p2_ref, p4_ref, p8_ref, p16_ref)

    sub = tm // MIX_SUBTILES
    for part in range(MIX_SUBTILES):
        _mix_rows(part * sub, sub, i * tm, ogla_ref, og_ref, g0_ref, g1_ref, h_ref, gnw_ref, wgrp_ref,
                  pscale_ref, wbg_ref, wbp_ref, wout_ref, n2_ref, wr_ref, hx_ref, info_ref,
                  ue_ref, sum_refs, tri_ref, carry_ref, seq_rows=seq_rows, d_model=d_model)
    cnt_ref[...] = carry_ref[...]


def _mix_rows(r0, rows, tile_row0, ogla_ref, og_ref, g0_ref, g1_ref, h_ref, gnw_ref, wgrp_ref,
              pscale_ref, wbg_ref, wbp_ref, wout_ref, n2_ref, wr_ref, hx_ref, info_ref,
              ue_ref, sum_refs, tri_ref, carry_ref, *, seq_rows, d_model):
    gd = d_model // len(POOL_WINDOWS)
    rs = pl.ds(r0, rows)
    rh = pl.ds(r0 + HALO, rows)

    rowf = (tile_row0 + r0 + lax.broadcasted_iota(jnp.int32, (rows, 1), 0)).astype(F32)
    lp = rowf - jnp.floor(rowf / float(seq_rows)) * float(seq_rows)
    is_pad = lp < float(PAD)
    seqpos1 = lp - float(PAD - 1)

    ypool = []
    for gi, w in enumerate(POOL_WINDOWS):
        cs = slice(gi * gd, (gi + 1) * gd)
        inv_cnt = 1.0 / jnp.clip(seqpos1, 1.0, float(w))
        pooled = sum_refs[gi][rh, 0:gd] * inv_cnt - ue_ref[rh, cs]
        mixed = _dot(pooled.astype(BF16), wgrp_ref[gi]) * pscale_ref[:, cs]
        ypool.append(mixed.astype(BF16))
    ypool = jnp.concatenate(ypool, axis=1)

    dv = gnw_ref.shape[1]
    ygla = []
    for hd in range(GLA_HEADS):
        vs = slice(hd * dv, (hd + 1) * dv)
        o = ogla_ref[rs, vs].astype(F32)
        o = o * lax.rsqrt(jnp.mean(o * o, axis=-1, keepdims=True) + EPS) * gnw_ref[...]
        og = og_ref[rs, vs]
        ygla.append(o.astype(BF16) * (og * _sigmoid(og)))
    br_g = _dot(jnp.concatenate(ygla, axis=1), wbg_ref[...])
    br_p = _dot(ypool, wbp_ref[...])
    merged = (_sigmoid(g0_ref[rs, :]) * br_g.astype(BF16) + _sigmoid(g1_ref[rs, :]) * br_p.astype(BF16))
    delta = _dot(merged, wout_ref[...])
    h_new = h_ref[rs, :] + jnp.where(is_pad, 0.0, delta)
    hx_ref[rs, 0:d_model] = h_new

    ms = jnp.mean(h_new * h_new, axis=-1, keepdims=True)
    xn = (h_new * lax.rsqrt(ms + EPS)) * n2_ref[...]
    x_hi, x_lo = _split(xn)
    w_hi, w_lo = _split(wr_ref[...])
    lt = _dot_nt(w_hi, x_hi) + _dot_nt(w_hi, x_lo) + _dot_nt(w_lo, x_hi)

    lg = [lt[j:j + 1, :] for j in range(N_GROUPS)]
    m = jnp.maximum(jnp.maximum(lg[0], lg[1]), jnp.maximum(lg[2], lg[3]))
    gidx = jnp.where(lg[0] == m, 0, jnp.where(lg[1] == m, 1, jnp.where(lg[2] == m, 2, 3)))
    pg_top = 1.0 / (jnp.exp(lg[0] - m) + jnp.exp(lg[1] - m) + jnp.exp(lg[2] - m) + jnp.exp(lg[3] - m))
    sel = []
    for e in range(EXPERTS_PER_GROUP):
        cand = [lt[N_GROUPS + g * EXPERTS_PER_GROUP + e:N_GROUPS + g * EXPERTS_PER_GROUP + e + 1, :]
                for g in range(N_GROUPS)]
        sel.append(jnp.where(gidx == 0, cand[0], jnp.where(gidx == 1, cand[1],
                                                           jnp.where(gidx == 2, cand[2], cand[3]))))
    m1 = jnp.maximum(jnp.maximum(sel[0], sel[1]), jnp.maximum(sel[2], sel[3]))
    i1 = jnp.where(sel[0] == m1, 0, jnp.where(sel[1] == m1, 1, jnp.where(sel[2] == m1, 2, 3)))
    neg = jnp.float32(-jnp.inf)
    rest = [jnp.where(i1 == e, neg, sel[e]) for e in range(EXPERTS_PER_GROUP)]
    m2 = jnp.maximum(jnp.maximum(rest[0], rest[1]), jnp.maximum(rest[2], rest[3]))
    i2 = jnp.where(rest[0] == m2, 0, jnp.where(rest[1] == m2, 1, jnp.where(rest[2] == m2, 2, 3)))
    r21 = jnp.exp(m2 - m1)
    w1 = pg_top / (1.0 + r21)
    w2 = pg_top * r21 / (1.0 + r21)
    lo = jnp.minimum(i1, i2)
    hi = jnp.maximum(i1, i2)
    w_lo = jnp.where(i1 < i2, w1, w2)
    w_hi = jnp.where(i1 < i2, w2, w1)
    pidx = jnp.where(lo == 0, hi - 1, jnp.where(lo == 1, hi + 1, 5))
    cls = gidx * len(PAIRS) + pidx

    crow = lax.broadcasted_iota(jnp.int32, (CLASS_ROWS, rows), 0)
    onehot = jnp.where(crow == cls, 1.0, 0.0)
    prefix = _dot(onehot.astype(BF16), tri_ref[0:rows, 0:rows]) + carry_ref[:, 0:1]
    rank = jnp.sum(onehot * prefix, axis=0, keepdims=True)
    carry_ref[...] = carry_ref[...] + jnp.sum(onehot, axis=1, keepdims=True)

    irow = lax.broadcasted_iota(jnp.int32, (8, rows), 0)
    info_ref[:, rs] = jnp.where(irow == 0, cls.astype(F32), jnp.where(irow == 1, rank, 0.0))
    wrow = lax.broadcasted_iota(jnp.int32, (ROUTE_LANES, rows), 0)
    wrows = jnp.where(wrow == 0, w_lo, jnp.where(wrow == 1, w_hi, 0.0))
    hx_ref[rs, d_model:] = wrows.T


def _mix(ogla, z, h, layer_params, layer, *, tm, seq_rows):
    T, D = h.shape
    nblk = tm // HALO
    return pl.pallas_call(
        functools.partial(_mix_kernel, tm=tm, seq_rows=seq_rows, d_model=D),
        grid=(T // tm,),
        in_specs=[pl.BlockSpec((tm, D), lambda i: (i, 0)),
                  pl.BlockSpec((tm, D), lambda i: (i, 2)),
                  pl.BlockSpec((tm, D), lambda i: (i, 3)),
                  pl.BlockSpec((HALO, D), lambda i: (jnp.maximum(i * nblk - 1, 0), 3)),
                  pl.BlockSpec((tm, D), lambda i: (i, 4)),
                  pl.BlockSpec((tm, D), lambda i: (i, 5)),
                  pl.BlockSpec((tm, D), lambda i: (i, 0))]
                 + [_layer_block(p, layer) for p in layer_params],
        out_specs=[pl.BlockSpec((tm, D + ROUTE_LANES), lambda i: (i, 0)),
                   pl.BlockSpec((8, tm), lambda i: (0, i)),
                   pl.BlockSpec((CLASS_ROWS, LANES), lambda i: (0, 0))],
        out_shape=[jax.ShapeDtypeStruct((T, D + ROUTE_LANES), F32),
                   jax.ShapeDtypeStruct((8, T), F32),
                   jax.ShapeDtypeStruct((CLASS_ROWS, LANES), F32)],
        scratch_shapes=[pltpu.VMEM((tm + HALO, D), F32),
                        pltpu.VMEM((tm + HALO, D), F32),
                        pltpu.VMEM((tm + HALO, D - D // 4), F32),
                        pltpu.VMEM((tm + HALO, D - 2 * (D // 4)), F32),
                        pltpu.VMEM((tm + HALO, D // 4), F32),
                        pltpu.VMEM((tm, tm), BF16),
                        pltpu.VMEM((CLASS_ROWS, LANES), F32)],
        compiler_params=_cparams(("arbitrary",)),
        name="mix",
    )(ogla, z, z, z, z, z, h, *layer_params)


def _dispatch_kernel(tend_ref, nu_ref, pos_ref, hx_ref, xs_ref, zero_ref, sem, zsem,
                     *, tm, tm_exp, n_tiles):
    def fill(j):
        return pltpu.make_async_copy(zero_ref, xs_ref.at[pl.ds(j * tm_exp, tm_exp)], zsem)

    @pl.when(pl.program_id(0) == 0)
    def _():
        zero_ref[...] = jnp.zeros_like(zero_ref)
        for wait in (False, True):
            for c in range(N_CLASSES):
                prev = tend_ref[c - 1] if c else 0

                @pl.when(tend_ref[c] > prev)
                def _():
                    cp = fill(tend_ref[c] - 1)
                    cp.wait() if wait else cp.start()

            def tail(j, carry):
                cp = fill(j)
                cp.wait() if wait else cp.start()
                return carry
            lax.fori_loop(nu_ref[0], n_tiles, tail, 0)

    for r in range(tm):
        pltpu.make_async_copy(hx_ref.at[pl.ds(r, 1)], xs_ref.at[pl.ds(pos_ref[r], 1)],
                              sem).start(priority=r % DMA_THREADS)
    pltpu.make_async_copy(hx_ref, xs_ref.at[pl.ds(0, tm)], sem).wait()


def _dispatch(hx, pos, tile_end, n_used, *, tm, tm_exp, n_sorted):
    T, W = hx.shape
    grid_spec = pltpu.PrefetchScalarGridSpec(
        num_scalar_prefetch=2,
        grid=(T // tm,),
        in_specs=[pl.BlockSpec((tm,), lambda i, te, nu: (i,), memory_space=pltpu.SMEM),
                  pl.BlockSpec((tm, W), lambda i, te, nu: (i, 0))],
        out_specs=pl.BlockSpec(memory_space=pl.ANY),
        scratch_shapes=[pltpu.VMEM((tm_exp, W), F32),
                        pltpu.SemaphoreType.DMA(()), pltpu.SemaphoreType.DMA(())],
    )
    return pl.pallas_call(
        functools.partial(_dispatch_kernel, tm=tm, tm_exp=tm_exp, n_tiles=n_sorted // tm_exp),
        grid_spec=grid_spec,
        out_shape=jax.ShapeDtypeStruct((n_sorted, W), F32),
        compiler_params=_cparams(("arbitrary",)),
        name="dispatch",
    )(tile_end, n_used, pos, hx)


def _collect_norm_kernel(pos_ref, ys_ref, nw_ref, o_ref, buf_ref, sem, *, tm):
    _row_gather(ys_ref, pos_ref, buf_ref, sem, tm)
    _row_gather_wait(ys_ref, buf_ref, sem, tm)
    x = buf_ref[...]
    ms = jnp.mean(x * x, axis=-1, keepdims=True)
    o_ref[...] = (x * lax.rsqrt(ms + EPS)) * nw_ref[...]


def _collect_norm(ys, pos, norm_w, *, tm):
    n_out = pos.shape[0]
    D = ys.shape[1]
    return pl.pallas_call(
        functools.partial(_collect_norm_kernel, tm=tm),
        grid=(n_out // tm,),
        in_specs=[pl.BlockSpec((tm,), lambda i: (i,), memory_space=pltpu.SMEM),
                  pl.BlockSpec(memory_space=pl.ANY),
                  _resident(norm_w.shape)],
        out_specs=pl.BlockSpec((tm, D), lambda i: (i, 0)),
        out_shape=jax.ShapeDtypeStruct((n_out, D), F32),
        scratch_shapes=[pltpu.VMEM((tm, D), F32), pltpu.SemaphoreType.DMA(())],
        compiler_params=_cparams(("arbitrary",)),
        name="collect_norm",
    )(pos, ys, norm_w)


def _moe_kernel(ea_ref, eb_ref, nu_ref, xs_ref, n2_ref, wga_ref, wua_ref, wda_ref,
                wgb_ref, wub_ref, wdb_ref, ys_ref, *, d_model):
    i = pl.program_id(0)

    @pl.when(i < nu_ref[0])
    def _():
        x = xs_ref[:, 0:d_model]
        ms = jnp.mean(x * x, axis=-1, keepdims=True)
        xn = ((x * lax.rsqrt(ms + EPS)) * n2_ref[...]).astype(BF16)
        y = x
        for col, (wg, wu, wd) in enumerate(((wga_ref, wua_ref, wda_ref), (wgb_ref, wub_ref, wdb_ref))):
            gate = _dot(xn, wg[...])
            up = _dot(xn, wu[...])
            hid = (gate * _sigmoid(gate) * up).astype(BF16)
            y = y + xs_ref[:, d_model + col:d_model + col + 1] * _dot(hid, wd[...])
        ys_ref[...] = y

    @pl.when(i >= nu_ref[0])
    def _():
        ys_ref[...] = jnp.zeros_like(ys_ref)


def _moe(xs, tile_ea, tile_eb, n_used, n2, weg, weu, wed, layer, *, tm):
    n_sorted, W = xs.shape
    D = n2.shape[2]
    de = weg.shape[3]
    row = lambda i, ea, eb, nu: (jnp.minimum(i, nu[0] - 1), 0)
    wa = lambda i, ea, eb, nu: (layer, ea[i], 0, 0)
    wb = lambda i, ea, eb, nu: (layer, eb[i], 0, 0)
    grid_spec = pltpu.PrefetchScalarGridSpec(
        num_scalar_prefetch=3,
        grid=(n_sorted // tm,),
        in_specs=[pl.BlockSpec((tm, W), row),
                  pl.BlockSpec((None, 1, D), lambda i, ea, eb, nu: (layer, 0, 0)),
                  pl.BlockSpec((None, None, D, de), wa), pl.BlockSpec((None, None, D, de), wa),
                  pl.BlockSpec((None, None, de, D), wa),
                  pl.BlockSpec((None, None, D, de), wb), pl.BlockSpec((None, None, D, de), wb),
                  pl.BlockSpec((None, None, de, D), wb)],
        out_specs=pl.BlockSpec((tm, D), lambda i, ea, eb, nu: (i, 0)),
    )
    return pl.pallas_call(
        functools.partial(_moe_kernel, d_model=D),
        grid_spec=grid_spec,
        out_shape=jax.ShapeDtypeStruct((n_sorted, D), F32),
        compiler_params=_cparams(("arbitrary",)),
        name="experts",
    )(tile_ea, tile_eb, n_used, xs, n2, weg, weu, wed, weg, weu, wed)


def _row_tile(T, cap):
    t = cap
    while T % t:
        t //= 2
    return t


def kernel(x, meta_tokens, norm1_w, w_in, w_gate_up, b_gate, gla_norm_w, w_pool_grp, pool_scale,
           w_br_gla, w_br_pool, w_out, norm2_w, w_router_group, w_router_expert, w_exp_gate,
           w_exp_up, w_exp_down, final_norm_w):
    B, S, D = x.shape
    depth = w_in.shape[0]
    key = w_gate_up.shape[2]
    rank = w_gate_up.shape[1]
    dv = gla_norm_w.shape[1]
    val = GLA_HEADS * dv
    dk = key // GLA_HEADS
    assert S % CHUNK == 0 and key * 2 == D and val == D
    assert w_pool_grp.shape[2] * len(POOL_WINDOWS) == D
    LP = PAD + N_META + S
    n_chunks = LP // CHUNK
    T = B * LP
    tm_proj = _row_tile(T, 512)
    tm_mix = _row_tile(T, 512)
    tm_row = _row_tile(T, 512)
    tm_exp = 256
    n_sorted = -(-(T + N_CLASSES * (tm_exp - 1)) // tm_exp) * tm_exp
    n_tiles = n_sorted // tm_exp

    head = jnp.concatenate([jnp.zeros((PAD, D), F32), meta_tokens.astype(F32)], axis=0)
    lp = np.arange(T) % LP
    ys = x.astype(F32).reshape(B * S, D)
    pos = jnp.asarray(np.where(lp >= CHUNK, (np.arange(T) // LP) * S + lp - CHUNK, 0), jnp.int32)

    c_q, c_k, c_v, c_og = 0, key, 2 * key, 2 * key + val
    c_gl = c_og + val
    c_u = c_gl + rank
    c_gt = c_u + D
    n_main = 2 * key + 2 * val + D + 2 * D

    w_in_b = w_in.astype(BF16)
    w_all = jnp.concatenate(
        [w_in_b[:, :, c_q:c_gl], w_in_b[:, :, c_u:], w_in_b[:, :, c_gl:c_u],
         jnp.zeros((depth, D, LANES - rank), BF16)], axis=2)
    wg_hi, wg_lo = _split(jnp.concatenate(
        [w_gate_up, jnp.zeros((depth, LANES - rank, key), F32)], axis=1))
    wr = jnp.swapaxes(jnp.concatenate(
        [w_router_group, w_router_expert,
         jnp.zeros((depth, D, CLASS_ROWS - N_GROUPS - N_EXPERTS), F32)], axis=2), 1, 2)
    row_vec = lambda p: p[:, None, :]
    mix_params = (row_vec(gla_norm_w), w_pool_grp.astype(BF16), row_vec(pool_scale),
                  w_br_gla.astype(BF16), w_br_pool.astype(BF16), w_out.astype(BF16),
                  row_vec(norm2_w), wr)
    weg, weu, wed = w_exp_gate.astype(BF16), w_exp_up.astype(BF16), w_exp_down.astype(BF16)

    for l in range(depth):
        z, zg, h = _inproj_gather(ys, pos, head, row_vec(norm1_w), w_all, l, tm=tm_proj, n_main=n_main,
                                  seq_rows=LP if l == 0 else 0, n_batch=B)
        ogla = _gla(z, zg, wg_hi, wg_lo, row_vec(b_gate), l, batch=B, n_chunks=n_chunks, dk=dk, dv=dv)
        hx, info, counts = _mix(ogla, z, h, mix_params, l, tm=tm_mix, seq_rows=LP)

        cnt = counts[:N_CLASSES, 0].astype(jnp.int32)
        seg_tiles = (cnt + tm_exp - 1) // tm_exp
        tile_end = jnp.cumsum(seg_tiles)
        start = (tile_end - seg_tiles) * tm_exp
        n_used = tile_end[-1:]
        tile_cls = jnp.minimum(
            jnp.sum(jnp.arange(n_tiles)[:, None] >= tile_end[None, :], axis=1), N_CLASSES - 1)
        tile_cls = jnp.where(jnp.arange(n_tiles) < n_used[0], tile_cls,
                             jnp.take(tile_cls, jnp.maximum(n_used[0] - 1, 0)))
        pair_lo = jnp.array([p[0] for p in PAIRS], jnp.int32)
        pair_hi = jnp.array([p[1] for p in PAIRS], jnp.int32)
        grp = tile_cls // len(PAIRS)
        tile_ea = (grp * EXPERTS_PER_GROUP + jnp.take(pair_lo, tile_cls % len(PAIRS))).astype(jnp.int32)
        tile_eb = (grp * EXPERTS_PER_GROUP + jnp.take(pair_hi, tile_cls % len(PAIRS))).astype(jnp.int32)
        cls_t = info[0].astype(jnp.int32)
        pos = info[1].astype(jnp.int32) + jnp.sum(
            jnp.where(cls_t[:, None] == jnp.arange(N_CLASSES)[None, :], start[None, :], 0), axis=1)

        xs = _dispatch(hx, pos, tile_end.astype(jnp.int32), n_used.astype(jnp.int32),
                       tm=tm_row, tm_exp=tm_exp, n_sorted=n_sorted)
        ys = _moe(xs, tile_ea, tile_eb, n_used.astype(jnp.int32), row_vec(norm2_w), weg, weu, wed, l,
                  tm=tm_exp)

    pos_out = pos.reshape(B, LP)[:, PAD + N_META:].reshape(B * S)
    out = _collect_norm(ys, pos_out, final_norm_w[None], tm=_row_tile(B * S, 512))
    return out.reshape(B, S, D)
```

```python
import functools

import jax
import jax.numpy as jnp
import numpy as np
from jax import lax
from jax.experimental import pallas as pl
from jax.experimental.pallas import tpu as pltpu

F32 = jnp.float32
BF16 = jnp.bfloat16

EPS = 1e-6
CHUNK = 64
N_META = 16
PAD = CHUNK - N_META
GLA_HEADS = 4
GATE_NORMALIZER = 16.0
POOL_WINDOWS = (2, 4, 8, 16)
N_GROUPS = 4
EXPERTS_PER_GROUP = 4
N_EXPERTS = N_GROUPS * EXPERTS_PER_GROUP
PAIRS = ((0, 1), (0, 2), (0, 3), (1, 2), (1, 3), (2, 3))
N_CLASSES = N_GROUPS * len(PAIRS)
CLASS_ROWS = 32
LANES = 128
HALO = 32
ROUTE_LANES = 128
VMEM_LIMIT = 56 * 1024 * 1024
DMA_THREADS = 2
MIX_SUBTILES = 1


def _cparams(sem):
    return pltpu.CompilerParams(dimension_semantics=sem, vmem_limit_bytes=VMEM_LIMIT)


def _resident(shape):
    nd = len(shape)
    return pl.BlockSpec(shape, lambda *_: (0,) * nd, pipeline_mode=pl.Buffered(1))


def _layer_block(arr, layer):
    nd = arr.ndim
    return pl.BlockSpec((None,) + arr.shape[1:], lambda *_: (layer,) + (0,) * (nd - 1),
                        pipeline_mode=pl.Buffered(1))


def _split(x):
    hi = x.astype(BF16)
    lo = (x - hi.astype(F32)).astype(BF16)
    return hi, lo


def _dot(a, b):
    return jnp.dot(a, b, preferred_element_type=F32)


def _dot_nt(a, b):
    return lax.dot_general(a, b, (((1,), (1,)), ((), ())), preferred_element_type=F32)


def _dot_tn(a, b):
    return lax.dot_general(a, b, (((0,), (0,)), ((), ())), preferred_element_type=F32)


def _sigmoid(x):
    return 0.5 * jnp.tanh(0.5 * x) + 0.5


def _row_gather(src_ref, pos_ref, dst_ref, sem, tm):
    for r in range(tm):
        pltpu.make_async_copy(src_ref.at[pl.ds(pos_ref[r], 1)], dst_ref.at[pl.ds(r, 1)],
                              sem).start(priority=r % DMA_THREADS)


def _row_gather_wait(src_ref, dst_ref, sem, tm):
    pltpu.make_async_copy(src_ref.at[pl.ds(0, tm)], dst_ref, sem).wait()


def _inproj_gather_kernel(pos0_ref, posn_ref, ys_ref, head_ref, nw_ref, wa_ref, wb_ref, wg_ref,
                          z_ref, zg_ref, h_ref, buf_ref, xn_ref, sem,
                          *, tm, col_blk, seq_rows, n_batch):
    i = pl.program_id(0)
    last = pl.num_programs(0) - 1
    slot = i % 2

    @pl.when(i == 0)
    def _():
        _row_gather(ys_ref, pos0_ref, buf_ref.at[0], sem.at[0], tm)

    _row_gather_wait(ys_ref, buf_ref.at[slot], sem.at[slot], tm)
    if seq_rows:
        b0 = (i * tm + seq_rows - 1) // seq_rows
        for k in range(tm // seq_rows + 1):
            off = (b0 + k) * seq_rows - i * tm

            @pl.when(jnp.logical_and(off < tm, b0 + k < n_batch))
            def _():
                buf_ref[slot, pl.ds(pl.multiple_of(off, CHUNK), CHUNK), :] = head_ref[...]

    x = buf_ref[slot]
    h_ref[...] = x
    ms = jnp.mean(x * x, axis=-1, keepdims=True)
    xn_ref[...] = ((x * lax.rsqrt(ms + EPS)) * nw_ref[...]).astype(BF16)
    _row_gather(ys_ref, posn_ref, buf_ref.at[1 - slot], sem.at[1 - slot], tm)
    col = 0
    for w_ref in (wa_ref, wb_ref):
        for j in range(w_ref.shape[1] // col_blk):
            sl = slice(j * col_blk, (j + 1) * col_blk)
            z_ref[:, col:col + col_blk] = _dot(xn_ref[...], w_ref[:, sl]).astype(BF16)
            col += col_blk
    zg_ref[...] = _dot(xn_ref[...], wg_ref[...])

    @pl.when(i == last)
    def _():
        _row_gather_wait(ys_ref, buf_ref.at[1 - slot], sem.at[1 - slot], tm)


def _inproj_gather(ys, pos, head, norm_w, w_a, w_b, w_g, layer, *, tm, seq_rows=0, n_batch=0):
    T = pos.shape[0]
    D = ys.shape[1]
    n_main = w_a.shape[2] + w_b.shape[2]
    n_all = n_main + w_g.shape[2]
    n_steps = T // tm
    assert tm % CHUNK == 0 and seq_rows % CHUNK == 0
    return pl.pallas_call(
        functools.partial(_inproj_gather_kernel, tm=tm, col_blk=512,
                          seq_rows=seq_rows, n_batch=n_batch),
        grid=(n_steps,),
        in_specs=[pl.BlockSpec((tm,), lambda i: (0,), memory_space=pltpu.SMEM),
                  pl.BlockSpec((tm,), lambda i: (jnp.minimum(i + 1, n_steps - 1),),
                               memory_space=pltpu.SMEM),
                  pl.BlockSpec(memory_space=pl.ANY),
                  _resident(head.shape),
                  _layer_block(norm_w, layer), _layer_block(w_a, layer),
                  _layer_block(w_b, layer), _layer_block(w_g, layer)],
        out_specs=[pl.BlockSpec((tm, n_main), lambda i: (i, 0)),
                   pl.BlockSpec((tm, n_all - n_main), lambda i: (i, 0)),
                   pl.BlockSpec((tm, D), lambda i: (i, 0))],
        out_shape=[jax.ShapeDtypeStruct((T, n_main), BF16),
                   jax.ShapeDtypeStruct((T, n_all - n_main), F32),
                   jax.ShapeDtypeStruct((T, D), F32)],
        scratch_shapes=[pltpu.VMEM((2, tm, D), F32), pltpu.VMEM((tm, D), BF16),
                        pltpu.SemaphoreType.DMA((2,))],
        compiler_params=_cparams(("arbitrary",)),
        name="inproj_gather",
    )(pos, pos, ys, head, norm_w, w_a, w_b, w_g)


def _gla_prep(q_ref, k_ref, zg_ref, wgh_ref, wgl_ref, bg_ref, scaled_ref, dec_ref, slot, first,
              *, bb, dk):
    rows = bb * CHUNK
    key = GLA_HEADS * dk
    gl_hi, gl_lo = _split(zg_ref[...].reshape(rows, LANES))
    gpre = (_dot(gl_hi, wgh_ref[...]) + _dot(gl_hi, wgl_ref[...]) + _dot(gl_lo, wgh_ref[...])
            + bg_ref[...])
    g = (jnp.minimum(gpre, 0.0) - jnp.log(1.0 + jnp.exp(-jnp.abs(gpre)))) * (1.0 / GATE_NORMALIZER)
    if first:
        row = lax.broadcasted_iota(jnp.int32, (rows, 1), 0)
        g = jnp.where((row & (CHUNK - 1)) < PAD, 0.0, g)

    ri = lax.broadcasted_iota(jnp.int32, (rows, rows), 0)
    ci = lax.broadcasted_iota(jnp.int32, (rows, rows), 1)
    tri = jnp.where(jnp.logical_and((ri // CHUNK) == (ci // CHUNK), ri >= ci), 1.0, 0.0).astype(BF16)
    g_hi, g_lo = _split(g)
    bcum = _dot(tri, g_hi) + _dot(tri, g_lo)
    gam = jnp.concatenate(
        [jnp.broadcast_to(bcum[(c + 1) * CHUNK - 1:(c + 1) * CHUNK, :], (CHUNK, key)) for c in range(bb)],
        axis=0)
    eb = jnp.exp(bcum)
    ieb = jnp.exp(-bcum)
    q = q_ref[...].reshape(rows, key).astype(F32) * (dk ** -0.5)
    k = k_ref[...].reshape(rows, key).astype(F32)
    scaled_ref[slot, 0] = (q * eb).astype(BF16)
    scaled_ref[slot, 1] = (q * ieb).astype(BF16)
    scaled_ref[slot, 2] = (k * eb).astype(BF16)
    scaled_ref[slot, 3] = (k * ieb).astype(BF16)
    scaled_ref[slot, 4] = (k * jnp.exp(gam - bcum)).astype(BF16)
    for c in range(bb):
        dec_ref[slot, c] = eb[(c + 1) * CHUNK - 1:(c + 1) * CHUNK, :]


def _gla_kernel(q0_ref, k0_ref, zg0_ref, qn_ref, kn_ref, zgn_ref, v_ref, wgh_ref, wgl_ref, bg_ref,
                o_ref, st_ref, scaled_ref, dec_ref, *, bb, dk, dv):
    n = pl.program_id(1)
    prep = functools.partial(_gla_prep, wgh_ref=wgh_ref, wgl_ref=wgl_ref, bg_ref=bg_ref,
                             scaled_ref=scaled_ref, dec_ref=dec_ref, bb=bb, dk=dk)

    @pl.when(n == 0)
    def _():
        st_ref[...] = jnp.zeros_like(st_ref)
        prep(q0_ref, k0_ref, zg0_ref, slot=0, first=True)

    li = lax.broadcasted_iota(jnp.int32, (CHUNK, CHUNK), 0)
    lj = lax.broadcasted_iota(jnp.int32, (CHUNK, CHUNK), 1)
    lower = li >= lj
    groups = [(c, hd) for c in range(bb) for hd in range(GLA_HEADS)]

    def step(slot):
        prep(qn_ref, kn_ref, zgn_ref, slot=1 - slot, first=False)
        attn = {}
        for c, hd in groups:
            rs = slice(c * CHUNK, (c + 1) * CHUNK)
            ks = slice(hd * dk, (hd + 1) * dk)
            a_lo = _dot_nt(scaled_ref[slot, 0, rs, ks], scaled_ref[slot, 3, rs, ks])
            a_up = _dot_nt(scaled_ref[slot, 1, rs, ks], scaled_ref[slot, 2, rs, ks])
            attn[c, hd] = jnp.where(lower, a_lo, a_up).astype(BF16)
        for c, hd in groups:
            rs = slice(c * CHUNK, (c + 1) * CHUNK)
            ks = slice(hd * dk, (hd + 1) * dk)
            vs = slice(hd * dv, (hd + 1) * dv)
            o_ref[c, :, vs] = (_dot(attn[c, hd], v_ref[c, :, vs])
                               + _dot(scaled_ref[slot, 0, rs, ks], st_ref[c, hd].astype(BF16))
                               ).astype(BF16)
        for c, hd in groups:
            rs = slice(c * CHUNK, (c + 1) * CHUNK)
            ks = slice(hd * dk, (hd + 1) * dk)
            vs = slice(hd * dv, (hd + 1) * dv)
            dec = jnp.broadcast_to(dec_ref[slot, c, :, ks], (dk, dk)).T
            st_ref[c, hd] = (st_ref[c, hd] * jnp.concatenate([dec] * (dv // dk), axis=1)
                             + _dot_tn(scaled_ref[slot, 4, rs, ks], v_ref[c, :, vs]))

    for slot in (0, 1):
        pl.when(n % 2 == slot)(functools.partial(step, slot))


def _gla(z, zg, wg_hi, wg_lo, bg, layer, *, batch, n_chunks, dk, dv):
    T = z.shape[0]
    key = GLA_HEADS * dk
    val = GLA_HEADS * dv
    bb = 4 if batch % 4 == 0 else 1
    z3 = z.reshape(batch, n_chunks * CHUNK, z.shape[1])
    zg3 = zg.reshape(batch, n_chunks * CHUNK, zg.shape[1])
    nxt = lambda n: jnp.minimum(n + 1, n_chunks - 1)
    o = pl.pallas_call(
        functools.partial(_gla_kernel, bb=bb, dk=dk, dv=dv),
        grid=(batch // bb, n_chunks),
        in_specs=[pl.BlockSpec((bb, CHUNK, key), lambda b, n: (b, 0, 0)),
                  pl.BlockSpec((bb, CHUNK, key), lambda b, n: (b, 0, 1)),
                  pl.BlockSpec((bb, CHUNK, LANES), lambda b, n: (b, 0, 0)),
                  pl.BlockSpec((bb, CHUNK, key), lambda b, n: (b, nxt(n), 0)),
                  pl.BlockSpec((bb, CHUNK, key), lambda b, n: (b, nxt(n), 1)),
                  pl.BlockSpec((bb, CHUNK, LANES), lambda b, n: (b, nxt(n), 0)),
                  pl.BlockSpec((bb, CHUNK, val), lambda b, n: (b, n, 1)),
                  _layer_block(wg_hi, layer), _layer_block(wg_lo, layer), _layer_block(bg, layer)],
        out_specs=pl.BlockSpec((bb, CHUNK, val), lambda b, n: (b, n, 0)),
        out_shape=jax.ShapeDtypeStruct((batch, n_chunks * CHUNK, val), BF16),
        scratch_shapes=[pltpu.VMEM((bb, GLA_HEADS, dk, dv), F32),
                        pltpu.VMEM((2, 5, bb * CHUNK, key), BF16),
                        pltpu.VMEM((2, bb, 1, key), F32)],
        compiler_params=_cparams(("parallel", "arbitrary")),
        name="gla",
    )(z3, z3, zg3, z3, z3, zg3, z3, wg_hi, wg_lo, bg)
    return o.reshape(T, val)


def _mix_kernel(ogla_ref, og_ref, u_ref, halo_ref, g0_ref, g1_ref, h_ref, gnw_ref,
                wgrp_ref, pscale_ref, wbg_ref, wbp_ref, wout_ref, n2_ref, wr_ref,
                hx_ref, info_ref, cnt_ref,
                ue_ref, p2_ref, p4_ref, p8_ref, p16_ref, tri_ref, carry_ref,
                *, tm, seq_rows, d_model):
    i = pl.program_id(0)
    gd = d_model // len(POOL_WINDOWS)

    @pl.when(i == 0)
    def _():
        r = lax.broadcasted_iota(jnp.int32, (tm, tm), 0)
        c = lax.broadcasted_iota(jnp.int32, (tm, tm), 1)
        tri_ref[...] = jnp.where(r < c, 1.0, 0.0).astype(BF16)
        carry_ref[...] = jnp.zeros_like(carry_ref)

    ue_ref[0:HALO, :] = halo_ref[...].astype(F32)
    ue_ref[HALO:, :] = u_ref[...].astype(F32)
    n8 = tm + HALO - 8
    p2_ref[8:, :] = ue_ref[8:, :] + ue_ref[pl.ds(7, n8), :]
    p4_ref[16:, :] = p2_ref[16:, gd:] + p2_ref[pl.ds(14, n8 - 8), gd:]
    p8_ref[24:, :] = p4_ref[24:, gd:] + p4_ref[pl.ds(20, n8 - 16), gd:]
    p16_ref[HALO:, :] = p8_ref[HALO:, gd:] + p8_ref[pl.ds(24, tm), gd:]
    sum_refs = (p2_ref, p4_ref, p8_ref, p16_ref)

    sub = tm // MIX_SUBTILES
    for part in range(MIX_SUBTILES):
        _mix_rows(part * sub, sub, i * tm, ogla_ref, og_ref, g0_ref, g1_ref, h_ref, gnw_ref, wgrp_ref,
                  pscale_ref, wbg_ref, wbp_ref, wout_ref, n2_ref, wr_ref, hx_ref, info_ref,
                  ue_ref, sum_refs, tri_ref, carry_ref, seq_rows=seq_rows, d_model=d_model)
    cnt_ref[...] = carry_ref[...]


def _mix_rows(r0, rows, tile_row0, ogla_ref, og_ref, g0_ref, g1_ref, h_ref, gnw_ref, wgrp_ref,
              pscale_ref, wbg_ref, wbp_ref, wout_ref, n2_ref, wr_ref, hx_ref, info_ref,
              ue_ref, sum_refs, tri_ref, carry_ref, *, seq_rows, d_model):
    gd = d_model // len(POOL_WINDOWS)
    rs = pl.ds(r0, rows)
    rh = pl.ds(r0 + HALO, rows)

    rowf = (tile_row0 + r0 + lax.broadcasted_iota(jnp.int32, (rows, 1), 0)).astype(F32)
    lp = rowf - jnp.floor(rowf / float(seq_rows)) * float(seq_rows)
    is_pad = lp < float(PAD)
    seqpos1 = lp - float(PAD - 1)

    br_p = None
    for gi, w in enumerate(POOL_WINDOWS):
        cs = slice(gi * gd, (gi + 1) * gd)
        inv_cnt = 1.0 / jnp.clip(seqpos1, 1.0, float(w))
        pooled = sum_refs[gi][rh, 0:gd] * inv_cnt - ue_ref[rh, cs]
        mixed = _dot(pooled.astype(BF16), wgrp_ref[gi]) * pscale_ref[:, cs]
        part = _dot(mixed.astype(BF16), wbp_ref[cs, :])
        br_p = part if br_p is None else br_p + part

    dv = gnw_ref.shape[1]
    br_g = None
    for hd in range(GLA_HEADS):
        vs = slice(hd * dv, (hd + 1) * dv)
        o = ogla_ref[rs, vs].astype(F32)
        o = o * lax.rsqrt(jnp.mean(o * o, axis=-1, keepdims=True) + EPS) * gnw_ref[...]
        og = og_ref[rs, vs]
        ygla = o.astype(BF16) * (og * _sigmoid(og))
        part = _dot(ygla, wbg_ref[vs, :])
        br_g = part if br_g is None else br_g + part
    merged = (_sigmoid(g0_ref[rs, :]) * br_g.astype(BF16) + _sigmoid(g1_ref[rs, :]) * br_p.astype(BF16))
    delta = _dot(merged, wout_ref[...])
    h_new = h_ref[rs, :] + jnp.where(is_pad, 0.0, delta)
    hx_ref[rs, :] = h_new

    ms = jnp.mean(h_new * h_new, axis=-1, keepdims=True)
    xn = (h_new * lax.rsqrt(ms + EPS)) * n2_ref[...]
    x_hi, x_lo = _split(xn)
    w_hi, w_lo = _split(wr_ref[...])
    lt = _dot_nt(w_hi, x_hi) + _dot_nt(w_hi, x_lo) + _dot_nt(w_lo, x_hi)

    lg = [lt[j:j + 1, :] for j in range(N_GROUPS)]
    m = jnp.maximum(jnp.maximum(lg[0], lg[1]), jnp.maximum(lg[2], lg[3]))
    gidx = jnp.where(lg[0] == m, 0, jnp.where(lg[1] == m, 1, jnp.where(lg[2] == m, 2, 3)))
    pg_top = 1.0 / (jnp.exp(lg[0] - m) + jnp.exp(lg[1] - m) + jnp.exp(lg[2] - m) + jnp.exp(lg[3] - m))
    sel = []
    for e in range(EXPERTS_PER_GROUP):
        cand = [lt[N_GROUPS + g * EXPERTS_PER_GROUP + e:N_GROUPS + g * EXPERTS_PER_GROUP + e + 1, :]
                for g in range(N_GROUPS)]
        sel.append(jnp.where(gidx == 0, cand[0], jnp.where(gidx == 1, cand[1],
                                                           jnp.where(gidx == 2, cand[2], cand[3]))))
    m1 = jnp.maximum(jnp.maximum(sel[0], sel[1]), jnp.maximum(sel[2], sel[3]))
    i1 = jnp.where(sel[0] == m1, 0, jnp.where(sel[1] == m1, 1, jnp.where(sel[2] == m1, 2, 3)))
    neg = jnp.float32(-jnp.inf)
    rest = [jnp.where(i1 == e, neg, sel[e]) for e in range(EXPERTS_PER_GROUP)]
    m2 = jnp.maximum(jnp.maximum(rest[0], rest[1]), jnp.maximum(rest[2], rest[3]))
    i2 = jnp.where(rest[0] == m2, 0, jnp.where(rest[1] == m2, 1, jnp.where(rest[2] == m2, 2, 3)))
    r21 = jnp.exp(m2 - m1)
    w1 = pg_top / (1.0 + r21)
    w2 = pg_top * r21 / (1.0 + r21)
    lo = jnp.minimum(i1, i2)
    hi = jnp.maximum(i1, i2)
    w_lo = jnp.where(i1 < i2, w1, w2)
    w_hi = jnp.where(i1 < i2, w2, w1)
    pidx = jnp.where(lo == 0, hi - 1, jnp.where(lo == 1, hi + 1, 5))
    cls = gidx * len(PAIRS) + pidx

    crow = lax.broadcasted_iota(jnp.int32, (CLASS_ROWS, rows), 0)
    onehot = jnp.where(crow == cls, 1.0, 0.0)
    prefix = _dot(onehot.astype(BF16), tri_ref[0:rows, 0:rows]) + carry_ref[:, 0:1]
    rank = jnp.sum(onehot * prefix, axis=0, keepdims=True)
    carry_ref[...] = carry_ref[...] + jnp.sum(onehot, axis=1, keepdims=True)

    irow = lax.broadcasted_iota(jnp.int32, (8, rows), 0)
    info_ref[:, rs] = jnp.where(irow == 0, cls.astype(F32), jnp.where(
        irow == 1, rank, jnp.where(irow == 2, w_lo, jnp.where(irow == 3, w_hi, 0.0))))


def _mix(ogla, z, h, layer_params, layer, *, tm, seq_rows):
    T, D = h.shape
    nblk = tm // HALO
    return pl.pallas_call(
        functools.partial(_mix_kernel, tm=tm, seq_rows=seq_rows, d_model=D),
        grid=(T // tm,),
        in_specs=[pl.BlockSpec((tm, D), lambda i: (i, 0)),
                  pl.BlockSpec((tm, D), lambda i: (i, 2)),
                  pl.BlockSpec((tm, D), lambda i: (i, 3)),
                  pl.BlockSpec((HALO, D), lambda i: (jnp.maximum(i * nblk - 1, 0), 3)),
                  pl.BlockSpec((tm, D), lambda i: (i, 4)),
                  pl.BlockSpec((tm, D), lambda i: (i, 5)),
                  pl.BlockSpec((tm, D), lambda i: (i, 0))]
                 + [_layer_block(p, layer) for p in layer_params],
        out_specs=[pl.BlockSpec((tm, D), lambda i: (i, 0)),
                   pl.BlockSpec((8, tm), lambda i: (0, i)),
                   pl.BlockSpec((CLASS_ROWS, LANES), lambda i: (0, 0))],
        out_shape=[jax.ShapeDtypeStruct((T, D), F32),
                   jax.ShapeDtypeStruct((8, T), F32),
                   jax.ShapeDtypeStruct((CLASS_ROWS, LANES), F32)],
        scratch_shapes=[pltpu.VMEM((tm + HALO, D), F32),
                        pltpu.VMEM((tm + HALO, D), F32),
                        pltpu.VMEM((tm + HALO, D - D // 4), F32),
                        pltpu.VMEM((tm + HALO, D - 2 * (D // 4)), F32),
                        pltpu.VMEM((tm + HALO, D // 4), F32),
                        pltpu.VMEM((tm, tm), BF16),
                        pltpu.VMEM((CLASS_ROWS, LANES), F32)],
        compiler_params=_cparams(("arbitrary",)),
        name="mix",
    )(ogla, z, z, z, z, z, h, *layer_params)


def _dispatch_kernel(tend_ref, nu_ref, pos_ref, h_ref, info_ref, xs_ref, row_ref, zero_ref, sem, zsem,
                     *, tm, tm_exp, n_tiles, d_model):
    def fill(j):
        return pltpu.make_async_copy(zero_ref, xs_ref.at[pl.ds(j * tm_exp, tm_exp)], zsem)

    @pl.when(pl.program_id(0) == 0)
    def _():
        zero_ref[...] = jnp.zeros_like(zero_ref)
        for wait in (False, True):
            for c in range(N_CLASSES):
                prev = tend_ref[c - 1] if c else 0

                @pl.when(tend_ref[c] > prev)
                def _():
                    cp = fill(tend_ref[c] - 1)
                    cp.wait() if wait else cp.start()

            def tail(j, carry):
                cp = fill(j)
                cp.wait() if wait else cp.start()
                return carry
            lax.fori_loop(nu_ref[0], n_tiles, tail, 0)

    row_ref[:, 0:d_model] = h_ref[...]
    lane = lax.broadcasted_iota(jnp.int32, (ROUTE_LANES, tm), 0)
    row_ref[:, d_model:] = jnp.where(lane == 0, info_ref[2:3, :],
                                     jnp.where(lane == 1, info_ref[3:4, :], 0.0)).T
    for r in range(tm):
        pltpu.make_async_copy(row_ref.at[pl.ds(r, 1)], xs_ref.at[pl.ds(pos_ref[r], 1)],
                              sem).start(priority=r % DMA_THREADS)
    pltpu.make_async_copy(row_ref, xs_ref.at[pl.ds(0, tm)], sem).wait()


def _dispatch(h, info, pos, tile_end, n_used, *, tm, tm_exp, n_sorted):
    T, D = h.shape
    W = D + ROUTE_LANES
    grid_spec = pltpu.PrefetchScalarGridSpec(
        num_scalar_prefetch=2,
        grid=(T // tm,),
        in_specs=[pl.BlockSpec((tm,), lambda i, te, nu: (i,), memory_space=pltpu.SMEM),
                  pl.BlockSpec((tm, D), lambda i, te, nu: (i, 0)),
                  pl.BlockSpec((8, tm), lambda i, te, nu: (0, i))],
        out_specs=pl.BlockSpec(memory_space=pl.ANY),
        scratch_shapes=[pltpu.VMEM((tm, W), F32), pltpu.VMEM((tm_exp, W), F32),
                        pltpu.SemaphoreType.DMA(()), pltpu.SemaphoreType.DMA(())],
    )
    return pl.pallas_call(
        functools.partial(_dispatch_kernel, tm=tm, tm_exp=tm_exp, n_tiles=n_sorted // tm_exp,
                          d_model=D),
        grid_spec=grid_spec,
        out_shape=jax.ShapeDtypeStruct((n_sorted, W), F32),
        compiler_params=_cparams(("arbitrary",)),
        name="dispatch",
    )(tile_end, n_used, pos, h, info)


def _collect_norm_kernel(pos_ref, ys_ref, nw_ref, o_ref, buf_ref, sem, *, tm):
    _row_gather(ys_ref, pos_ref, buf_ref, sem, tm)
    _row_gather_wait(ys_ref, buf_ref, sem, tm)
    x = buf_ref[...]
    ms = jnp.mean(x * x, axis=-1, keepdims=True)
    o_ref[...] = (x * lax.rsqrt(ms + EPS)) * nw_ref[...]


def _collect_norm(ys, pos, norm_w, *, tm):
    n_out = pos.shape[0]
    D = ys.shape[1]
    return pl.pallas_call(
        functools.partial(_collect_norm_kernel, tm=tm),
        grid=(n_out // tm,),
        in_specs=[pl.BlockSpec((tm,), lambda i: (i,), memory_space=pltpu.SMEM),
                  pl.BlockSpec(memory_space=pl.ANY),
                  _resident(norm_w.shape)],
        out_specs=pl.BlockSpec((tm, D), lambda i: (i, 0)),
        out_shape=jax.ShapeDtypeStruct((n_out, D), F32),
        scratch_shapes=[pltpu.VMEM((tm, D), F32), pltpu.SemaphoreType.DMA(())],
        compiler_params=_cparams(("arbitrary",)),
        name="collect_norm",
    )(pos, ys, norm_w)


def _moe_kernel(ea_ref, eb_ref, nu_ref, xs_ref, n2_ref, wga_ref, wua_ref, wda_ref,
                wgb_ref, wub_ref, wdb_ref, ys_ref, *, d_model):
    i = pl.program_id(0)

    @pl.when(i < nu_ref[0])
    def _():
        x = xs_ref[:, 0:d_model]
        ms = jnp.mean(x * x, axis=-1, keepdims=True)
        xn = ((x * lax.rsqrt(ms + EPS)) * n2_ref[...]).astype(BF16)
        y = x
        for col, (wg, wu, wd) in enumerate(((wga_ref, wua_ref, wda_ref), (wgb_ref, wub_ref, wdb_ref))):
            gate = _dot(xn, wg[...])
            up = _dot(xn, wu[...])
            hid = (gate * _sigmoid(gate) * up).astype(BF16)
            y = y + xs_ref[:, d_model + col:d_model + col + 1] * _dot(hid, wd[...])
        ys_ref[...] = y

    @pl.when(i >= nu_ref[0])
    def _():
        ys_ref[...] = jnp.zeros_like(ys_ref)


def _moe(xs, tile_ea, tile_eb, n_used, n2, weg, weu, wed, layer, *, tm):
    n_sorted, W = xs.shape
    D = n2.shape[2]
    de = weg.shape[3]
    row = lambda i, ea, eb, nu: (jnp.minimum(i, nu[0] - 1), 0)
    wa = lambda i, ea, eb, nu: (layer, ea[i], 0, 0)
    wb = lambda i, ea, eb, nu: (layer, eb[i], 0, 0)
    grid_spec = pltpu.PrefetchScalarGridSpec(
        num_scalar_prefetch=3,
        grid=(n_sorted // tm,),
        in_specs=[pl.BlockSpec((tm, W), row),
                  pl.BlockSpec((None, 1, D), lambda i, ea, eb, nu: (layer, 0, 0)),
                  pl.BlockSpec((None, None, D, de), wa), pl.BlockSpec((None, None, D, de), wa),
                  pl.BlockSpec((None, None, de, D), wa),
                  pl.BlockSpec((None, None, D, de), wb), pl.BlockSpec((None, None, D, de), wb),
                  pl.BlockSpec((None, None, de, D), wb)],
        out_specs=pl.BlockSpec((tm, D), lambda i, ea, eb, nu: (i, 0)),
    )
    return pl.pallas_call(
        functools.partial(_moe_kernel, d_model=D),
        grid_spec=grid_spec,
        out_shape=jax.ShapeDtypeStruct((n_sorted, D), F32),
        compiler_params=_cparams(("arbitrary",)),
        name="experts",
    )(tile_ea, tile_eb, n_used, xs, n2, weg, weu, wed, weg, weu, wed)


def _row_tile(T, cap):
    t = cap
    while T % t:
        t //= 2
    return t


def kernel(x, meta_tokens, norm1_w, w_in, w_gate_up, b_gate, gla_norm_w, w_pool_grp, pool_scale,
           w_br_gla, w_br_pool, w_out, norm2_w, w_router_group, w_router_expert, w_exp_gate,
           w_exp_up, w_exp_down, final_norm_w):
    B, S, D = x.shape
    depth = w_in.shape[0]
    key = w_gate_up.shape[2]
    rank = w_gate_up.shape[1]
    dv = gla_norm_w.shape[1]
    val = GLA_HEADS * dv
    dk = key // GLA_HEADS
    assert S % CHUNK == 0 and key * 2 == D and val == D
    assert w_pool_grp.shape[2] * len(POOL_WINDOWS) == D
    LP = PAD + N_META + S
    n_chunks = LP // CHUNK
    T = B * LP
    tm_proj = _row_tile(T, 512)
    tm_mix = _row_tile(T, 512)
    tm_row = _row_tile(T, 512)
    tm_exp = 256
    n_sorted = -(-(T + N_CLASSES * (tm_exp - 1)) // tm_exp) * tm_exp
    n_tiles = n_sorted // tm_exp

    head = jnp.concatenate([jnp.zeros((PAD, D), F32), meta_tokens.astype(F32)], axis=0)
    lp = np.arange(T) % LP
    ys = x.astype(F32).reshape(B * S, D)
    pos = jnp.asarray(np.where(lp >= CHUNK, (np.arange(T) // LP) * S + lp - CHUNK, 0), jnp.int32)

    c_q = 0
    c_gl = 2 * key + 2 * val
    c_u = c_gl + rank

    w_a = w_in[:, :, c_q:c_gl].astype(BF16)
    w_b = w_in[:, :, c_u:].astype(BF16)
    w_g = jnp.concatenate([w_in[:, :, c_gl:c_u].astype(BF16),
                           jnp.zeros((depth, D, LANES - rank), BF16)], axis=2)
    wg_hi, wg_lo = _split(jnp.concatenate(
        [w_gate_up, jnp.zeros((depth, LANES - rank, key), F32)], axis=1))
    wr = jnp.swapaxes(jnp.concatenate(
        [w_router_group, w_router_expert,
         jnp.zeros((depth, D, CLASS_ROWS - N_GROUPS - N_EXPERTS), F32)], axis=2), 1, 2)
    row_vec = lambda p: p[:, None, :]
    mix_params = (row_vec(gla_norm_w), w_pool_grp.astype(BF16), row_vec(pool_scale),
                  w_br_gla.astype(BF16), w_br_pool.astype(BF16), w_out.astype(BF16),
                  row_vec(norm2_w), wr)
    weg, weu, wed = w_exp_gate.astype(BF16), w_exp_up.astype(BF16), w_exp_down.astype(BF16)

    for l in range(depth):
        z, zg, h = _inproj_gather(ys, pos, head, row_vec(norm1_w), w_a, w_b, w_g, l, tm=tm_proj,
                                  seq_rows=LP if l == 0 else 0, n_batch=B)
        ogla = _gla(z, zg, wg_hi, wg_lo, row_vec(b_gate), l, batch=B, n_chunks=n_chunks, dk=dk, dv=dv)
        hx, info, counts = _mix(ogla, z, h, mix_params, l, tm=tm_mix, seq_rows=LP)

        cnt = counts[:N_CLASSES, 0].astype(jnp.int32)
        seg_tiles = (cnt + tm_exp - 1) // tm_exp
        tile_end = jnp.cumsum(seg_tiles)
        start = (tile_end - seg_tiles) * tm_exp
        n_used = tile_end[-1:]
        tile_cls = jnp.minimum(
            jnp.sum(jnp.arange(n_tiles)[:, None] >= tile_end[None, :], axis=1), N_CLASSES - 1)
        tile_cls = jnp.where(jnp.arange(n_tiles) < n_used[0], tile_cls,
                             jnp.take(tile_cls, jnp.maximum(n_used[0] - 1, 0)))
        pair_lo = jnp.array([p[0] for p in PAIRS], jnp.int32)
        pair_hi = jnp.array([p[1] for p in PAIRS], jnp.int32)
        grp = tile_cls // len(PAIRS)
        tile_ea = (grp * EXPERTS_PER_GROUP + jnp.take(pair_lo, tile_cls % len(PAIRS))).astype(jnp.int32)
        tile_eb = (grp * EXPERTS_PER_GROUP + jnp.take(pair_hi, tile_cls % len(PAIRS))).astype(jnp.int32)
        cls_t = info[0].astype(jnp.int32)
        pos = info[1].astype(jnp.int32) + jnp.sum(
            jnp.where(cls_t[:, None] == jnp.arange(N_CLASSES)[None, :], start[None, :], 0), axis=1)

        xs = _dispatch(hx, info, pos, tile_end.astype(jnp.int32), n_used.astype(jnp.int32),
                       tm=tm_row, tm_exp=tm_exp, n_sorted=n_sorted)
        ys = _moe(xs, tile_ea, tile_eb, n_used.astype(jnp.int32), row_vec(norm2_w), weg, weu, wed, l,
                  tm=tm_exp)

    pos_out = pos.reshape(B, LP)[:, PAD + N_META:].reshape(B * S)
    out = _collect_norm(ys, pos_out, final_norm_w[None], tm=_row_tile(B * S, 512))
    return out.reshape(B, S, D)
```

```python
import functools

import jax
import jax.numpy as jnp
import numpy as np
from jax import lax
from jax.experimental import pallas as pl
from jax.experimental.pallas import tpu as pltpu

F32 = jnp.float32
BF16 = jnp.bfloat16

EPS = 1e-6
CHUNK = 64
N_META = 16
PAD = CHUNK - N_META
GLA_HEADS = 4
GATE_NORMALIZER = 16.0
POOL_WINDOWS = (2, 4, 8, 16)
N_GROUPS = 4
EXPERTS_PER_GROUP = 4
N_EXPERTS = N_GROUPS * EXPERTS_PER_GROUP
PAIRS = ((0, 1), (0, 2), (0, 3), (1, 2), (1, 3), (2, 3))
N_CLASSES = N_GROUPS * len(PAIRS)
CLASS_ROWS = 32
LANES = 128
HALO = 32
ROUTE_LANES = 128
DMA_THREADS = 2
MIX_SUBTILES = 1

VMEM_LIMIT = 56 * 1024 * 1024
ROW_TILE = 512
EXPERT_TILE = 256
PROJ_COL_BLOCK = 512


def _cparams(sem):
    return pltpu.CompilerParams(dimension_semantics=sem, vmem_limit_bytes=VMEM_LIMIT)


def _resident(shape):
    nd = len(shape)
    return pl.BlockSpec(shape, lambda *_: (0,) * nd, pipeline_mode=pl.Buffered(1))


def _layer_block(arr, layer):
    nd = arr.ndim
    return pl.BlockSpec((None,) + arr.shape[1:], lambda *_: (layer,) + (0,) * (nd - 1),
                        pipeline_mode=pl.Buffered(1))


def _split(x):
    hi = x.astype(BF16)
    lo = (x - hi.astype(F32)).astype(BF16)
    return hi, lo


def _dot(a, b):
    return jnp.dot(a, b, preferred_element_type=F32)


def _dot_nt(a, b):
    return lax.dot_general(a, b, (((1,), (1,)), ((), ())), preferred_element_type=F32)


def _dot_tn(a, b):
    return lax.dot_general(a, b, (((0,), (0,)), ((), ())), preferred_element_type=F32)


def _sigmoid(x):
    return 0.5 * jnp.tanh(0.5 * x) + 0.5


def _row_gather(src_ref, pos_ref, dst_ref, sem, tm):
    for r in range(tm):
        pltpu.make_async_copy(src_ref.at[pl.ds(pos_ref[r], 1)], dst_ref.at[pl.ds(r, 1)],
                              sem).start(priority=r % DMA_THREADS)


def _row_gather_wait(src_ref, dst_ref, sem, tm):
    pltpu.make_async_copy(src_ref.at[pl.ds(0, tm)], dst_ref, sem).wait()


def _inproj_gather_kernel(pos0_ref, posn_ref, ys_ref, head_ref, nw_ref, wa_ref, wb_ref, wg_ref,
                          z_ref, zg_ref, h_ref, buf_ref, xn_ref, sem,
                          *, tm, col_blk, seq_rows, n_batch):
    i = pl.program_id(0)
    last = pl.num_programs(0) - 1
    slot = i % 2

    @pl.when(i == 0)
    def _():
        _row_gather(ys_ref, pos0_ref, buf_ref.at[0], sem.at[0], tm)

    _row_gather_wait(ys_ref, buf_ref.at[slot], sem.at[slot], tm)
    if seq_rows:
        b0 = (i * tm + seq_rows - 1) // seq_rows
        for k in range(tm // seq_rows + 1):
            off = (b0 + k) * seq_rows - i * tm

            @pl.when(jnp.logical_and(off < tm, b0 + k < n_batch))
            def _():
                buf_ref[slot, pl.ds(pl.multiple_of(off, CHUNK), CHUNK), :] = head_ref[...]

    x = buf_ref[slot]
    h_ref[...] = x
    ms = jnp.mean(x * x, axis=-1, keepdims=True)
    xn_ref[...] = ((x * lax.rsqrt(ms + EPS)) * nw_ref[...]).astype(BF16)
    _row_gather(ys_ref, posn_ref, buf_ref.at[1 - slot], sem.at[1 - slot], tm)
    col = 0
    for w_ref in (wa_ref, wb_ref):
        for j in range(w_ref.shape[1] // col_blk):
            sl = slice(j * col_blk, (j + 1) * col_blk)
            z_ref[:, col:col + col_blk] = _dot(xn_ref[...], w_ref[:, sl]).astype(BF16)
            col += col_blk
    zg_ref[...] = _dot(xn_ref[...], wg_ref[...])

    @pl.when(i == last)
    def _():
        _row_gather_wait(ys_ref, buf_ref.at[1 - slot], sem.at[1 - slot], tm)


def _inproj_gather(ys, pos, head, norm_w, w_a, w_b, w_g, layer, *, tm, seq_rows=0, n_batch=0):
    T = pos.shape[0]
    D = ys.shape[1]
    n_main = w_a.shape[2] + w_b.shape[2]
    n_all = n_main + w_g.shape[2]
    n_steps = T // tm
    assert tm % CHUNK == 0 and seq_rows % CHUNK == 0
    return pl.pallas_call(
        functools.partial(_inproj_gather_kernel, tm=tm, col_blk=PROJ_COL_BLOCK,
                          seq_rows=seq_rows, n_batch=n_batch),
        grid=(n_steps,),
        in_specs=[pl.BlockSpec((tm,), lambda i: (0,), memory_space=pltpu.SMEM),
                  pl.BlockSpec((tm,), lambda i: (jnp.minimum(i + 1, n_steps - 1),),
                               memory_space=pltpu.SMEM),
                  pl.BlockSpec(memory_space=pl.ANY),
                  _resident(head.shape),
                  _layer_block(norm_w, layer), _layer_block(w_a, layer),
                  _layer_block(w_b, layer), _layer_block(w_g, layer)],
        out_specs=[pl.BlockSpec((tm, n_main), lambda i: (i, 0)),
                   pl.BlockSpec((tm, n_all - n_main), lambda i: (i, 0)),
                   pl.BlockSpec((tm, D), lambda i: (i, 0))],
        out_shape=[jax.ShapeDtypeStruct((T, n_main), BF16),
                   jax.ShapeDtypeStruct((T, n_all - n_main), F32),
                   jax.ShapeDtypeStruct((T, D), F32)],
        scratch_shapes=[pltpu.VMEM((2, tm, D), F32), pltpu.VMEM((tm, D), BF16),
                        pltpu.SemaphoreType.DMA((2,))],
        compiler_params=_cparams(("arbitrary",)),
        name="inproj_gather",
    )(pos, pos, ys, head, norm_w, w_a, w_b, w_g)


def _gla_prep(q_ref, k_ref, zg_ref, wgh_ref, wgl_ref, bg_ref, scaled_ref, dec_ref, slot, first,
              *, bb, dk):
    rows = bb * CHUNK
    key = GLA_HEADS * dk
    gl_hi, gl_lo = _split(zg_ref[...].reshape(rows, LANES))
    gpre = (_dot(gl_hi, wgh_ref[...]) + _dot(gl_hi, wgl_ref[...]) + _dot(gl_lo, wgh_ref[...])
            + bg_ref[...])
    g = (jnp.minimum(gpre, 0.0) - jnp.log(1.0 + jnp.exp(-jnp.abs(gpre)))) * (1.0 / GATE_NORMALIZER)
    if first:
        row = lax.broadcasted_iota(jnp.int32, (rows, 1), 0)
        g = jnp.where((row & (CHUNK - 1)) < PAD, 0.0, g)

    ri = lax.broadcasted_iota(jnp.int32, (rows, rows), 0)
    ci = lax.broadcasted_iota(jnp.int32, (rows, rows), 1)
    tri = jnp.where(jnp.logical_and((ri // CHUNK) == (ci // CHUNK), ri >= ci), 1.0, 0.0).astype(BF16)
    g_hi, g_lo = _split(g)
    bcum = _dot(tri, g_hi) + _dot(tri, g_lo)
    gam = jnp.concatenate(
        [jnp.broadcast_to(bcum[(c + 1) * CHUNK - 1:(c + 1) * CHUNK, :], (CHUNK, key)) for c in range(bb)],
        axis=0)
    eb = jnp.exp(bcum)
    ieb = jnp.exp(-bcum)
    q = q_ref[...].reshape(rows, key).astype(F32) * (dk ** -0.5)
    k = k_ref[...].reshape(rows, key).astype(F32)
    scaled_ref[slot, 0] = (q * eb).astype(BF16)
    scaled_ref[slot, 1] = (q * ieb).astype(BF16)
    scaled_ref[slot, 2] = (k * eb).astype(BF16)
    scaled_ref[slot, 3] = (k * ieb).astype(BF16)
    scaled_ref[slot, 4] = (k * jnp.exp(gam - bcum)).astype(BF16)
    for c in range(bb):
        dec_ref[slot, c] = eb[(c + 1) * CHUNK - 1:(c + 1) * CHUNK, :]


def _gla_kernel(q0_ref, k0_ref, zg0_ref, qn_ref, kn_ref, zgn_ref, v_ref, wgh_ref, wgl_ref, bg_ref,
                o_ref, st_ref, scaled_ref, dec_ref, *, bb, dk, dv):
    n = pl.program_id(1)
    prep = functools.partial(_gla_prep, wgh_ref=wgh_ref, wgl_ref=wgl_ref, bg_ref=bg_ref,
                             scaled_ref=scaled_ref, dec_ref=dec_ref, bb=bb, dk=dk)

    @pl.when(n == 0)
    def _():
        st_ref[...] = jnp.zeros_like(st_ref)
        prep(q0_ref, k0_ref, zg0_ref, slot=0, first=True)

    li = lax.broadcasted_iota(jnp.int32, (CHUNK, CHUNK), 0)
    lj = lax.broadcasted_iota(jnp.int32, (CHUNK, CHUNK), 1)
    lower = li >= lj
    groups = [(c, hd) for c in range(bb) for hd in range(GLA_HEADS)]

    def step(slot):
        prep(qn_ref, kn_ref, zgn_ref, slot=1 - slot, first=False)
        attn = {}
        for c, hd in groups:
            rs = slice(c * CHUNK, (c + 1) * CHUNK)
            ks = slice(hd * dk, (hd + 1) * dk)
            a_lo = _dot_nt(scaled_ref[slot, 0, rs, ks], scaled_ref[slot, 3, rs, ks])
            a_up = _dot_nt(scaled_ref[slot, 1, rs, ks], scaled_ref[slot, 2, rs, ks])
            attn[c, hd] = jnp.where(lower, a_lo, a_up).astype(BF16)
        for c, hd in groups:
            rs = slice(c * CHUNK, (c + 1) * CHUNK)
            ks = slice(hd * dk, (hd + 1) * dk)
            vs = slice(hd * dv, (hd + 1) * dv)
            o_ref[c, :, vs] = (_dot(attn[c, hd], v_ref[c, :, vs])
                               + _dot(scaled_ref[slot, 0, rs, ks], st_ref[c, hd].astype(BF16))
                               ).astype(BF16)
        for c, hd in groups:
            rs = slice(c * CHUNK, (c + 1) * CHUNK)
            ks = slice(hd * dk, (hd + 1) * dk)
            vs = slice(hd * dv, (hd + 1) * dv)
            dec = jnp.broadcast_to(dec_ref[slot, c, :, ks], (dk, dk)).T
            st_ref[c, hd] = (st_ref[c, hd] * jnp.concatenate([dec] * (dv // dk), axis=1)
                             + _dot_tn(scaled_ref[slot, 4, rs, ks], v_ref[c, :, vs]))

    for slot in (0, 1):
        pl.when(n % 2 == slot)(functools.partial(step, slot))


def _gla(z, zg, wg_hi, wg_lo, bg, layer, *, batch, n_chunks, dk, dv):
    T = z.shape[0]
    key = GLA_HEADS * dk
    val = GLA_HEADS * dv
    bb = 8 if batch % 8 == 0 else (4 if batch % 4 == 0 else 1)
    z3 = z.reshape(batch, n_chunks * CHUNK, z.shape[1])
    zg3 = zg.reshape(batch, n_chunks * CHUNK, zg.shape[1])
    nxt = lambda n: jnp.minimum(n + 1, n_chunks - 1)
    o = pl.pallas_call(
        functools.partial(_gla_kernel, bb=bb, dk=dk, dv=dv),
        grid=(batch // bb, n_chunks),
        in_specs=[pl.BlockSpec((bb, CHUNK, key), lambda b, n: (b, 0, 0)),
                  pl.BlockSpec((bb, CHUNK, key), lambda b, n: (b, 0, 1)),
                  pl.BlockSpec((bb, CHUNK, LANES), lambda b, n: (b, 0, 0)),
                  pl.BlockSpec((bb, CHUNK, key), lambda b, n: (b, nxt(n), 0)),
                  pl.BlockSpec((bb, CHUNK, key), lambda b, n: (b, nxt(n), 1)),
                  pl.BlockSpec((bb, CHUNK, LANES), lambda b, n: (b, nxt(n), 0)),
                  pl.BlockSpec((bb, CHUNK, val), lambda b, n: (b, n, 1)),
                  _layer_block(wg_hi, layer), _layer_block(wg_lo, layer), _layer_block(bg, layer)],
        out_specs=pl.BlockSpec((bb, CHUNK, val), lambda b, n: (b, n, 0)),
        out_shape=jax.ShapeDtypeStruct((batch, n_chunks * CHUNK, val), BF16),
        scratch_shapes=[pltpu.VMEM((bb, GLA_HEADS, dk, dv), F32),
                        pltpu.VMEM((2, 5, bb * CHUNK, key), BF16),
                        pltpu.VMEM((2, bb, 1, key), F32)],
        compiler_params=_cparams(("parallel", "arbitrary")),
        name="gla",
    )(z3, z3, zg3, z3, z3, zg3, z3, wg_hi, wg_lo, bg)
    return o.reshape(T, val)


def _mix_kernel(ogla_ref, og_ref, u_ref, halo_ref, g0_ref, g1_ref, h_ref, gnw_ref,
                wgrp_ref, pscale_ref, wbg_ref, wbp_ref, wout_ref, n2_ref, wr_ref,
                hx_ref, info_ref, cnt_ref,
                ue_ref, p2_ref, p4_ref, p8_ref, p16_ref, tri_ref, carry_ref,
                *, tm, seq_rows, d_model):
    i = pl.program_id(0)
    gd = d_model // len(POOL_WINDOWS)

    @pl.when(i == 0)
    def _():
        r = lax.broadcasted_iota(jnp.int32, (tm, tm), 0)
        c = lax.broadcasted_iota(jnp.int32, (tm, tm), 1)
        tri_ref[...] = jnp.where(r < c, 1.0, 0.0).astype(BF16)
        carry_ref[...] = jnp.zeros_like(carry_ref)

    ue_ref[0:HALO, :] = halo_ref[...].astype(F32)
    ue_ref[HALO:, :] = u_ref[...].astype(F32)
    n8 = tm + HALO - 8
    p2_ref[8:, :] = ue_ref[8:, :] + ue_ref[pl.ds(7, n8), :]
    p4_ref[16:, :] = p2_ref[16:, gd:] + p2_ref[pl.ds(14, n8 - 8), gd:]
    p8_ref[24:, :] = p4_ref[24:, gd:] + p4_ref[pl.ds(20, n8 - 16), gd:]
    p16_ref[HALO:, :] = p8_ref[HALO:, gd:] + p8_ref[pl.ds(24, tm), gd:]
    sum_refs = (p2_ref, p4_ref, p8_ref, p16_ref)

    sub = tm // MIX_SUBTILES
    for part in range(MIX_SUBTILES):
        _mix_rows(part * sub, sub, i * tm, ogla_ref, og_ref, g0_ref, g1_ref, h_ref, gnw_ref, wgrp_ref,
                  pscale_ref, wbg_ref, wbp_ref, wout_ref, n2_ref, wr_ref, hx_ref, info_ref,
                  ue_ref, sum_refs, tri_ref, carry_ref, seq_rows=seq_rows, d_model=d_model)
    cnt_ref[...] = carry_ref[...]


def _mix_rows(r0, rows, tile_row0, ogla_ref, og_ref, g0_ref, g1_ref, h_ref, gnw_ref, wgrp_ref,
              pscale_ref, wbg_ref, wbp_ref, wout_ref, n2_ref, wr_ref, hx_ref, info_ref,
              ue_ref, sum_refs, tri_ref, carry_ref, *, seq_rows, d_model):
    gd = d_model // len(POOL_WINDOWS)
    rs = pl.ds(r0, rows)
    rh = pl.ds(r0 + HALO, rows)

    rowf = (tile_row0 + r0 + lax.broadcasted_iota(jnp.int32, (rows, 1), 0)).astype(F32)
    lp = rowf - jnp.floor(rowf / float(seq_rows)) * float(seq_rows)
    is_pad = lp < float(PAD)
    seqpos1 = lp - float(PAD - 1)

    br_p = None
    for gi, w in enumerate(POOL_WINDOWS):
        cs = slice(gi * gd, (gi + 1) * gd)
        inv_cnt = 1.0 / jnp.clip(seqpos1, 1.0, float(w))
        pooled = sum_refs[gi][rh, 0:gd] * inv_cnt - ue_ref[rh, cs]
        mixed = _dot(pooled.astype(BF16), wgrp_ref[gi]) * pscale_ref[:, cs]
        part = _dot(mixed.astype(BF16), wbp_ref[cs, :])
        br_p = part if br_p is None else br_p + part

    dv = gnw_ref.shape[1]
    br_g = None
    for hd in range(GLA_HEADS):
        vs = slice(hd * dv, (hd + 1) * dv)
        o = ogla_ref[rs, vs].astype(F32)
        o = o * lax.rsqrt(jnp.mean(o * o, axis=-1, keepdims=True) + EPS) * gnw_ref[...]
        og = og_ref[rs, vs]
        ygla = o.astype(BF16) * (og * _sigmoid(og))
        part = _dot(ygla, wbg_ref[vs, :])
        br_g = part if br_g is None else br_g + part
    merged = (_sigmoid(g0_ref[rs, :]) * br_g.astype(BF16) + _sigmoid(g1_ref[rs, :]) * br_p.astype(BF16))
    delta = _dot(merged, wout_ref[...])
    h_new = h_ref[rs, :] + jnp.where(is_pad, 0.0, delta)
    hx_ref[rs, :] = h_new

    ms = jnp.mean(h_new * h_new, axis=-1, keepdims=True)
    xn = (h_new * lax.rsqrt(ms + EPS)) * n2_ref[...]
    x_hi, x_lo = _split(xn)
    w_hi, w_lo = _split(wr_ref[...])
    lt = _dot_nt(w_hi, x_hi) + _dot_nt(w_hi, x_lo) + _dot_nt(w_lo, x_hi)

    lg = [lt[j:j + 1, :] for j in range(N_GROUPS)]
    m = jnp.maximum(jnp.maximum(lg[0], lg[1]), jnp.maximum(lg[2], lg[3]))
    gidx = jnp.where(lg[0] == m, 0, jnp.where(lg[1] == m, 1, jnp.where(lg[2] == m, 2, 3)))
    pg_top = 1.0 / (jnp.exp(lg[0] - m) + jnp.exp(lg[1] - m) + jnp.exp(lg[2] - m) + jnp.exp(lg[3] - m))
    sel = []
    for e in range(EXPERTS_PER_GROUP):
        cand = [lt[N_GROUPS + g * EXPERTS_PER_GROUP + e:N_GROUPS + g * EXPERTS_PER_GROUP + e + 1, :]
                for g in range(N_GROUPS)]
        sel.append(jnp.where(gidx == 0, cand[0], jnp.where(gidx == 1, cand[1],
                                                           jnp.where(gidx == 2, cand[2], cand[3]))))
    m1 = jnp.maximum(jnp.maximum(sel[0], sel[1]), jnp.maximum(sel[2], sel[3]))
    i1 = jnp.where(sel[0] == m1, 0, jnp.where(sel[1] == m1, 1, jnp.where(sel[2] == m1, 2, 3)))
    neg = jnp.float32(-jnp.inf)
    rest = [jnp.where(i1 == e, neg, sel[e]) for e in range(EXPERTS_PER_GROUP)]
    m2 = jnp.maximum(jnp.maximum(rest[0], rest[1]), jnp.maximum(rest[2], rest[3]))
    i2 = jnp.where(rest[0] == m2, 0, jnp.where(rest[1] == m2, 1, jnp.where(rest[2] == m2, 2, 3)))
    r21 = jnp.exp(m2 - m1)
    w1 = pg_top / (1.0 + r21)
    w2 = pg_top * r21 / (1.0 + r21)
    lo = jnp.minimum(i1, i2)
    hi = jnp.maximum(i1, i2)
    w_lo = jnp.where(i1 < i2, w1, w2)
    w_hi = jnp.where(i1 < i2, w2, w1)
    pidx = jnp.where(lo == 0, hi - 1, jnp.where(lo == 1, hi + 1, 5))
    cls = gidx * len(PAIRS) + pidx

    crow = lax.broadcasted_iota(jnp.int32, (CLASS_ROWS, rows), 0)
    onehot = jnp.where(crow == cls, 1.0, 0.0)
    prefix = _dot(onehot.astype(BF16), tri_ref[0:rows, 0:rows]) + carry_ref[:, 0:1]
    rank = jnp.sum(onehot * prefix, axis=0, keepdims=True)
    carry_ref[...] = carry_ref[...] + jnp.sum(onehot, axis=1, keepdims=True)

    irow = lax.broadcasted_iota(jnp.int32, (8, rows), 0)
    info_ref[:, rs] = jnp.where(irow == 0, cls.astype(F32), jnp.where(
        irow == 1, rank, jnp.where(irow == 2, w_lo, jnp.where(irow == 3, w_hi, 0.0))))


def _mix(ogla, z, h, layer_params, layer, *, tm, seq_rows):
    T, D = h.shape
    nblk = tm // HALO
    return pl.pallas_call(
        functools.partial(_mix_kernel, tm=tm, seq_rows=seq_rows, d_model=D),
        grid=(T // tm,),
        in_specs=[pl.BlockSpec((tm, D), lambda i: (i, 0)),
                  pl.BlockSpec((tm, D), lambda i: (i, 2)),
                  pl.BlockSpec((tm, D), lambda i: (i, 3)),
                  pl.BlockSpec((HALO, D), lambda i: (jnp.maximum(i * nblk - 1, 0), 3)),
                  pl.BlockSpec((tm, D), lambda i: (i, 4)),
                  pl.BlockSpec((tm, D), lambda i: (i, 5)),
                  pl.BlockSpec((tm, D), lambda i: (i, 0))]
                 + [_layer_block(p, layer) for p in layer_params],
        out_specs=[pl.BlockSpec((tm, D), lambda i: (i, 0)),
                   pl.BlockSpec((8, tm), lambda i: (0, i)),
                   pl.BlockSpec((CLASS_ROWS, LANES), lambda i: (0, 0))],
        out_shape=[jax.ShapeDtypeStruct((T, D), F32),
                   jax.ShapeDtypeStruct((8, T), F32),
                   jax.ShapeDtypeStruct((CLASS_ROWS, LANES), F32)],
        scratch_shapes=[pltpu.VMEM((tm + HALO, D), F32),
                        pltpu.VMEM((tm + HALO, D), F32),
                        pltpu.VMEM((tm + HALO, D - D // 4), F32),
                        pltpu.VMEM((tm + HALO, D - 2 * (D // 4)), F32),
                        pltpu.VMEM((tm + HALO, D // 4), F32),
                        pltpu.VMEM((tm, tm), BF16),
                        pltpu.VMEM((CLASS_ROWS, LANES), F32)],
        compiler_params=_cparams(("arbitrary",)),
        name="mix",
    )(ogla, z, z, z, z, z, h, *layer_params)


def _dispatch_kernel(tend_ref, nu_ref, pos_ref, h_ref, info_ref, xs_ref, row_ref, zero_ref, sem, zsem,
                     *, tm, tm_exp, n_tiles, d_model):
    def fill(j):
        return pltpu.make_async_copy(zero_ref, xs_ref.at[pl.ds(j * tm_exp, tm_exp)], zsem)

    @pl.when(pl.program_id(0) == 0)
    def _():
        zero_ref[...] = jnp.zeros_like(zero_ref)
        for wait in (False, True):
            for c in range(N_CLASSES):
                prev = tend_ref[c - 1] if c else 0

                @pl.when(tend_ref[c] > prev)
                def _():
                    cp = fill(tend_ref[c] - 1)
                    cp.wait() if wait else cp.start()

            def tail(j, carry):
                cp = fill(j)
                cp.wait() if wait else cp.start()
                return carry
            lax.fori_loop(nu_ref[0], n_tiles, tail, 0)

    row_ref[:, 0:d_model] = h_ref[...]
    lane = lax.broadcasted_iota(jnp.int32, (ROUTE_LANES, tm), 0)
    row_ref[:, d_model:] = jnp.where(lane == 0, info_ref[2:3, :],
                                     jnp.where(lane == 1, info_ref[3:4, :], 0.0)).T
    for r in range(tm):
        pltpu.make_async_copy(row_ref.at[pl.ds(r, 1)], xs_ref.at[pl.ds(pos_ref[r], 1)],
                              sem).start(priority=r % DMA_THREADS)
    pltpu.make_async_copy(row_ref, xs_ref.at[pl.ds(0, tm)], sem).wait()


def _dispatch(h, info, pos, tile_end, n_used, *, tm, tm_exp, n_sorted):
    T, D = h.shape
    W = D + ROUTE_LANES
    grid_spec = pltpu.PrefetchScalarGridSpec(
        num_scalar_prefetch=2,
        grid=(T // tm,),
        in_specs=[pl.BlockSpec((tm,), lambda i, te, nu: (i,), memory_space=pltpu.SMEM),
                  pl.BlockSpec((tm, D), lambda i, te, nu: (i, 0)),
                  pl.BlockSpec((8, tm), lambda i, te, nu: (0, i))],
        out_specs=pl.BlockSpec(memory_space=pl.ANY),
        scratch_shapes=[pltpu.VMEM((tm, W), F32), pltpu.VMEM((tm_exp, W), F32),
                        pltpu.SemaphoreType.DMA(()), pltpu.SemaphoreType.DMA(())],
    )
    return pl.pallas_call(
        functools.partial(_dispatch_kernel, tm=tm, tm_exp=tm_exp, n_tiles=n_sorted // tm_exp,
                          d_model=D),
        grid_spec=grid_spec,
        out_shape=jax.ShapeDtypeStruct((n_sorted, W), F32),
        compiler_params=_cparams(("arbitrary",)),
        name="dispatch",
    )(tile_end, n_used, pos, h, info)


def _collect_norm_kernel(pos_ref, ys_ref, nw_ref, o_ref, buf_ref, sem, *, tm):
    _row_gather(ys_ref, pos_ref, buf_ref, sem, tm)
    _row_gather_wait(ys_ref, buf_ref, sem, tm)
    x = buf_ref[...]
    ms = jnp.mean(x * x, axis=-1, keepdims=True)
    o_ref[...] = (x * lax.rsqrt(ms + EPS)) * nw_ref[...]


def _collect_norm(ys, pos, norm_w, *, tm):
    n_out = pos.shape[0]
    D = ys.shape[1]
    return pl.pallas_call(
        functools.partial(_collect_norm_kernel, tm=tm),
        grid=(n_out // tm,),
        in_specs=[pl.BlockSpec((tm,), lambda i: (i,), memory_space=pltpu.SMEM),
                  pl.BlockSpec(memory_space=pl.ANY),
                  _resident(norm_w.shape)],
        out_specs=pl.BlockSpec((tm, D), lambda i: (i, 0)),
        out_shape=jax.ShapeDtypeStruct((n_out, D), F32),
        scratch_shapes=[pltpu.VMEM((tm, D), F32), pltpu.SemaphoreType.DMA(())],
        compiler_params=_cparams(("arbitrary",)),
        name="collect_norm",
    )(pos, ys, norm_w)


def _moe_kernel(ea_ref, eb_ref, nu_ref, xs_ref, n2_ref, wga_ref, wua_ref, wda_ref,
                wgb_ref, wub_ref, wdb_ref, ys_ref, *, d_model):
    i = pl.program_id(0)

    @pl.when(i < nu_ref[0])
    def _():
        x = xs_ref[:, 0:d_model]
        ms = jnp.mean(x * x, axis=-1, keepdims=True)
        xn = ((x * lax.rsqrt(ms + EPS)) * n2_ref[...]).astype(BF16)
        y = x
        for col, (wg, wu, wd) in enumerate(((wga_ref, wua_ref, wda_ref), (wgb_ref, wub_ref, wdb_ref))):
            gate = _dot(xn, wg[...])
            up = _dot(xn, wu[...])
            hid = (gate * _sigmoid(gate) * up).astype(BF16)
            y = y + xs_ref[:, d_model + col:d_model + col + 1] * _dot(hid, wd[...])
        ys_ref[...] = y

    @pl.when(i >= nu_ref[0])
    def _():
        ys_ref[...] = jnp.zeros_like(ys_ref)


def _moe(xs, tile_ea, tile_eb, n_used, n2, weg, weu, wed, layer, *, tm):
    n_sorted, W = xs.shape
    D = n2.shape[2]
    de = weg.shape[3]
    row = lambda i, ea, eb, nu: (jnp.minimum(i, nu[0] - 1), 0)
    wa = lambda i, ea, eb, nu: (layer, ea[i], 0, 0)
    wb = lambda i, ea, eb, nu: (layer, eb[i], 0, 0)
    grid_spec = pltpu.PrefetchScalarGridSpec(
        num_scalar_prefetch=3,
        grid=(n_sorted // tm,),
        in_specs=[pl.BlockSpec((tm, W), row),
                  pl.BlockSpec((None, 1, D), lambda i, ea, eb, nu: (layer, 0, 0)),
                  pl.BlockSpec((None, None, D, de), wa), pl.BlockSpec((None, None, D, de), wa),
                  pl.BlockSpec((None, None, de, D), wa),
                  pl.BlockSpec((None, None, D, de), wb), pl.BlockSpec((None, None, D, de), wb),
                  pl.BlockSpec((None, None, de, D), wb)],
        out_specs=pl.BlockSpec((tm, D), lambda i, ea, eb, nu: (i, 0)),
    )
    return pl.pallas_call(
        functools.partial(_moe_kernel, d_model=D),
        grid_spec=grid_spec,
        out_shape=jax.ShapeDtypeStruct((n_sorted, D), F32),
        compiler_params=_cparams(("arbitrary",)),
        name="experts",
    )(tile_ea, tile_eb, n_used, xs, n2, weg, weu, wed, weg, weu, wed)


def _row_tile(T, cap):
    t = cap
    while T % t:
        t //= 2
    return t


def kernel(x, meta_tokens, norm1_w, w_in, w_gate_up, b_gate, gla_norm_w, w_pool_grp, pool_scale,
           w_br_gla, w_br_pool, w_out, norm2_w, w_router_group, w_router_expert, w_exp_gate,
           w_exp_up, w_exp_down, final_norm_w):
    B, S, D = x.shape
    depth = w_in.shape[0]
    key = w_gate_up.shape[2]
    rank = w_gate_up.shape[1]
    dv = gla_norm_w.shape[1]
    val = GLA_HEADS * dv
    dk = key // GLA_HEADS
    assert S % CHUNK == 0 and key * 2 == D and val == D
    assert w_pool_grp.shape[2] * len(POOL_WINDOWS) == D
    LP = PAD + N_META + S
    n_chunks = LP // CHUNK
    T = B * LP
    tm_proj = tm_mix = tm_row = _row_tile(T, ROW_TILE)
    tm_exp = EXPERT_TILE
    n_sorted = -(-(T + N_CLASSES * (tm_exp - 1)) // tm_exp) * tm_exp
    n_tiles = n_sorted // tm_exp

    head = jnp.concatenate([jnp.zeros((PAD, D), F32), meta_tokens.astype(F32)], axis=0)
    lp = np.arange(T) % LP
    ys = x.astype(F32).reshape(B * S, D)
    pos = jnp.asarray(np.where(lp >= CHUNK, (np.arange(T) // LP) * S + lp - CHUNK, 0), jnp.int32)

    c_q = 0
    c_gl = 2 * key + 2 * val
    c_u = c_gl + rank

    w_a = w_in[:, :, c_q:c_gl].astype(BF16)
    w_b = w_in[:, :, c_u:].astype(BF16)
    w_g = jnp.concatenate([w_in[:, :, c_gl:c_u].astype(BF16),
                           jnp.zeros((depth, D, LANES - rank), BF16)], axis=2)
    wg_hi, wg_lo = _split(jnp.concatenate(
        [w_gate_up, jnp.zeros((depth, LANES - rank, key), F32)], axis=1))
    wr = jnp.swapaxes(jnp.concatenate(
        [w_router_group, w_router_expert,
         jnp.zeros((depth, D, CLASS_ROWS - N_GROUPS - N_EXPERTS), F32)], axis=2), 1, 2)
    row_vec = lambda p: p[:, None, :]
    mix_params = (row_vec(gla_norm_w), w_pool_grp.astype(BF16), row_vec(pool_scale),
                  w_br_gla.astype(BF16), w_br_pool.astype(BF16), w_out.astype(BF16),
                  row_vec(norm2_w), wr)
    weg, weu, wed = w_exp_gate.astype(BF16), w_exp_up.astype(BF16), w_exp_down.astype(BF16)

    for l in range(depth):
        z, zg, h = _inproj_gather(ys, pos, head, row_vec(norm1_w), w_a, w_b, w_g, l, tm=tm_proj,
                                  seq_rows=LP if l == 0 else 0, n_batch=B)
        ogla = _gla(z, zg, wg_hi, wg_lo, row_vec(b_gate), l, batch=B, n_chunks=n_chunks, dk=dk, dv=dv)
        hx, info, counts = _mix(ogla, z, h, mix_params, l, tm=tm_mix, seq_rows=LP)

        cnt = counts[:N_CLASSES, 0].astype(jnp.int32)
        seg_tiles = (cnt + tm_exp - 1) // tm_exp
        tile_end = jnp.cumsum(seg_tiles)
        start = (tile_end - seg_tiles) * tm_exp
        n_used = tile_end[-1:]
        tile_cls = jnp.minimum(
            jnp.sum(jnp.arange(n_tiles)[:, None] >= tile_end[None, :], axis=1), N_CLASSES - 1)
        tile_cls = jnp.where(jnp.arange(n_tiles) < n_used[0], tile_cls,
                             jnp.take(tile_cls, jnp.maximum(n_used[0] - 1, 0)))
        pair_lo = jnp.array([p[0] for p in PAIRS], jnp.int32)
        pair_hi = jnp.array([p[1] for p in PAIRS], jnp.int32)
        grp = tile_cls // len(PAIRS)
        tile_ea = (grp * EXPERTS_PER_GROUP + jnp.take(pair_lo, tile_cls % len(PAIRS))).astype(jnp.int32)
        tile_eb = (grp * EXPERTS_PER_GROUP + jnp.take(pair_hi, tile_cls % len(PAIRS))).astype(jnp.int32)
        cls_t = info[0].astype(jnp.int32)
        pos = info[1].astype(jnp.int32) + jnp.sum(
            jnp.where(cls_t[:, None] == jnp.arange(N_CLASSES)[None, :], start[None, :], 0), axis=1)

        xs = _dispatch(hx, info, pos, tile_end.astype(jnp.int32), n_used.astype(jnp.int32),
                       tm=tm_row, tm_exp=tm_exp, n_sorted=n_sorted)
        ys = _moe(xs, tile_ea, tile_eb, n_used.astype(jnp.int32), row_vec(norm2_w), weg, weu, wed, l,
                  tm=tm_exp)

    pos_out = pos.reshape(B, LP)[:, PAD + N_META:].reshape(B * S)
    out = _collect_norm(ys, pos_out, final_norm_w[None], tm=_row_tile(B * S, ROW_TILE))
    return out.reshape(B, S, D)
```

```python
import functools

import jax
import jax.numpy as jnp
import numpy as np
from jax import lax
from jax.experimental import pallas as pl
from jax.experimental.pallas import tpu as pltpu

F32 = jnp.float32
BF16 = jnp.bfloat16

EPS = 1e-6
CHUNK = 64
N_META = 16
PAD = CHUNK - N_META
GLA_HEADS = 4
GATE_NORMALIZER = 16.0
POOL_WINDOWS = (2, 4, 8, 16)
N_GROUPS = 4
EXPERTS_PER_GROUP = 4
N_EXPERTS = N_GROUPS * EXPERTS_PER_GROUP
PAIRS = ((0, 1), (0, 2), (0, 3), (1, 2), (1, 3), (2, 3))
N_CLASSES = N_GROUPS * len(PAIRS)
CLASS_ROWS = 32
LANES = 128
HALO = 32
ROUTE_LANES = 128
DMA_THREADS = 2
MIX_SUBTILES = 1

VMEM_LIMIT = 56 * 1024 * 1024
ROW_TILE = 512
EXPERT_TILE = 256
PROJ_COL_BLOCK = 512


def _cparams(sem):
    return pltpu.CompilerParams(dimension_semantics=sem, vmem_limit_bytes=VMEM_LIMIT)


def _resident(shape):
    nd = len(shape)
    return pl.BlockSpec(shape, lambda *_: (0,) * nd, pipeline_mode=pl.Buffered(1))


def _layer_block(arr, layer):
    nd = arr.ndim
    return pl.BlockSpec((None,) + arr.shape[1:], lambda *_: (layer,) + (0,) * (nd - 1),
                        pipeline_mode=pl.Buffered(1))


def _split(x):
    hi = x.astype(BF16)
    lo = (x - hi.astype(F32)).astype(BF16)
    return hi, lo


def _dot(a, b):
    return jnp.dot(a, b, preferred_element_type=F32)


def _dot_nt(a, b):
    return lax.dot_general(a, b, (((1,), (1,)), ((), ())), preferred_element_type=F32)


def _dot_tn(a, b):
    return lax.dot_general(a, b, (((0,), (0,)), ((), ())), preferred_element_type=F32)


def _sigmoid(x):
    return 0.5 * jnp.tanh(0.5 * x) + 0.5


def _row_gather(src_ref, pos_ref, dst_ref, sem, tm):
    for r in range(tm):
        pltpu.make_async_copy(src_ref.at[pl.ds(pos_ref[r], 1)], dst_ref.at[pl.ds(r, 1)],
                              sem).start(priority=r % DMA_THREADS)


def _row_gather_wait(src_ref, dst_ref, sem, tm):
    pltpu.make_async_copy(src_ref.at[pl.ds(0, tm)], dst_ref, sem).wait()


def _inproj_gather_kernel(pos0_ref, posn_ref, ys_ref, head_ref, nw_ref, wa_ref, wb_ref, wg_ref,
                          z_ref, zg_ref, h_ref, buf_ref, xn_ref, sem,
                          *, tm, col_blk, seq_rows, n_batch):
    i = pl.program_id(0)
    last = pl.num_programs(0) - 1
    slot = i % 2

    @pl.when(i == 0)
    def _():
        _row_gather(ys_ref, pos0_ref, buf_ref.at[0], sem.at[0], tm)

    _row_gather_wait(ys_ref, buf_ref.at[slot], sem.at[slot], tm)
    if seq_rows:
        b0 = (i * tm + seq_rows - 1) // seq_rows
        for k in range(tm // seq_rows + 1):
            off = (b0 + k) * seq_rows - i * tm

            @pl.when(jnp.logical_and(off < tm, b0 + k < n_batch))
            def _():
                buf_ref[slot, pl.ds(pl.multiple_of(off, CHUNK), CHUNK), :] = head_ref[...]

    x = buf_ref[slot]
    h_ref[...] = x
    ms = jnp.mean(x * x, axis=-1, keepdims=True)
    xn_ref[...] = ((x * lax.rsqrt(ms + EPS)) * nw_ref[...]).astype(BF16)
    _row_gather(ys_ref, posn_ref, buf_ref.at[1 - slot], sem.at[1 - slot], tm)
    col = 0
    for w_ref in (wa_ref, wb_ref):
        for j in range(w_ref.shape[1] // col_blk):
            sl = slice(j * col_blk, (j + 1) * col_blk)
            z_ref[:, col:col + col_blk] = _dot(xn_ref[...], w_ref[:, sl]).astype(BF16)
            col += col_blk
    zg_ref[...] = _dot(xn_ref[...], wg_ref[...])

    @pl.when(i == last)
    def _():
        _row_gather_wait(ys_ref, buf_ref.at[1 - slot], sem.at[1 - slot], tm)


def _inproj_gather(ys, pos, head, norm_w, w_a, w_b, w_g, layer, *, tm, seq_rows=0, n_batch=0):
    T = pos.shape[0]
    D = ys.shape[1]
    n_main = w_a.shape[2] + w_b.shape[2]
    n_all = n_main + w_g.shape[2]
    n_steps = T // tm
    assert tm % CHUNK == 0 and seq_rows % CHUNK == 0
    return pl.pallas_call(
        functools.partial(_inproj_gather_kernel, tm=tm, col_blk=PROJ_COL_BLOCK,
                          seq_rows=seq_rows, n_batch=n_batch),
        grid=(n_steps,),
        in_specs=[pl.BlockSpec((tm,), lambda i: (0,), memory_space=pltpu.SMEM),
                  pl.BlockSpec((tm,), lambda i: (jnp.minimum(i + 1, n_steps - 1),),
                               memory_space=pltpu.SMEM),
                  pl.BlockSpec(memory_space=pl.ANY),
                  _resident(head.shape),
                  _layer_block(norm_w, layer), _layer_block(w_a, layer),
                  _layer_block(w_b, layer), _layer_block(w_g, layer)],
        out_specs=[pl.BlockSpec((tm, n_main), lambda i: (i, 0)),
                   pl.BlockSpec((tm, n_all - n_main), lambda i: (i, 0)),
                   pl.BlockSpec((tm, D), lambda i: (i, 0))],
        out_shape=[jax.ShapeDtypeStruct((T, n_main), BF16),
                   jax.ShapeDtypeStruct((T, n_all - n_main), F32),
                   jax.ShapeDtypeStruct((T, D), F32)],
        scratch_shapes=[pltpu.VMEM((2, tm, D), F32), pltpu.VMEM((tm, D), BF16),
                        pltpu.SemaphoreType.DMA((2,))],
        compiler_params=_cparams(("arbitrary",)),
        name="inproj_gather",
    )(pos, pos, ys, head, norm_w, w_a, w_b, w_g)


def _gla_prep(q_ref, k_ref, zg_ref, wgh_ref, wgl_ref, bg_ref, scaled_ref, dec_ref, slot, first,
              *, bb, dk):
    rows = bb * CHUNK
    key = GLA_HEADS * dk
    gl_hi, gl_lo = _split(zg_ref[...].reshape(rows, LANES))
    gpre = (_dot(gl_hi, wgh_ref[...]) + _dot(gl_hi, wgl_ref[...]) + _dot(gl_lo, wgh_ref[...])
            + bg_ref[...])
    g = (jnp.minimum(gpre, 0.0) - jnp.log(1.0 + jnp.exp(-jnp.abs(gpre)))) * (1.0 / GATE_NORMALIZER)
    if first:
        row = lax.broadcasted_iota(jnp.int32, (rows, 1), 0)
        g = jnp.where((row & (CHUNK - 1)) < PAD, 0.0, g)

    ri = lax.broadcasted_iota(jnp.int32, (rows, rows), 0)
    ci = lax.broadcasted_iota(jnp.int32, (rows, rows), 1)
    tri = jnp.where(jnp.logical_and((ri // CHUNK) == (ci // CHUNK), ri >= ci), 1.0, 0.0).astype(BF16)
    g_hi, g_lo = _split(g)
    bcum = _dot(tri, g_hi) + _dot(tri, g_lo)
    gam = jnp.concatenate(
        [jnp.broadcast_to(bcum[(c + 1) * CHUNK - 1:(c + 1) * CHUNK, :], (CHUNK, key)) for c in range(bb)],
        axis=0)
    eb = jnp.exp(bcum)
    ieb = jnp.exp(-bcum)
    q = q_ref[...].reshape(rows, key).astype(F32) * (dk ** -0.5)
    k = k_ref[...].reshape(rows, key).astype(F32)
    scaled_ref[slot, 0] = (q * eb).astype(BF16)
    scaled_ref[slot, 1] = (q * ieb).astype(BF16)
    scaled_ref[slot, 2] = (k * eb).astype(BF16)
    scaled_ref[slot, 3] = (k * ieb).astype(BF16)
    scaled_ref[slot, 4] = (k * jnp.exp(gam - bcum)).astype(BF16)
    for c in range(bb):
        dec_ref[slot, c] = eb[(c + 1) * CHUNK - 1:(c + 1) * CHUNK, :]


def _gla_kernel(q0_ref, k0_ref, zg0_ref, qn_ref, kn_ref, zgn_ref, v_ref, wgh_ref, wgl_ref, bg_ref,
                o_ref, st_ref, scaled_ref, dec_ref, *, bb, dk, dv):
    n = pl.program_id(1)
    prep = functools.partial(_gla_prep, wgh_ref=wgh_ref, wgl_ref=wgl_ref, bg_ref=bg_ref,
                             scaled_ref=scaled_ref, dec_ref=dec_ref, bb=bb, dk=dk)

    @pl.when(n == 0)
    def _():
        st_ref[...] = jnp.zeros_like(st_ref)
        prep(q0_ref, k0_ref, zg0_ref, slot=0, first=True)

    li = lax.broadcasted_iota(jnp.int32, (CHUNK, CHUNK), 0)
    lj = lax.broadcasted_iota(jnp.int32, (CHUNK, CHUNK), 1)
    lower = li >= lj
    groups = [(c, hd) for c in range(bb) for hd in range(GLA_HEADS)]

    def step(slot):
        prep(qn_ref, kn_ref, zgn_ref, slot=1 - slot, first=False)
        attn = {}
        for c, hd in groups:
            rs = slice(c * CHUNK, (c + 1) * CHUNK)
            ks = slice(hd * dk, (hd + 1) * dk)
            a_lo = _dot_nt(scaled_ref[slot, 0, rs, ks], scaled_ref[slot, 3, rs, ks])
            a_up = _dot_nt(scaled_ref[slot, 1, rs, ks], scaled_ref[slot, 2, rs, ks])
            attn[c, hd] = jnp.where(lower, a_lo, a_up).astype(BF16)
        for c, hd in groups:
            rs = slice(c * CHUNK, (c + 1) * CHUNK)
            ks = slice(hd * dk, (hd + 1) * dk)
            vs = slice(hd * dv, (hd + 1) * dv)
            o_ref[c, :, vs] = (_dot(attn[c, hd], v_ref[c, :, vs])
                               + _dot(scaled_ref[slot, 0, rs, ks], st_ref[c, hd].astype(BF16))
                               ).astype(BF16)
        for c, hd in groups:
            rs = slice(c * CHUNK, (c + 1) * CHUNK)
            ks = slice(hd * dk, (hd + 1) * dk)
            vs = slice(hd * dv, (hd + 1) * dv)
            dec = jnp.broadcast_to(dec_ref[slot, c, :, ks], (dk, dk)).T
            st_ref[c, hd] = (st_ref[c, hd] * jnp.concatenate([dec] * (dv // dk), axis=1)
                             + _dot_tn(scaled_ref[slot, 4, rs, ks], v_ref[c, :, vs]))

    for slot in (0, 1):
        pl.when(n % 2 == slot)(functools.partial(step, slot))


def _gla(z, zg, wg_hi, wg_lo, bg, layer, *, batch, n_chunks, dk, dv):
    T = z.shape[0]
    key = GLA_HEADS * dk
    val = GLA_HEADS * dv
    bb = 8 if batch % 8 == 0 else (4 if batch % 4 == 0 else 1)
    z3 = z.reshape(batch, n_chunks * CHUNK, z.shape[1])
    zg3 = zg.reshape(batch, n_chunks * CHUNK, zg.shape[1])
    nxt = lambda n: jnp.minimum(n + 1, n_chunks - 1)
    o = pl.pallas_call(
        functools.partial(_gla_kernel, bb=bb, dk=dk, dv=dv),
        grid=(batch // bb, n_chunks),
        in_specs=[pl.BlockSpec((bb, CHUNK, key), lambda b, n: (b, 0, 0)),
                  pl.BlockSpec((bb, CHUNK, key), lambda b, n: (b, 0, 1)),
                  pl.BlockSpec((bb, CHUNK, LANES), lambda b, n: (b, 0, 0)),
                  pl.BlockSpec((bb, CHUNK, key), lambda b, n: (b, nxt(n), 0)),
                  pl.BlockSpec((bb, CHUNK, key), lambda b, n: (b, nxt(n), 1)),
                  pl.BlockSpec((bb, CHUNK, LANES), lambda b, n: (b, nxt(n), 0)),
                  pl.BlockSpec((bb, CHUNK, val), lambda b, n: (b, n, 1)),
                  _layer_block(wg_hi, layer), _layer_block(wg_lo, layer), _layer_block(bg, layer)],
        out_specs=pl.BlockSpec((bb, CHUNK, val), lambda b, n: (b, n, 0)),
        out_shape=jax.ShapeDtypeStruct((batch, n_chunks * CHUNK, val), BF16),
        scratch_shapes=[pltpu.VMEM((bb, GLA_HEADS, dk, dv), F32),
                        pltpu.VMEM((2, 5, bb * CHUNK, key), BF16),
                        pltpu.VMEM((2, bb, 1, key), F32)],
        compiler_params=_cparams(("parallel", "arbitrary")),
        name="gla",
    )(z3, z3, zg3, z3, z3, zg3, z3, wg_hi, wg_lo, bg)
    return o.reshape(T, val)


def _mix_kernel(ogla_ref, og_ref, u_ref, halo_ref, g0_ref, g1_ref, h_ref, gnw_ref,
                wgrp_ref, pscale_ref, wbg_ref, wbp_ref, wout_ref, n2_ref, wr_ref,
                hx_ref, info_ref, cnt_ref,
                ue_ref, p2_ref, p4_ref, p8_ref, p16_ref, tri_ref, carry_ref,
                *, tm, seq_rows, d_model):
    i = pl.program_id(0)
    gd = d_model // len(POOL_WINDOWS)

    @pl.when(i == 0)
    def _():
        r = lax.broadcasted_iota(jnp.int32, (tm, tm), 0)
        c = lax.broadcasted_iota(jnp.int32, (tm, tm), 1)
        tri_ref[...] = jnp.where(r < c, 1.0, 0.0).astype(BF16)
        carry_ref[...] = jnp.zeros_like(carry_ref)

    ue_ref[0:HALO, :] = halo_ref[...].astype(F32)
    ue_ref[HALO:, :] = u_ref[...].astype(F32)
    n8 = tm + HALO - 8
    p2_ref[8:, :] = ue_ref[8:, :] + ue_ref[pl.ds(7, n8), :]
    p4_ref[16:, :] = p2_ref[16:, gd:] + p2_ref[pl.ds(14, n8 - 8), gd:]
    p8_ref[24:, :] = p4_ref[24:, gd:] + p4_ref[pl.ds(20, n8 - 16), gd:]
    p16_ref[HALO:, :] = p8_ref[HALO:, gd:] + p8_ref[pl.ds(24, tm), gd:]
    sum_refs = (p2_ref, p4_ref, p8_ref, p16_ref)

    sub = tm // MIX_SUBTILES
    for part in range(MIX_SUBTILES):
        _mix_rows(part * sub, sub, i * tm, ogla_ref, og_ref, g0_ref, g1_ref, h_ref, gnw_ref, wgrp_ref,
                  pscale_ref, wbg_ref, wbp_ref, wout_ref, n2_ref, wr_ref, hx_ref, info_ref,
                  ue_ref, sum_refs, tri_ref, carry_ref, seq_rows=seq_rows, d_model=d_model)
    cnt_ref[...] = carry_ref[...]


def _mix_rows(r0, rows, tile_row0, ogla_ref, og_ref, g0_ref, g1_ref, h_ref, gnw_ref, wgrp_ref,
              pscale_ref, wbg_ref, wbp_ref, wout_ref, n2_ref, wr_ref, hx_ref, info_ref,
              ue_ref, sum_refs, tri_ref, carry_ref, *, seq_rows, d_model):
    gd = d_model // len(POOL_WINDOWS)
    rs = pl.ds(r0, rows)
    rh = pl.ds(r0 + HALO, rows)

    rowf = (tile_row0 + r0 + lax.broadcasted_iota(jnp.int32, (rows, 1), 0)).astype(F32)
    lp = rowf - jnp.floor(rowf / float(seq_rows)) * float(seq_rows)
    is_pad = lp < float(PAD)
    seqpos1 = lp - float(PAD - 1)

    br_p = None
    for gi, w in enumerate(POOL_WINDOWS):
        cs = slice(gi * gd, (gi + 1) * gd)
        inv_cnt = 1.0 / jnp.clip(seqpos1, 1.0, float(w))
        pooled = sum_refs[gi][rh, 0:gd] * inv_cnt - ue_ref[rh, cs]
        mixed = _dot(pooled.astype(BF16), wgrp_ref[gi]) * pscale_ref[:, cs]
        part = _dot(mixed.astype(BF16), wbp_ref[cs, :])
        br_p = part if br_p is None else br_p + part

    dv = gnw_ref.shape[1]
    br_g = None
    for hd in range(GLA_HEADS):
        vs = slice(hd * dv, (hd + 1) * dv)
        o = ogla_ref[rs, vs].astype(F32)
        o = o * lax.rsqrt(jnp.mean(o * o, axis=-1, keepdims=True) + EPS) * gnw_ref[...]
        og = og_ref[rs, vs]
        ygla = o.astype(BF16) * (og * _sigmoid(og))
        part = _dot(ygla, wbg_ref[vs, :])
        br_g = part if br_g is None else br_g + part
    merged = (_sigmoid(g0_ref[rs, :]) * br_g.astype(BF16) + _sigmoid(g1_ref[rs, :]) * br_p.astype(BF16))
    delta = _dot(merged, wout_ref[...])
    h_new = h_ref[rs, :] + jnp.where(is_pad, 0.0, delta)
    hx_ref[rs, :] = h_new

    ms = jnp.mean(h_new * h_new, axis=-1, keepdims=True)
    xn = (h_new * lax.rsqrt(ms + EPS)) * n2_ref[...]
    x_hi, x_lo = _split(xn)
    w_hi, w_lo = _split(wr_ref[...])
    lt = _dot_nt(w_hi, x_hi) + _dot_nt(w_hi, x_lo) + _dot_nt(w_lo, x_hi)

    lg = [lt[j:j + 1, :] for j in range(N_GROUPS)]
    m = jnp.maximum(jnp.maximum(lg[0], lg[1]), jnp.maximum(lg[2], lg[3]))
    gidx = jnp.where(lg[0] == m, 0, jnp.where(lg[1] == m, 1, jnp.where(lg[2] == m, 2, 3)))
    pg_top = 1.0 / (jnp.exp(lg[0] - m) + jnp.exp(lg[1] - m) + jnp.exp(lg[2] - m) + jnp.exp(lg[3] - m))
    sel = []
    for e in range(EXPERTS_PER_GROUP):
        cand = [lt[N_GROUPS + g * EXPERTS_PER_GROUP + e:N_GROUPS + g * EXPERTS_PER_GROUP + e + 1, :]
                for g in range(N_GROUPS)]
        sel.append(jnp.where(gidx == 0, cand[0], jnp.where(gidx == 1, cand[1],
                                                           jnp.where(gidx == 2, cand[2], cand[3]))))
    m1 = jnp.maximum(jnp.maximum(sel[0], sel[1]), jnp.maximum(sel[2], sel[3]))
    i1 = jnp.where(sel[0] == m1, 0, jnp.where(sel[1] == m1, 1, jnp.where(sel[2] == m1, 2, 3)))
    neg = jnp.float32(-jnp.inf)
    rest = [jnp.where(i1 == e, neg, sel[e]) for e in range(EXPERTS_PER_GROUP)]
    m2 = jnp.maximum(jnp.maximum(rest[0], rest[1]), jnp.maximum(rest[2], rest[3]))
    i2 = jnp.where(rest[0] == m2, 0, jnp.where(rest[1] == m2, 1, jnp.where(rest[2] == m2, 2, 3)))
    r21 = jnp.exp(m2 - m1)
    w1 = pg_top / (1.0 + r21)
    w2 = pg_top * r21 / (1.0 + r21)
    lo = jnp.minimum(i1, i2)
    hi = jnp.maximum(i1, i2)
    w_lo = jnp.where(i1 < i2, w1, w2)
    w_hi = jnp.where(i1 < i2, w2, w1)
    pidx = jnp.where(lo == 0, hi - 1, jnp.where(lo == 1, hi + 1, 5))
    cls = gidx * len(PAIRS) + pidx

    crow = lax.broadcasted_iota(jnp.int32, (CLASS_ROWS, rows), 0)
    onehot = jnp.where(crow == cls, 1.0, 0.0)
    prefix = _dot(onehot.astype(BF16), tri_ref[0:rows, 0:rows]) + carry_ref[:, 0:1]
    rank = jnp.sum(onehot * prefix, axis=0, keepdims=True)
    carry_ref[...] = carry_ref[...] + jnp.sum(onehot, axis=1, keepdims=True)

    irow = lax.broadcasted_iota(jnp.int32, (8, rows), 0)
    info_ref[:, rs] = jnp.where(irow == 0, cls.astype(F32), jnp.where(
        irow == 1, rank, jnp.where(irow == 2, w_lo, jnp.where(irow == 3, w_hi, 0.0))))


def _mix(ogla, z, h, layer_params, layer, *, tm, seq_rows):
    T, D = h.shape
    nblk = tm // HALO
    return pl.pallas_call(
        functools.partial(_mix_kernel, tm=tm, seq_rows=seq_rows, d_model=D),
        grid=(T // tm,),
        in_specs=[pl.BlockSpec((tm, D), lambda i: (i, 0)),
                  pl.BlockSpec((tm, D), lambda i: (i, 2)),
                  pl.BlockSpec((tm, D), lambda i: (i, 3)),
                  pl.BlockSpec((HALO, D), lambda i: (jnp.maximum(i * nblk - 1, 0), 3)),
                  pl.BlockSpec((tm, D), lambda i: (i, 4)),
                  pl.BlockSpec((tm, D), lambda i: (i, 5)),
                  pl.BlockSpec((tm, D), lambda i: (i, 0))]
                 + [_layer_block(p, layer) for p in layer_params],
        out_specs=[pl.BlockSpec((tm, D), lambda i: (i, 0)),
                   pl.BlockSpec((8, tm), lambda i: (0, i)),
                   pl.BlockSpec((CLASS_ROWS, LANES), lambda i: (0, 0))],
        out_shape=[jax.ShapeDtypeStruct((T, D), F32),
                   jax.ShapeDtypeStruct((8, T), F32),
                   jax.ShapeDtypeStruct((CLASS_ROWS, LANES), F32)],
        scratch_shapes=[pltpu.VMEM((tm + HALO, D), F32),
                        pltpu.VMEM((tm + HALO, D), F32),
                        pltpu.VMEM((tm + HALO, D - D // 4), F32),
                        pltpu.VMEM((tm + HALO, D - 2 * (D // 4)), F32),
                        pltpu.VMEM((tm + HALO, D // 4), F32),
                        pltpu.VMEM((tm, tm), BF16),
                        pltpu.VMEM((CLASS_ROWS, LANES), F32)],
        compiler_params=_cparams(("arbitrary",)),
        name="mix",
    )(ogla, z, z, z, z, z, h, *layer_params)


def _dispatch_kernel(tend_ref, nu_ref, pos_ref, h_ref, info_ref, xs_ref, row_ref, zero_ref, sem, zsem,
                     *, tm, tm_exp, n_tiles, d_model):
    def fill(j):
        return pltpu.make_async_copy(zero_ref, xs_ref.at[pl.ds(j * tm_exp, tm_exp)], zsem)

    @pl.when(pl.program_id(0) == 0)
    def _():
        zero_ref[...] = jnp.zeros_like(zero_ref)
        for wait in (False, True):
            for c in range(N_CLASSES):
                prev = tend_ref[c - 1] if c else 0

                @pl.when(tend_ref[c] > prev)
                def _():
                    cp = fill(tend_ref[c] - 1)
                    cp.wait() if wait else cp.start()

            def tail(j, carry):
                cp = fill(j)
                cp.wait() if wait else cp.start()
                return carry
            lax.fori_loop(nu_ref[0], n_tiles, tail, 0)

    i = pl.program_id(0)
    slot = i % 2
    rows = row_ref.at[slot]
    rows[:, 0:d_model] = h_ref[...]
    lane = lax.broadcasted_iota(jnp.int32, (ROUTE_LANES, tm), 0)
    rows[:, d_model:] = jnp.where(lane == 0, info_ref[2:3, :],
                                  jnp.where(lane == 1, info_ref[3:4, :], 0.0)).T
    for r in range(tm):
        pltpu.make_async_copy(rows.at[pl.ds(r, 1)], xs_ref.at[pl.ds(pos_ref[r], 1)],
                              sem.at[slot]).start(priority=r % DMA_THREADS)

    def wait_all(s):
        pltpu.make_async_copy(row_ref.at[s], xs_ref.at[pl.ds(0, tm)], sem.at[s]).wait()

    @pl.when(i > 0)
    def _():
        wait_all(1 - slot)

    @pl.when(i == pl.num_programs(0) - 1)
    def _():
        wait_all(slot)


def _dispatch(h, info, pos, tile_end, n_used, *, tm, tm_exp, n_sorted):
    T, D = h.shape
    W = D + ROUTE_LANES
    grid_spec = pltpu.PrefetchScalarGridSpec(
        num_scalar_prefetch=2,
        grid=(T // tm,),
        in_specs=[pl.BlockSpec((tm,), lambda i, te, nu: (i,), memory_space=pltpu.SMEM),
                  pl.BlockSpec((tm, D), lambda i, te, nu: (i, 0)),
                  pl.BlockSpec((8, tm), lambda i, te, nu: (0, i))],
        out_specs=pl.BlockSpec(memory_space=pl.ANY),
        scratch_shapes=[pltpu.VMEM((2, tm, W), F32), pltpu.VMEM((tm_exp, W), F32),
                        pltpu.SemaphoreType.DMA((2,)), pltpu.SemaphoreType.DMA(())],
    )
    return pl.pallas_call(
        functools.partial(_dispatch_kernel, tm=tm, tm_exp=tm_exp, n_tiles=n_sorted // tm_exp,
                          d_model=D),
        grid_spec=grid_spec,
        out_shape=jax.ShapeDtypeStruct((n_sorted, W), F32),
        compiler_params=_cparams(("arbitrary",)),
        name="dispatch",
    )(tile_end, n_used, pos, h, info)


def _collect_norm_kernel(pos0_ref, posn_ref, ys_ref, nw_ref, o_ref, buf_ref, sem, *, tm):
    i = pl.program_id(0)
    slot = i % 2

    @pl.when(i == 0)
    def _():
        _row_gather(ys_ref, pos0_ref, buf_ref.at[0], sem.at[0], tm)

    _row_gather(ys_ref, posn_ref, buf_ref.at[1 - slot], sem.at[1 - slot], tm)
    _row_gather_wait(ys_ref, buf_ref.at[slot], sem.at[slot], tm)
    x = buf_ref[slot]
    ms = jnp.mean(x * x, axis=-1, keepdims=True)
    o_ref[...] = (x * lax.rsqrt(ms + EPS)) * nw_ref[...]

    @pl.when(i == pl.num_programs(0) - 1)
    def _():
        _row_gather_wait(ys_ref, buf_ref.at[1 - slot], sem.at[1 - slot], tm)


def _collect_norm(ys, pos, norm_w, *, tm):
    n_out = pos.shape[0]
    D = ys.shape[1]
    n_steps = n_out // tm
    return pl.pallas_call(
        functools.partial(_collect_norm_kernel, tm=tm),
        grid=(n_steps,),
        in_specs=[pl.BlockSpec((tm,), lambda i: (0,), memory_space=pltpu.SMEM),
                  pl.BlockSpec((tm,), lambda i: (jnp.minimum(i + 1, n_steps - 1),),
                               memory_space=pltpu.SMEM),
                  pl.BlockSpec(memory_space=pl.ANY),
                  _resident(norm_w.shape)],
        out_specs=pl.BlockSpec((tm, D), lambda i: (i, 0)),
        out_shape=jax.ShapeDtypeStruct((n_out, D), F32),
        scratch_shapes=[pltpu.VMEM((2, tm, D), F32), pltpu.SemaphoreType.DMA((2,))],
        compiler_params=_cparams(("arbitrary",)),
        name="collect_norm",
    )(pos, pos, ys, norm_w)


def _moe_kernel(ea_ref, eb_ref, nu_ref, xs_ref, n2_ref, wga_ref, wua_ref, wda_ref,
                wgb_ref, wub_ref, wdb_ref, ys_ref, *, d_model):
    i = pl.program_id(0)

    @pl.when(i < nu_ref[0])
    def _():
        x = xs_ref[:, 0:d_model]
        ms = jnp.mean(x * x, axis=-1, keepdims=True)
        xn = ((x * lax.rsqrt(ms + EPS)) * n2_ref[...]).astype(BF16)
        y = x
        for col, (wg, wu, wd) in enumerate(((wga_ref, wua_ref, wda_ref), (wgb_ref, wub_ref, wdb_ref))):
            gate = _dot(xn, wg[...])
            up = _dot(xn, wu[...])
            hid = (gate * _sigmoid(gate) * up).astype(BF16)
            y = y + xs_ref[:, d_model + col:d_model + col + 1] * _dot(hid, wd[...])
        ys_ref[...] = y

    @pl.when(i >= nu_ref[0])
    def _():
        ys_ref[...] = jnp.zeros_like(ys_ref)


def _moe(xs, tile_ea, tile_eb, n_used, n2, weg, weu, wed, layer, *, tm):
    n_sorted, W = xs.shape
    D = n2.shape[2]
    de = weg.shape[3]
    row = lambda i, ea, eb, nu: (jnp.minimum(i, nu[0] - 1), 0)
    wa = lambda i, ea, eb, nu: (layer, ea[i], 0, 0)
    wb = lambda i, ea, eb, nu: (layer, eb[i], 0, 0)
    grid_spec = pltpu.PrefetchScalarGridSpec(
        num_scalar_prefetch=3,
        grid=(n_sorted // tm,),
        in_specs=[pl.BlockSpec((tm, W), row),
                  pl.BlockSpec((None, 1, D), lambda i, ea, eb, nu: (layer, 0, 0)),
                  pl.BlockSpec((None, None, D, de), wa), pl.BlockSpec((None, None, D, de), wa),
                  pl.BlockSpec((None, None, de, D), wa),
                  pl.BlockSpec((None, None, D, de), wb), pl.BlockSpec((None, None, D, de), wb),
                  pl.BlockSpec((None, None, de, D), wb)],
        out_specs=pl.BlockSpec((tm, D), lambda i, ea, eb, nu: (i, 0)),
    )
    return pl.pallas_call(
        functools.partial(_moe_kernel, d_model=D),
        grid_spec=grid_spec,
        out_shape=jax.ShapeDtypeStruct((n_sorted, D), F32),
        compiler_params=_cparams(("arbitrary",)),
        name="experts",
    )(tile_ea, tile_eb, n_used, xs, n2, weg, weu, wed, weg, weu, wed)


def _row_tile(T, cap):
    t = cap
    while T % t:
        t //= 2
    return t


def kernel(x, meta_tokens, norm1_w, w_in, w_gate_up, b_gate, gla_norm_w, w_pool_grp, pool_scale,
           w_br_gla, w_br_pool, w_out, norm2_w, w_router_group, w_router_expert, w_exp_gate,
           w_exp_up, w_exp_down, final_norm_w):
    B, S, D = x.shape
    depth = w_in.shape[0]
    key = w_gate_up.shape[2]
    rank = w_gate_up.shape[1]
    dv = gla_norm_w.shape[1]
    val = GLA_HEADS * dv
    dk = key // GLA_HEADS
    assert S % CHUNK == 0 and key * 2 == D and val == D
    assert w_pool_grp.shape[2] * len(POOL_WINDOWS) == D
    LP = PAD + N_META + S
    n_chunks = LP // CHUNK
    T = B * LP
    tm_proj = tm_mix = tm_row = _row_tile(T, ROW_TILE)
    tm_exp = EXPERT_TILE
    n_sorted = -(-(T + N_CLASSES * (tm_exp - 1)) // tm_exp) * tm_exp
    n_tiles = n_sorted // tm_exp

    head = jnp.concatenate([jnp.zeros((PAD, D), F32), meta_tokens.astype(F32)], axis=0)
    lp = np.arange(T) % LP
    ys = x.astype(F32).reshape(B * S, D)
    pos = jnp.asarray(np.where(lp >= CHUNK, (np.arange(T) // LP) * S + lp - CHUNK, 0), jnp.int32)

    c_q = 0
    c_gl = 2 * key + 2 * val
    c_u = c_gl + rank

    w_a = w_in[:, :, c_q:c_gl].astype(BF16)
    w_b = w_in[:, :, c_u:].astype(BF16)
    w_g = jnp.concatenate([w_in[:, :, c_gl:c_u].astype(BF16),
                           jnp.zeros((depth, D, LANES - rank), BF16)], axis=2)
    wg_hi, wg_lo = _split(jnp.concatenate(
        [w_gate_up, jnp.zeros((depth, LANES - rank, key), F32)], axis=1))
    wr = jnp.swapaxes(jnp.concatenate(
        [w_router_group, w_router_expert,
         jnp.zeros((depth, D, CLASS_ROWS - N_GROUPS - N_EXPERTS), F32)], axis=2), 1, 2)
    row_vec = lambda p: p[:, None, :]
    mix_params = (row_vec(gla_norm_w), w_pool_grp.astype(BF16), row_vec(pool_scale),
                  w_br_gla.astype(BF16), w_br_pool.astype(BF16), w_out.astype(BF16),
                  row_vec(norm2_w), wr)
    weg, weu, wed = w_exp_gate.astype(BF16), w_exp_up.astype(BF16), w_exp_down.astype(BF16)

    for l in range(depth):
        z, zg, h = _inproj_gather(ys, pos, head, row_vec(norm1_w), w_a, w_b, w_g, l, tm=tm_proj,
                                  seq_rows=LP if l == 0 else 0, n_batch=B)
        ogla = _gla(z, zg, wg_hi, wg_lo, row_vec(b_gate), l, batch=B, n_chunks=n_chunks, dk=dk, dv=dv)
        hx, info, counts = _mix(ogla, z, h, mix_params, l, tm=tm_mix, seq_rows=LP)

        cnt = counts[:N_CLASSES, 0].astype(jnp.int32)
        seg_tiles = (cnt + tm_exp - 1) // tm_exp
        tile_end = jnp.cumsum(seg_tiles)
        start = (tile_end - seg_tiles) * tm_exp
        n_used = tile_end[-1:]
        tile_cls = jnp.minimum(
            jnp.sum(jnp.arange(n_tiles)[:, None] >= tile_end[None, :], axis=1), N_CLASSES - 1)
        tile_cls = jnp.where(jnp.arange(n_tiles) < n_used[0], tile_cls,
                             jnp.take(tile_cls, jnp.maximum(n_used[0] - 1, 0)))
        pair_lo = jnp.array([p[0] for p in PAIRS], jnp.int32)
        pair_hi = jnp.array([p[1] for p in PAIRS], jnp.int32)
        grp = tile_cls // len(PAIRS)
        tile_ea = (grp * EXPERTS_PER_GROUP + jnp.take(pair_lo, tile_cls % len(PAIRS))).astype(jnp.int32)
        tile_eb = (grp * EXPERTS_PER_GROUP + jnp.take(pair_hi, tile_cls % len(PAIRS))).astype(jnp.int32)
        cls_t = info[0].astype(jnp.int32)
        pos = info[1].astype(jnp.int32) + jnp.sum(
            jnp.where(cls_t[:, None] == jnp.arange(N_CLASSES)[None, :], start[None, :], 0), axis=1)

        xs = _dispatch(hx, info, pos, tile_end.astype(jnp.int32), n_used.astype(jnp.int32),
                       tm=tm_row, tm_exp=tm_exp, n_sorted=n_sorted)
        ys = _moe(xs, tile_ea, tile_eb, n_used.astype(jnp.int32), row_vec(norm2_w), weg, weu, wed, l,
                  tm=tm_exp)

    pos_out = pos.reshape(B, LP)[:, PAD + N_META:].reshape(B * S)
    out = _collect_norm(ys, pos_out, final_norm_w[None], tm=_row_tile(B * S, ROW_TILE))
    return out.reshape(B, S, D)
```

```python
import functools

import jax
import jax.numpy as jnp
import numpy as np
from jax import lax
from jax.experimental import pallas as pl
from jax.experimental.pallas import tpu as pltpu

F32 = jnp.float32
BF16 = jnp.bfloat16

EPS = 1e-6
CHUNK = 64
N_META = 16
PAD = CHUNK - N_META
GLA_HEADS = 4
GATE_NORMALIZER = 16.0
POOL_WINDOWS = (2, 4, 8, 16)
N_GROUPS = 4
EXPERTS_PER_GROUP = 4
N_EXPERTS = N_GROUPS * EXPERTS_PER_GROUP
PAIRS = ((0, 1), (0, 2), (0, 3), (1, 2), (1, 3), (2, 3))
N_CLASSES = N_GROUPS * len(PAIRS)
CLASS_ROWS = 32
LANES = 128
HALO = 32
ROUTE_LANES = 128
DMA_THREADS = 2
MIX_SUBTILES = 1

VMEM_LIMIT = 56 * 1024 * 1024
ROW_TILE = 512
EXPERT_TILE = 256
PROJ_COL_BLOCK = 512


def _cparams(sem):
    return pltpu.CompilerParams(dimension_semantics=sem, vmem_limit_bytes=VMEM_LIMIT)


def _resident(shape):
    nd = len(shape)
    return pl.BlockSpec(shape, lambda *_: (0,) * nd, pipeline_mode=pl.Buffered(1))


def _layer_block(arr, layer):
    nd = arr.ndim
    return pl.BlockSpec((None,) + arr.shape[1:], lambda *_: (layer,) + (0,) * (nd - 1),
                        pipeline_mode=pl.Buffered(1))


def _split(x):
    hi = x.astype(BF16)
    lo = (x - hi.astype(F32)).astype(BF16)
    return hi, lo


def _dot(a, b):
    return jnp.dot(a, b, preferred_element_type=F32)


def _dot_nt(a, b):
    return lax.dot_general(a, b, (((1,), (1,)), ((), ())), preferred_element_type=F32)


def _dot_tn(a, b):
    return lax.dot_general(a, b, (((0,), (0,)), ((), ())), preferred_element_type=F32)


def _sigmoid(x):
    return 0.5 * jnp.tanh(0.5 * x) + 0.5


def _row_gather(src_ref, pos_ref, dst_ref, sem, tm):
    for r in range(tm):
        pltpu.make_async_copy(src_ref.at[pl.ds(pos_ref[r], 1)], dst_ref.at[pl.ds(r, 1)],
                              sem).start(priority=r % DMA_THREADS)


def _row_gather_wait(src_ref, dst_ref, sem, tm):
    pltpu.make_async_copy(src_ref.at[pl.ds(0, tm)], dst_ref, sem).wait()


def _inproj_gather_kernel(pos0_ref, posn_ref, ys_ref, head_ref, nw_ref, wa_ref, wb_ref, wg_ref,
                          z_ref, zg_ref, h_ref, buf_ref, xn_ref, sem,
                          *, tm, col_blk, seq_rows, n_batch):
    i = pl.program_id(0)
    last = pl.num_programs(0) - 1
    slot = i % 2

    @pl.when(i == 0)
    def _():
        _row_gather(ys_ref, pos0_ref, buf_ref.at[0], sem.at[0], tm)

    _row_gather_wait(ys_ref, buf_ref.at[slot], sem.at[slot], tm)
    if seq_rows:
        b0 = (i * tm + seq_rows - 1) // seq_rows
        for k in range(tm // seq_rows + 1):
            off = (b0 + k) * seq_rows - i * tm

            @pl.when(jnp.logical_and(off < tm, b0 + k < n_batch))
            def _():
                buf_ref[slot, pl.ds(pl.multiple_of(off, CHUNK), CHUNK), :] = head_ref[...]

    x = buf_ref[slot]
    h_ref[...] = x
    ms = jnp.mean(x * x, axis=-1, keepdims=True)
    xn_ref[...] = ((x * lax.rsqrt(ms + EPS)) * nw_ref[...]).astype(BF16)
    _row_gather(ys_ref, posn_ref, buf_ref.at[1 - slot], sem.at[1 - slot], tm)
    col = 0
    for w_ref in (wa_ref, wb_ref):
        for j in range(w_ref.shape[1] // col_blk):
            sl = slice(j * col_blk, (j + 1) * col_blk)
            z_ref[:, col:col + col_blk] = _dot(xn_ref[...], w_ref[:, sl]).astype(BF16)
            col += col_blk
    zg_ref[...] = _dot(xn_ref[...], wg_ref[...])

    @pl.when(i == last)
    def _():
        _row_gather_wait(ys_ref, buf_ref.at[1 - slot], sem.at[1 - slot], tm)


def _inproj_gather(ys, pos, head, norm_w, w_a, w_b, w_g, layer, *, tm, seq_rows=0, n_batch=0):
    T = pos.shape[0]
    D = ys.shape[1]
    n_main = w_a.shape[2] + w_b.shape[2]
    n_all = n_main + w_g.shape[2]
    n_steps = T // tm
    assert tm % CHUNK == 0 and seq_rows % CHUNK == 0
    return pl.pallas_call(
        functools.partial(_inproj_gather_kernel, tm=tm, col_blk=PROJ_COL_BLOCK,
                          seq_rows=seq_rows, n_batch=n_batch),
        grid=(n_steps,),
        in_specs=[pl.BlockSpec((tm,), lambda i: (0,), memory_space=pltpu.SMEM),
                  pl.BlockSpec((tm,), lambda i: (jnp.minimum(i + 1, n_steps - 1),),
                               memory_space=pltpu.SMEM),
                  pl.BlockSpec(memory_space=pl.ANY),
                  _resident(head.shape),
                  _layer_block(norm_w, layer), _layer_block(w_a, layer),
                  _layer_block(w_b, layer), _layer_block(w_g, layer)],
        out_specs=[pl.BlockSpec((tm, n_main), lambda i: (i, 0)),
                   pl.BlockSpec((tm, n_all - n_main), lambda i: (i, 0)),
                   pl.BlockSpec((tm, D), lambda i: (i, 0))],
        out_shape=[jax.ShapeDtypeStruct((T, n_main), BF16),
                   jax.ShapeDtypeStruct((T, n_all - n_main), F32),
                   jax.ShapeDtypeStruct((T, D), F32)],
        scratch_shapes=[pltpu.VMEM((2, tm, D), F32), pltpu.VMEM((tm, D), BF16),
                        pltpu.SemaphoreType.DMA((2,))],
        compiler_params=_cparams(("arbitrary",)),
        name="inproj_gather",
    )(pos, pos, ys, head, norm_w, w_a, w_b, w_g)


def _gla_prep(q_ref, k_ref, zg_ref, wgh_ref, wgl_ref, bg_ref, scaled_ref, dec_ref, slot, first,
              *, bb, dk):
    rows = bb * CHUNK
    key = GLA_HEADS * dk
    gl = zg_ref[...].reshape(rows, LANES).astype(BF16)
    gpre = _dot(gl, wgh_ref[...]) + _dot(gl, wgl_ref[...]) + bg_ref[...]
    g = (jnp.minimum(gpre, 0.0) - jnp.log(1.0 + jnp.exp(-jnp.abs(gpre)))) * (1.0 / GATE_NORMALIZER)
    if first:
        row = lax.broadcasted_iota(jnp.int32, (rows, 1), 0)
        g = jnp.where((row & (CHUNK - 1)) < PAD, 0.0, g)

    ri = lax.broadcasted_iota(jnp.int32, (rows, rows), 0)
    ci = lax.broadcasted_iota(jnp.int32, (rows, rows), 1)
    tri = jnp.where(jnp.logical_and((ri // CHUNK) == (ci // CHUNK), ri >= ci), 1.0, 0.0).astype(BF16)
    g_hi, g_lo = _split(g)
    bcum = _dot(tri, g_hi) + _dot(tri, g_lo)
    gam = jnp.concatenate(
        [jnp.broadcast_to(bcum[(c + 1) * CHUNK - 1:(c + 1) * CHUNK, :], (CHUNK, key)) for c in range(bb)],
        axis=0)
    eb = jnp.exp(bcum)
    ieb = jnp.exp(-bcum)
    q = q_ref[...].reshape(rows, key).astype(F32) * (dk ** -0.5)
    k = k_ref[...].reshape(rows, key).astype(F32)
    scaled_ref[slot, 0] = (q * eb).astype(BF16)
    scaled_ref[slot, 1] = (q * ieb).astype(BF16)
    scaled_ref[slot, 2] = (k * eb).astype(BF16)
    scaled_ref[slot, 3] = (k * ieb).astype(BF16)
    scaled_ref[slot, 4] = (k * jnp.exp(gam - bcum)).astype(BF16)
    for c in range(bb):
        dec_ref[slot, c] = eb[(c + 1) * CHUNK - 1:(c + 1) * CHUNK, :]


def _gla_kernel(q0_ref, k0_ref, zg0_ref, qn_ref, kn_ref, zgn_ref, v_ref, wgh_ref, wgl_ref, bg_ref,
                o_ref, st_ref, scaled_ref, dec_ref, *, bb, dk, dv):
    n = pl.program_id(1)
    prep = functools.partial(_gla_prep, wgh_ref=wgh_ref, wgl_ref=wgl_ref, bg_ref=bg_ref,
                             scaled_ref=scaled_ref, dec_ref=dec_ref, bb=bb, dk=dk)

    @pl.when(n == 0)
    def _():
        st_ref[...] = jnp.zeros_like(st_ref)
        prep(q0_ref, k0_ref, zg0_ref, slot=0, first=True)

    li = lax.broadcasted_iota(jnp.int32, (CHUNK, CHUNK), 0)
    lj = lax.broadcasted_iota(jnp.int32, (CHUNK, CHUNK), 1)
    lower = li >= lj
    groups = [(c, hd) for c in range(bb) for hd in range(GLA_HEADS)]

    def step(slot):
        prep(qn_ref, kn_ref, zgn_ref, slot=1 - slot, first=False)
        attn = {}
        for c, hd in groups:
            rs = slice(c * CHUNK, (c + 1) * CHUNK)
            ks = slice(hd * dk, (hd + 1) * dk)
            a_lo = _dot_nt(scaled_ref[slot, 0, rs, ks], scaled_ref[slot, 3, rs, ks])
            a_up = _dot_nt(scaled_ref[slot, 1, rs, ks], scaled_ref[slot, 2, rs, ks])
            attn[c, hd] = jnp.where(lower, a_lo, a_up).astype(BF16)
        for c, hd in groups:
            rs = slice(c * CHUNK, (c + 1) * CHUNK)
            ks = slice(hd * dk, (hd + 1) * dk)
            vs = slice(hd * dv, (hd + 1) * dv)
            o_ref[c, :, vs] = (_dot(attn[c, hd], v_ref[c, :, vs])
                               + _dot(scaled_ref[slot, 0, rs, ks], st_ref[c, hd].astype(BF16))
                               ).astype(BF16)
        for c, hd in groups:
            rs = slice(c * CHUNK, (c + 1) * CHUNK)
            ks = slice(hd * dk, (hd + 1) * dk)
            vs = slice(hd * dv, (hd + 1) * dv)
            dec = jnp.broadcast_to(dec_ref[slot, c, :, ks], (dk, dk)).T
            st_ref[c, hd] = (st_ref[c, hd] * jnp.concatenate([dec] * (dv // dk), axis=1)
                             + _dot_tn(scaled_ref[slot, 4, rs, ks], v_ref[c, :, vs]))

    for slot in (0, 1):
        pl.when(n % 2 == slot)(functools.partial(step, slot))


def _gla(z, zg, wg_hi, wg_lo, bg, layer, *, batch, n_chunks, dk, dv):
    T = z.shape[0]
    key = GLA_HEADS * dk
    val = GLA_HEADS * dv
    bb = 8 if batch % 8 == 0 else (4 if batch % 4 == 0 else 1)
    z3 = z.reshape(batch, n_chunks * CHUNK, z.shape[1])
    zg3 = zg.reshape(batch, n_chunks * CHUNK, zg.shape[1])
    nxt = lambda n: jnp.minimum(n + 1, n_chunks - 1)
    o = pl.pallas_call(
        functools.partial(_gla_kernel, bb=bb, dk=dk, dv=dv),
        grid=(batch // bb, n_chunks),
        in_specs=[pl.BlockSpec((bb, CHUNK, key), lambda b, n: (b, 0, 0)),
                  pl.BlockSpec((bb, CHUNK, key), lambda b, n: (b, 0, 1)),
                  pl.BlockSpec((bb, CHUNK, LANES), lambda b, n: (b, 0, 0)),
                  pl.BlockSpec((bb, CHUNK, key), lambda b, n: (b, nxt(n), 0)),
                  pl.BlockSpec((bb, CHUNK, key), lambda b, n: (b, nxt(n), 1)),
                  pl.BlockSpec((bb, CHUNK, LANES), lambda b, n: (b, nxt(n), 0)),
                  pl.BlockSpec((bb, CHUNK, val), lambda b, n: (b, n, 1)),
                  _layer_block(wg_hi, layer), _layer_block(wg_lo, layer), _layer_block(bg, layer)],
        out_specs=pl.BlockSpec((bb, CHUNK, val), lambda b, n: (b, n, 0)),
        out_shape=jax.ShapeDtypeStruct((batch, n_chunks * CHUNK, val), BF16),
        scratch_shapes=[pltpu.VMEM((bb, GLA_HEADS, dk, dv), F32),
                        pltpu.VMEM((2, 5, bb * CHUNK, key), BF16),
                        pltpu.VMEM((2, bb, 1, key), F32)],
        compiler_params=_cparams(("parallel", "arbitrary")),
        name="gla",
    )(z3, z3, zg3, z3, z3, zg3, z3, wg_hi, wg_lo, bg)
    return o.reshape(T, val)


def _mix_kernel(ogla_ref, og_ref, u_ref, halo_ref, g0_ref, g1_ref, h_ref, gnw_ref,
                wgrp_ref, pscale_ref, wbg_ref, wbp_ref, wout_ref, n2_ref, wr_ref,
                hx_ref, info_ref, cnt_ref,
                ue_ref, p2_ref, p4_ref, p8_ref, p16_ref, tri_ref, carry_ref,
                *, tm, seq_rows, d_model):
    i = pl.program_id(0)
    gd = d_model // len(POOL_WINDOWS)

    @pl.when(i == 0)
    def _():
        r = lax.broadcasted_iota(jnp.int32, (tm, tm), 0)
        c = lax.broadcasted_iota(jnp.int32, (tm, tm), 1)
        tri_ref[...] = jnp.where(r < c, 1.0, 0.0).astype(BF16)
        carry_ref[...] = jnp.zeros_like(carry_ref)

    ue_ref[0:HALO, :] = halo_ref[...].astype(F32)
    ue_ref[HALO:, :] = u_ref[...].astype(F32)
    n8 = tm + HALO - 8
    p2_ref[8:, :] = ue_ref[8:, :] + ue_ref[pl.ds(7, n8), :]
    p4_ref[16:, :] = p2_ref[16:, gd:] + p2_ref[pl.ds(14, n8 - 8), gd:]
    p8_ref[24:, :] = p4_ref[24:, gd:] + p4_ref[pl.ds(20, n8 - 16), gd:]
    p16_ref[HALO:, :] = p8_ref[HALO:, gd:] + p8_ref[pl.ds(24, tm), gd:]
    sum_refs = (p2_ref, p4_ref, p8_ref, p16_ref)

    sub = tm // MIX_SUBTILES
    for part in range(MIX_SUBTILES):
        _mix_rows(part * sub, sub, i * tm, ogla_ref, og_ref, g0_ref, g1_ref, h_ref, gnw_ref, wgrp_ref,
                  pscale_ref, wbg_ref, wbp_ref, wout_ref, n2_ref, wr_ref, hx_ref, info_ref,
                  ue_ref, sum_refs, tri_ref, carry_ref, seq_rows=seq_rows, d_model=d_model)
    cnt_ref[...] = carry_ref[...]


def _mix_rows(r0, rows, tile_row0, ogla_ref, og_ref, g0_ref, g1_ref, h_ref, gnw_ref, wgrp_ref,
              pscale_ref, wbg_ref, wbp_ref, wout_ref, n2_ref, wr_ref, hx_ref, info_ref,
              ue_ref, sum_refs, tri_ref, carry_ref, *, seq_rows, d_model):
    gd = d_model // len(POOL_WINDOWS)
    rs = pl.ds(r0, rows)
    rh = pl.ds(r0 + HALO, rows)

    rowf = (tile_row0 + r0 + lax.broadcasted_iota(jnp.int32, (rows, 1), 0)).astype(F32)
    lp = rowf - jnp.floor(rowf / float(seq_rows)) * float(seq_rows)
    is_pad = lp < float(PAD)
    seqpos1 = lp - float(PAD - 1)

    br_p = None
    for gi, w in enumerate(POOL_WINDOWS):
        cs = slice(gi * gd, (gi + 1) * gd)
        inv_cnt = 1.0 / jnp.clip(seqpos1, 1.0, float(w))
        pooled = sum_refs[gi][rh, 0:gd] * inv_cnt - ue_ref[rh, cs]
        mixed = _dot(pooled.astype(BF16), wgrp_ref[gi]) * pscale_ref[:, cs]
        part = _dot(mixed.astype(BF16), wbp_ref[cs, :])
        br_p = part if br_p is None else br_p + part

    dv = gnw_ref.shape[1]
    br_g = None
    for hd in range(GLA_HEADS):
        vs = slice(hd * dv, (hd + 1) * dv)
        o = ogla_ref[rs, vs].astype(F32)
        o = o * lax.rsqrt(jnp.mean(o * o, axis=-1, keepdims=True) + EPS) * gnw_ref[...]
        og = og_ref[rs, vs]
        ygla = o.astype(BF16) * (og * _sigmoid(og))
        part = _dot(ygla, wbg_ref[vs, :])
        br_g = part if br_g is None else br_g + part
    merged = (_sigmoid(g0_ref[rs, :]) * br_g.astype(BF16) + _sigmoid(g1_ref[rs, :]) * br_p.astype(BF16))
    delta = _dot(merged, wout_ref[...])
    h_new = h_ref[rs, :] + jnp.where(is_pad, 0.0, delta)
    hx_ref[rs, :] = h_new

    ms = jnp.mean(h_new * h_new, axis=-1, keepdims=True)
    xn = (h_new * lax.rsqrt(ms + EPS)) * n2_ref[...]
    xb = xn.astype(BF16)
    w_hi, w_lo = _split(wr_ref[...])
    lt = _dot_nt(w_hi, xb) + _dot_nt(w_lo, xb)

    lg = [lt[j:j + 1, :] for j in range(N_GROUPS)]
    m = jnp.maximum(jnp.maximum(lg[0], lg[1]), jnp.maximum(lg[2], lg[3]))
    gidx = jnp.where(lg[0] == m, 0, jnp.where(lg[1] == m, 1, jnp.where(lg[2] == m, 2, 3)))
    pg_top = 1.0 / (jnp.exp(lg[0] - m) + jnp.exp(lg[1] - m) + jnp.exp(lg[2] - m) + jnp.exp(lg[3] - m))
    sel = []
    for e in range(EXPERTS_PER_GROUP):
        cand = [lt[N_GROUPS + g * EXPERTS_PER_GROUP + e:N_GROUPS + g * EXPERTS_PER_GROUP + e + 1, :]
                for g in range(N_GROUPS)]
        sel.append(jnp.where(gidx == 0, cand[0], jnp.where(gidx == 1, cand[1],
                                                           jnp.where(gidx == 2, cand[2], cand[3]))))
    m1 = jnp.maximum(jnp.maximum(sel[0], sel[1]), jnp.maximum(sel[2], sel[3]))
    i1 = jnp.where(sel[0] == m1, 0, jnp.where(sel[1] == m1, 1, jnp.where(sel[2] == m1, 2, 3)))
    neg = jnp.float32(-jnp.inf)
    rest = [jnp.where(i1 == e, neg, sel[e]) for e in range(EXPERTS_PER_GROUP)]
    m2 = jnp.maximum(jnp.maximum(rest[0], rest[1]), jnp.maximum(rest[2], rest[3]))
    i2 = jnp.where(rest[0] == m2, 0, jnp.where(rest[1] == m2, 1, jnp.where(rest[2] == m2, 2, 3)))
    r21 = jnp.exp(m2 - m1)
    w1 = pg_top / (1.0 + r21)
    w2 = pg_top * r21 / (1.0 + r21)
    lo = jnp.minimum(i1, i2)
    hi = jnp.maximum(i1, i2)
    w_lo = jnp.where(i1 < i2, w1, w2)
    w_hi = jnp.where(i1 < i2, w2, w1)
    pidx = jnp.where(lo == 0, hi - 1, jnp.where(lo == 1, hi + 1, 5))
    cls = gidx * len(PAIRS) + pidx

    crow = lax.broadcasted_iota(jnp.int32, (CLASS_ROWS, rows), 0)
    onehot = jnp.where(crow == cls, 1.0, 0.0)
    prefix = _dot(onehot.astype(BF16), tri_ref[0:rows, 0:rows]) + carry_ref[:, 0:1]
    rank = jnp.sum(onehot * prefix, axis=0, keepdims=True)
    carry_ref[...] = carry_ref[...] + jnp.sum(onehot, axis=1, keepdims=True)

    irow = lax.broadcasted_iota(jnp.int32, (8, rows), 0)
    info_ref[:, rs] = jnp.where(irow == 0, cls.astype(F32), jnp.where(
        irow == 1, rank, jnp.where(irow == 2, w_lo, jnp.where(irow == 3, w_hi, 0.0))))


def _mix(ogla, z, h, layer_params, layer, *, tm, seq_rows):
    T, D = h.shape
    nblk = tm // HALO
    return pl.pallas_call(
        functools.partial(_mix_kernel, tm=tm, seq_rows=seq_rows, d_model=D),
        grid=(T // tm,),
        in_specs=[pl.BlockSpec((tm, D), lambda i: (i, 0)),
                  pl.BlockSpec((tm, D), lambda i: (i, 2)),
                  pl.BlockSpec((tm, D), lambda i: (i, 3)),
                  pl.BlockSpec((HALO, D), lambda i: (jnp.maximum(i * nblk - 1, 0), 3)),
                  pl.BlockSpec((tm, D), lambda i: (i, 4)),
                  pl.BlockSpec((tm, D), lambda i: (i, 5)),
                  pl.BlockSpec((tm, D), lambda i: (i, 0))]
                 + [_layer_block(p, layer) for p in layer_params],
        out_specs=[pl.BlockSpec((tm, D), lambda i: (i, 0)),
                   pl.BlockSpec((8, tm), lambda i: (0, i)),
                   pl.BlockSpec((CLASS_ROWS, LANES), lambda i: (0, 0))],
        out_shape=[jax.ShapeDtypeStruct((T, D), F32),
                   jax.ShapeDtypeStruct((8, T), F32),
                   jax.ShapeDtypeStruct((CLASS_ROWS, LANES), F32)],
        scratch_shapes=[pltpu.VMEM((tm + HALO, D), F32),
                        pltpu.VMEM((tm + HALO, D), F32),
                        pltpu.VMEM((tm + HALO, D - D // 4), F32),
                        pltpu.VMEM((tm + HALO, D - 2 * (D // 4)), F32),
                        pltpu.VMEM((tm + HALO, D // 4), F32),
                        pltpu.VMEM((tm, tm), BF16),
                        pltpu.VMEM((CLASS_ROWS, LANES), F32)],
        compiler_params=_cparams(("arbitrary",)),
        name="mix",
    )(ogla, z, z, z, z, z, h, *layer_params)


def _dispatch_kernel(tend_ref, nu_ref, pos_ref, h_ref, info_ref, xs_ref, row_ref, zero_ref, sem, zsem,
                     *, tm, tm_exp, n_tiles, d_model):
    def fill(j):
        return pltpu.make_async_copy(zero_ref, xs_ref.at[pl.ds(j * tm_exp, tm_exp)], zsem)

    @pl.when(pl.program_id(0) == 0)
    def _():
        zero_ref[...] = jnp.zeros_like(zero_ref)
        for wait in (False, True):
            for c in range(N_CLASSES):
                prev = tend_ref[c - 1] if c else 0

                @pl.when(tend_ref[c] > prev)
                def _():
                    cp = fill(tend_ref[c] - 1)
                    cp.wait() if wait else cp.start()

            def tail(j, carry):
                cp = fill(j)
                cp.wait() if wait else cp.start()
                return carry
            lax.fori_loop(nu_ref[0], n_tiles, tail, 0)

    i = pl.program_id(0)
    slot = i % 2
    rows = row_ref.at[slot]
    rows[:, 0:d_model] = h_ref[...]
    lane = lax.broadcasted_iota(jnp.int32, (ROUTE_LANES, tm), 0)
    rows[:, d_model:] = jnp.where(lane == 0, info_ref[2:3, :],
                                  jnp.where(lane == 1, info_ref[3:4, :], 0.0)).T
    for r in range(tm):
        pltpu.make_async_copy(rows.at[pl.ds(r, 1)], xs_ref.at[pl.ds(pos_ref[r], 1)],
                              sem.at[slot]).start(priority=r % DMA_THREADS)

    def wait_all(s):
        pltpu.make_async_copy(row_ref.at[s], xs_ref.at[pl.ds(0, tm)], sem.at[s]).wait()

    @pl.when(i > 0)
    def _():
        wait_all(1 - slot)

    @pl.when(i == pl.num_programs(0) - 1)
    def _():
        wait_all(slot)


def _dispatch(h, info, pos, tile_end, n_used, *, tm, tm_exp, n_sorted):
    T, D = h.shape
    W = D + ROUTE_LANES
    grid_spec = pltpu.PrefetchScalarGridSpec(
        num_scalar_prefetch=2,
        grid=(T // tm,),
        in_specs=[pl.BlockSpec((tm,), lambda i, te, nu: (i,), memory_space=pltpu.SMEM),
                  pl.BlockSpec((tm, D), lambda i, te, nu: (i, 0)),
                  pl.BlockSpec((8, tm), lambda i, te, nu: (0, i))],
        out_specs=pl.BlockSpec(memory_space=pl.ANY),
        scratch_shapes=[pltpu.VMEM((2, tm, W), F32), pltpu.VMEM((tm_exp, W), F32),
                        pltpu.SemaphoreType.DMA((2,)), pltpu.SemaphoreType.DMA(())],
    )
    return pl.pallas_call(
        functools.partial(_dispatch_kernel, tm=tm, tm_exp=tm_exp, n_tiles=n_sorted // tm_exp,
                          d_model=D),
        grid_spec=grid_spec,
        out_shape=jax.ShapeDtypeStruct((n_sorted, W), F32),
        compiler_params=_cparams(("arbitrary",)),
        name="dispatch",
    )(tile_end, n_used, pos, h, info)


def _collect_norm_kernel(pos0_ref, posn_ref, ys_ref, nw_ref, o_ref, buf_ref, sem, *, tm):
    i = pl.program_id(0)
    slot = i % 2

    @pl.when(i == 0)
    def _():
        _row_gather(ys_ref, pos0_ref, buf_ref.at[0], sem.at[0], tm)

    _row_gather(ys_ref, posn_ref, buf_ref.at[1 - slot], sem.at[1 - slot], tm)
    _row_gather_wait(ys_ref, buf_ref.at[slot], sem.at[slot], tm)
    x = buf_ref[slot]
    ms = jnp.mean(x * x, axis=-1, keepdims=True)
    o_ref[...] = (x * lax.rsqrt(ms + EPS)) * nw_ref[...]

    @pl.when(i == pl.num_programs(0) - 1)
    def _():
        _row_gather_wait(ys_ref, buf_ref.at[1 - slot], sem.at[1 - slot], tm)


def _collect_norm(ys, pos, norm_w, *, tm):
    n_out = pos.shape[0]
    D = ys.shape[1]
    n_steps = n_out // tm
    return pl.pallas_call(
        functools.partial(_collect_norm_kernel, tm=tm),
        grid=(n_steps,),
        in_specs=[pl.BlockSpec((tm,), lambda i: (0,), memory_space=pltpu.SMEM),
                  pl.BlockSpec((tm,), lambda i: (jnp.minimum(i + 1, n_steps - 1),),
                               memory_space=pltpu.SMEM),
                  pl.BlockSpec(memory_space=pl.ANY),
                  _resident(norm_w.shape)],
        out_specs=pl.BlockSpec((tm, D), lambda i: (i, 0)),
        out_shape=jax.ShapeDtypeStruct((n_out, D), F32),
        scratch_shapes=[pltpu.VMEM((2, tm, D), F32), pltpu.SemaphoreType.DMA((2,))],
        compiler_params=_cparams(("arbitrary",)),
        name="collect_norm",
    )(pos, pos, ys, norm_w)


def _moe_kernel(ea_ref, eb_ref, nu_ref, xs_ref, n2_ref, wga_ref, wua_ref, wda_ref,
                wgb_ref, wub_ref, wdb_ref, ys_ref, *, d_model):
    i = pl.program_id(0)

    @pl.when(i < nu_ref[0])
    def _():
        x = xs_ref[:, 0:d_model]
        ms = jnp.mean(x * x, axis=-1, keepdims=True)
        xn = ((x * lax.rsqrt(ms + EPS)) * n2_ref[...]).astype(BF16)
        experts = ((wga_ref, wua_ref, wda_ref), (wgb_ref, wub_ref, wdb_ref))
        pre = [(_dot(xn, wg[...]), _dot(xn, wu[...])) for wg, wu, _ in experts]
        hid = [(gate * _sigmoid(gate) * up).astype(BF16) for gate, up in pre]
        y = x
        for col, (_, _, wd) in enumerate(experts):
            y = y + xs_ref[:, d_model + col:d_model + col + 1] * _dot(hid[col], wd[...])
        ys_ref[...] = y

    @pl.when(i >= nu_ref[0])
    def _():
        ys_ref[...] = jnp.zeros_like(ys_ref)


def _moe(xs, tile_ea, tile_eb, n_used, n2, weg, weu, wed, layer, *, tm):
    n_sorted, W = xs.shape
    D = n2.shape[2]
    de = weg.shape[3]
    row = lambda i, ea, eb, nu: (jnp.minimum(i, nu[0] - 1), 0)
    wa = lambda i, ea, eb, nu: (layer, ea[i], 0, 0)
    wb = lambda i, ea, eb, nu: (layer, eb[i], 0, 0)
    grid_spec = pltpu.PrefetchScalarGridSpec(
        num_scalar_prefetch=3,
        grid=(n_sorted // tm,),
        in_specs=[pl.BlockSpec((tm, W), row),
                  pl.BlockSpec((None, 1, D), lambda i, ea, eb, nu: (layer, 0, 0)),
                  pl.BlockSpec((None, None, D, de), wa), pl.BlockSpec((None, None, D, de), wa),
                  pl.BlockSpec((None, None, de, D), wa),
                  pl.BlockSpec((None, None, D, de), wb), pl.BlockSpec((None, None, D, de), wb),
                  pl.BlockSpec((None, None, de, D), wb)],
        out_specs=pl.BlockSpec((tm, D), lambda i, ea, eb, nu: (i, 0)),
    )
    return pl.pallas_call(
        functools.partial(_moe_kernel, d_model=D),
        grid_spec=grid_spec,
        out_shape=jax.ShapeDtypeStruct((n_sorted, D), F32),
        compiler_params=_cparams(("arbitrary",)),
        name="experts",
    )(tile_ea, tile_eb, n_used, xs, n2, weg, weu, wed, weg, weu, wed)


def _row_tile(T, cap):
    t = cap
    while T % t:
        t //= 2
    return t


def kernel(x, meta_tokens, norm1_w, w_in, w_gate_up, b_gate, gla_norm_w, w_pool_grp, pool_scale,
           w_br_gla, w_br_pool, w_out, norm2_w, w_router_group, w_router_expert, w_exp_gate,
           w_exp_up, w_exp_down, final_norm_w):
    B, S, D = x.shape
    depth = w_in.shape[0]
    key = w_gate_up.shape[2]
    rank = w_gate_up.shape[1]
    dv = gla_norm_w.shape[1]
    val = GLA_HEADS * dv
    dk = key // GLA_HEADS
    assert S % CHUNK == 0 and key * 2 == D and val == D
    assert w_pool_grp.shape[2] * len(POOL_WINDOWS) == D
    LP = PAD + N_META + S
    n_chunks = LP // CHUNK
    T = B * LP
    tm_proj = tm_mix = tm_row = _row_tile(T, ROW_TILE)
    tm_exp = EXPERT_TILE
    n_sorted = -(-(T + N_CLASSES * (tm_exp - 1)) // tm_exp) * tm_exp
    n_tiles = n_sorted // tm_exp

    head = jnp.concatenate([jnp.zeros((PAD, D), F32), meta_tokens.astype(F32)], axis=0)
    lp = np.arange(T) % LP
    ys = x.astype(F32).reshape(B * S, D)
    pos = jnp.asarray(np.where(lp >= CHUNK, (np.arange(T) // LP) * S + lp - CHUNK, 0), jnp.int32)

    c_q = 0
    c_gl = 2 * key + 2 * val
    c_u = c_gl + rank

    w_a = w_in[:, :, c_q:c_gl].astype(BF16)
    w_b = w_in[:, :, c_u:].astype(BF16)
    w_g = jnp.concatenate([w_in[:, :, c_gl:c_u].astype(BF16),
                           jnp.zeros((depth, D, LANES - rank), BF16)], axis=2)
    wg_hi, wg_lo = _split(jnp.concatenate(
        [w_gate_up, jnp.zeros((depth, LANES - rank, key), F32)], axis=1))
    wr = jnp.swapaxes(jnp.concatenate(
        [w_router_group, w_router_expert,
         jnp.zeros((depth, D, CLASS_ROWS - N_GROUPS - N_EXPERTS), F32)], axis=2), 1, 2)
    row_vec = lambda p: p[:, None, :]
    mix_params = (row_vec(gla_norm_w), w_pool_grp.astype(BF16), row_vec(pool_scale),
                  w_br_gla.astype(BF16), w_br_pool.astype(BF16), w_out.astype(BF16),
                  row_vec(norm2_w), wr)
    weg, weu, wed = w_exp_gate.astype(BF16), w_exp_up.astype(BF16), w_exp_down.astype(BF16)

    for l in range(depth):
        z, zg, h = _inproj_gather(ys, pos, head, row_vec(norm1_w), w_a, w_b, w_g, l, tm=tm_proj,
                                  seq_rows=LP if l == 0 else 0, n_batch=B)
        ogla = _gla(z, zg, wg_hi, wg_lo, row_vec(b_gate), l, batch=B, n_chunks=n_chunks, dk=dk, dv=dv)
        hx, info, counts = _mix(ogla, z, h, mix_params, l, tm=tm_mix, seq_rows=LP)

        cnt = counts[:N_CLASSES, 0].astype(jnp.int32)
        seg_tiles = (cnt + tm_exp - 1) // tm_exp
        tile_end = jnp.cumsum(seg_tiles)
        start = (tile_end - seg_tiles) * tm_exp
        n_used = tile_end[-1:]
        tile_cls = jnp.minimum(
            jnp.sum(jnp.arange(n_tiles)[:, None] >= tile_end[None, :], axis=1), N_CLASSES - 1)
        tile_cls = jnp.where(jnp.arange(n_tiles) < n_used[0], tile_cls,
                             jnp.take(tile_cls, jnp.maximum(n_used[0] - 1, 0)))
        pair_lo = jnp.array([p[0] for p in PAIRS], jnp.int32)
        pair_hi = jnp.array([p[1] for p in PAIRS], jnp.int32)
        grp = tile_cls // len(PAIRS)
        tile_ea = (grp * EXPERTS_PER_GROUP + jnp.take(pair_lo, tile_cls % len(PAIRS))).astype(jnp.int32)
        tile_eb = (grp * EXPERTS_PER_GROUP + jnp.take(pair_hi, tile_cls % len(PAIRS))).astype(jnp.int32)
        cls_t = info[0].astype(jnp.int32)
        pos = info[1].astype(jnp.int32) + jnp.sum(
            jnp.where(cls_t[:, None] == jnp.arange(N_CLASSES)[None, :], start[None, :], 0), axis=1)

        xs = _dispatch(hx, info, pos, tile_end.astype(jnp.int32), n_used.astype(jnp.int32),
                       tm=tm_row, tm_exp=tm_exp, n_sorted=n_sorted)
        ys = _moe(xs, tile_ea, tile_eb, n_used.astype(jnp.int32), row_vec(norm2_w), weg, weu, wed, l,
                  tm=tm_exp)

    pos_out = pos.reshape(B, LP)[:, PAD + N_META:].reshape(B * S)
    out = _collect_norm(ys, pos_out, final_norm_w[None], tm=_row_tile(B * S, ROW_TILE))
    return out.reshape(B, S, D)
```

```python
import functools

import jax
import jax.numpy as jnp
import numpy as np
from jax import lax
from jax.experimental import pallas as pl
from jax.experimental.pallas import tpu as pltpu

F32 = jnp.float32
BF16 = jnp.bfloat16

EPS = 1e-6
CHUNK = 64
N_META = 16
PAD = CHUNK - N_META
GLA_HEADS = 4
GATE_NORMALIZER = 16.0
POOL_WINDOWS = (2, 4, 8, 16)
N_GROUPS = 4
EXPERTS_PER_GROUP = 4
N_EXPERTS = N_GROUPS * EXPERTS_PER_GROUP
PAIRS = ((0, 1), (0, 2), (0, 3), (1, 2), (1, 3), (2, 3))
N_CLASSES = N_GROUPS * len(PAIRS)
CLASS_ROWS = 32
LANES = 128
HALO = 32
ROUTE_LANES = 128
DMA_THREADS = 2
MIX_SUBTILES = 1

VMEM_LIMIT = 56 * 1024 * 1024
ROW_TILE = 512
COPY_TILE = 1024
EXPERT_TILE = 512
PROJ_COL_BLOCK = 512


def _cparams(sem):
    return pltpu.CompilerParams(dimension_semantics=sem, vmem_limit_bytes=VMEM_LIMIT)


def _resident(shape):
    nd = len(shape)
    return pl.BlockSpec(shape, lambda *_: (0,) * nd, pipeline_mode=pl.Buffered(1))


def _layer_block(arr, layer):
    nd = arr.ndim
    return pl.BlockSpec((None,) + arr.shape[1:], lambda *_: (layer,) + (0,) * (nd - 1),
                        pipeline_mode=pl.Buffered(1))


def _split(x):
    hi = x.astype(BF16)
    lo = (x - hi.astype(F32)).astype(BF16)
    return hi, lo


def _dot(a, b):
    return jnp.dot(a, b, preferred_element_type=F32)


def _dot_nt(a, b):
    return lax.dot_general(a, b, (((1,), (1,)), ((), ())), preferred_element_type=F32)


def _dot_tn(a, b):
    return lax.dot_general(a, b, (((0,), (0,)), ((), ())), preferred_element_type=F32)


def _sigmoid(x):
    return 0.5 * jnp.tanh(0.5 * x) + 0.5


def _row_gather(src_ref, pos_ref, dst_ref, sem, tm):
    for r in range(tm):
        pltpu.make_async_copy(src_ref.at[pl.ds(pos_ref[r], 1)], dst_ref.at[pl.ds(r, 1)],
                              sem).start(priority=r % DMA_THREADS)


def _row_gather_wait(src_ref, dst_ref, sem, tm):
    pltpu.make_async_copy(src_ref.at[pl.ds(0, tm)], dst_ref, sem).wait()


def _inproj_gather_kernel(pos0_ref, posn_ref, ys_ref, head_ref, nw_ref, wa_ref, wb_ref, wg_ref,
                          z_ref, zg_ref, h_ref, buf_ref, xn_ref, sem,
                          *, tm, col_blk, seq_rows, n_batch):
    i = pl.program_id(0)
    last = pl.num_programs(0) - 1
    slot = i % 2

    @pl.when(i == 0)
    def _():
        _row_gather(ys_ref, pos0_ref, buf_ref.at[0], sem.at[0], tm)

    _row_gather_wait(ys_ref, buf_ref.at[slot], sem.at[slot], tm)
    if seq_rows:
        b0 = (i * tm + seq_rows - 1) // seq_rows
        for k in range(tm // seq_rows + 1):
            off = (b0 + k) * seq_rows - i * tm

            @pl.when(jnp.logical_and(off < tm, b0 + k < n_batch))
            def _():
                buf_ref[slot, pl.ds(pl.multiple_of(off, CHUNK), CHUNK), :] = head_ref[...]

    x = buf_ref[slot]
    h_ref[...] = x
    ms = jnp.mean(x * x, axis=-1, keepdims=True)
    xn_ref[...] = ((x * lax.rsqrt(ms + EPS)) * nw_ref[...]).astype(BF16)
    _row_gather(ys_ref, posn_ref, buf_ref.at[1 - slot], sem.at[1 - slot], tm)
    col = 0
    for w_ref in (wa_ref, wb_ref):
        for j in range(w_ref.shape[1] // col_blk):
            sl = slice(j * col_blk, (j + 1) * col_blk)
            z_ref[:, col:col + col_blk] = _dot(xn_ref[...], w_ref[:, sl]).astype(BF16)
            col += col_blk
    zg_ref[...] = _dot(xn_ref[...], wg_ref[...])

    @pl.when(i == last)
    def _():
        _row_gather_wait(ys_ref, buf_ref.at[1 - slot], sem.at[1 - slot], tm)


def _inproj_gather(ys, pos, head, norm_w, w_a, w_b, w_g, layer, *, tm, seq_rows=0, n_batch=0):
    T = pos.shape[0]
    D = ys.shape[1]
    n_main = w_a.shape[2] + w_b.shape[2]
    n_all = n_main + w_g.shape[2]
    n_steps = T // tm
    assert tm % CHUNK == 0 and seq_rows % CHUNK == 0
    return pl.pallas_call(
        functools.partial(_inproj_gather_kernel, tm=tm, col_blk=PROJ_COL_BLOCK,
                          seq_rows=seq_rows, n_batch=n_batch),
        grid=(n_steps,),
        in_specs=[pl.BlockSpec((tm,), lambda i: (0,), memory_space=pltpu.SMEM),
                  pl.BlockSpec((tm,), lambda i: (jnp.minimum(i + 1, n_steps - 1),),
                               memory_space=pltpu.SMEM),
                  pl.BlockSpec(memory_space=pl.ANY),
                  _resident(head.shape),
                  _layer_block(norm_w, layer), _layer_block(w_a, layer),
                  _layer_block(w_b, layer), _layer_block(w_g, layer)],
        out_specs=[pl.BlockSpec((tm, n_main), lambda i: (i, 0)),
                   pl.BlockSpec((tm, n_all - n_main), lambda i: (i, 0)),
                   pl.BlockSpec((tm, D), lambda i: (i, 0))],
        out_shape=[jax.ShapeDtypeStruct((T, n_main), BF16),
                   jax.ShapeDtypeStruct((T, n_all - n_main), F32),
                   jax.ShapeDtypeStruct((T, D), F32)],
        scratch_shapes=[pltpu.VMEM((2, tm, D), F32), pltpu.VMEM((tm, D), BF16),
                        pltpu.SemaphoreType.DMA((2,))],
        compiler_params=_cparams(("arbitrary",)),
        name="inproj_gather",
    )(pos, pos, ys, head, norm_w, w_a, w_b, w_g)


def _gla_prep(q_ref, k_ref, zg_ref, wgh_ref, wgl_ref, bg_ref, scaled_ref, dec_ref, slot, first,
              *, bb, dk):
    rows = bb * CHUNK
    key = GLA_HEADS * dk
    gl = zg_ref[...].reshape(rows, LANES).astype(BF16)
    gpre = _dot(gl, wgh_ref[...]) + _dot(gl, wgl_ref[...]) + bg_ref[...]
    g = (jnp.minimum(gpre, 0.0) - jnp.log(1.0 + jnp.exp(-jnp.abs(gpre)))) * (1.0 / GATE_NORMALIZER)
    if first:
        row = lax.broadcasted_iota(jnp.int32, (rows, 1), 0)
        g = jnp.where((row & (CHUNK - 1)) < PAD, 0.0, g)

    ri = lax.broadcasted_iota(jnp.int32, (rows, rows), 0)
    ci = lax.broadcasted_iota(jnp.int32, (rows, rows), 1)
    tri = jnp.where(jnp.logical_and((ri // CHUNK) == (ci // CHUNK), ri >= ci), 1.0, 0.0).astype(BF16)
    g_hi, g_lo = _split(g)
    bcum = _dot(tri, g_hi) + _dot(tri, g_lo)
    gam = jnp.concatenate(
        [jnp.broadcast_to(bcum[(c + 1) * CHUNK - 1:(c + 1) * CHUNK, :], (CHUNK, key)) for c in range(bb)],
        axis=0)
    eb = jnp.exp(bcum)
    ieb = jnp.exp(-bcum)
    q = q_ref[...].reshape(rows, key).astype(F32) * (dk ** -0.5)
    k = k_ref[...].reshape(rows, key).astype(F32)
    scaled_ref[slot, 0] = (q * eb).astype(BF16)
    scaled_ref[slot, 1] = (q * ieb).astype(BF16)
    scaled_ref[slot, 2] = (k * eb).astype(BF16)
    scaled_ref[slot, 3] = (k * ieb).astype(BF16)
    scaled_ref[slot, 4] = (k * jnp.exp(gam - bcum)).astype(BF16)
    for c in range(bb):
        dec_ref[slot, c] = eb[(c + 1) * CHUNK - 1:(c + 1) * CHUNK, :]


def _gla_kernel(q0_ref, k0_ref, zg0_ref, qn_ref, kn_ref, zgn_ref, v_ref, wgh_ref, wgl_ref, bg_ref,
                o_ref, st_ref, scaled_ref, dec_ref, *, bb, dk, dv):
    n = pl.program_id(1)
    prep = functools.partial(_gla_prep, wgh_ref=wgh_ref, wgl_ref=wgl_ref, bg_ref=bg_ref,
                             scaled_ref=scaled_ref, dec_ref=dec_ref, bb=bb, dk=dk)

    @pl.when(n == 0)
    def _():
        st_ref[...] = jnp.zeros_like(st_ref)
        prep(q0_ref, k0_ref, zg0_ref, slot=0, first=True)

    li = lax.broadcasted_iota(jnp.int32, (CHUNK, CHUNK), 0)
    lj = lax.broadcasted_iota(jnp.int32, (CHUNK, CHUNK), 1)
    lower = li >= lj
    groups = [(c, hd) for c in range(bb) for hd in range(GLA_HEADS)]

    def step(slot):
        prep(qn_ref, kn_ref, zgn_ref, slot=1 - slot, first=False)
        attn = {}
        for c, hd in groups:
            rs = slice(c * CHUNK, (c + 1) * CHUNK)
            ks = slice(hd * dk, (hd + 1) * dk)
            a_lo = _dot_nt(scaled_ref[slot, 0, rs, ks], scaled_ref[slot, 3, rs, ks])
            a_up = _dot_nt(scaled_ref[slot, 1, rs, ks], scaled_ref[slot, 2, rs, ks])
            attn[c, hd] = jnp.where(lower, a_lo, a_up).astype(BF16)
        for c, hd in groups:
            rs = slice(c * CHUNK, (c + 1) * CHUNK)
            ks = slice(hd * dk, (hd + 1) * dk)
            vs = slice(hd * dv, (hd + 1) * dv)
            o_ref[c, :, vs] = (_dot(attn[c, hd], v_ref[c, :, vs])
                               + _dot(scaled_ref[slot, 0, rs, ks], st_ref[c, hd].astype(BF16))
                               ).astype(BF16)
        for c, hd in groups:
            rs = slice(c * CHUNK, (c + 1) * CHUNK)
            ks = slice(hd * dk, (hd + 1) * dk)
            vs = slice(hd * dv, (hd + 1) * dv)
            dec = jnp.broadcast_to(dec_ref[slot, c, :, ks], (dk, dk)).T
            st_ref[c, hd] = (st_ref[c, hd] * jnp.concatenate([dec] * (dv // dk), axis=1)
                             + _dot_tn(scaled_ref[slot, 4, rs, ks], v_ref[c, :, vs]))

    for slot in (0, 1):
        pl.when(n % 2 == slot)(functools.partial(step, slot))


def _gla(z, zg, wg_hi, wg_lo, bg, layer, *, batch, n_chunks, dk, dv):
    T = z.shape[0]
    key = GLA_HEADS * dk
    val = GLA_HEADS * dv
    bb = 8 if batch % 8 == 0 else (4 if batch % 4 == 0 else 1)
    z3 = z.reshape(batch, n_chunks * CHUNK, z.shape[1])
    zg3 = zg.reshape(batch, n_chunks * CHUNK, zg.shape[1])
    nxt = lambda n: jnp.minimum(n + 1, n_chunks - 1)
    o = pl.pallas_call(
        functools.partial(_gla_kernel, bb=bb, dk=dk, dv=dv),
        grid=(batch // bb, n_chunks),
        in_specs=[pl.BlockSpec((bb, CHUNK, key), lambda b, n: (b, 0, 0)),
                  pl.BlockSpec((bb, CHUNK, key), lambda b, n: (b, 0, 1)),
                  pl.BlockSpec((bb, CHUNK, LANES), lambda b, n: (b, 0, 0)),
                  pl.BlockSpec((bb, CHUNK, key), lambda b, n: (b, nxt(n), 0)),
                  pl.BlockSpec((bb, CHUNK, key), lambda b, n: (b, nxt(n), 1)),
                  pl.BlockSpec((bb, CHUNK, LANES), lambda b, n: (b, nxt(n), 0)),
                  pl.BlockSpec((bb, CHUNK, val), lambda b, n: (b, n, 1)),
                  _layer_block(wg_hi, layer), _layer_block(wg_lo, layer), _layer_block(bg, layer)],
        out_specs=pl.BlockSpec((bb, CHUNK, val), lambda b, n: (b, n, 0)),
        out_shape=jax.ShapeDtypeStruct((batch, n_chunks * CHUNK, val), BF16),
        scratch_shapes=[pltpu.VMEM((bb, GLA_HEADS, dk, dv), F32),
                        pltpu.VMEM((2, 5, bb * CHUNK, key), BF16),
                        pltpu.VMEM((2, bb, 1, key), F32)],
        compiler_params=_cparams(("parallel", "arbitrary")),
        name="gla",
    )(z3, z3, zg3, z3, z3, zg3, z3, wg_hi, wg_lo, bg)
    return o.reshape(T, val)


def _mix_kernel(ogla_ref, og_ref, u_ref, halo_ref, g0_ref, g1_ref, h_ref, gnw_ref,
                wgrp_ref, pscale_ref, wbg_ref, wbp_ref, wout_ref, n2_ref, wr_ref,
                hx_ref, info_ref, cnt_ref,
                ue_ref, p2_ref, p4_ref, p8_ref, p16_ref, tri_ref, carry_ref,
                *, tm, seq_rows, d_model):
    i = pl.program_id(0)
    gd = d_model // len(POOL_WINDOWS)

    @pl.when(i == 0)
    def _():
        r = lax.broadcasted_iota(jnp.int32, (tm, tm), 0)
        c = lax.broadcasted_iota(jnp.int32, (tm, tm), 1)
        tri_ref[...] = jnp.where(r < c, 1.0, 0.0).astype(BF16)
        carry_ref[...] = jnp.zeros_like(carry_ref)

    ue_ref[0:HALO, :] = halo_ref[...].astype(F32)
    ue_ref[HALO:, :] = u_ref[...].astype(F32)
    n8 = tm + HALO - 8
    p2_ref[8:, :] = ue_ref[8:, :] + ue_ref[pl.ds(7, n8), :]
    p4_ref[16:, :] = p2_ref[16:, gd:] + p2_ref[pl.ds(14, n8 - 8), gd:]
    p8_ref[24:, :] = p4_ref[24:, gd:] + p4_ref[pl.ds(20, n8 - 16), gd:]
    p16_ref[HALO:, :] = p8_ref[HALO:, gd:] + p8_ref[pl.ds(24, tm), gd:]
    sum_refs = (p2_ref, p4_ref, p8_ref, p16_ref)

    sub = tm // MIX_SUBTILES
    for part in range(MIX_SUBTILES):
        _mix_rows(part * sub, sub, i * tm, ogla_ref, og_ref, g0_ref, g1_ref, h_ref, gnw_ref, wgrp_ref,
                  pscale_ref, wbg_ref, wbp_ref, wout_ref, n2_ref, wr_ref, hx_ref, info_ref,
                  ue_ref, sum_refs, tri_ref, carry_ref, seq_rows=seq_rows, d_model=d_model)
    cnt_ref[...] = carry_ref[...]


def _mix_rows(r0, rows, tile_row0, ogla_ref, og_ref, g0_ref, g1_ref, h_ref, gnw_ref, wgrp_ref,
              pscale_ref, wbg_ref, wbp_ref, wout_ref, n2_ref, wr_ref, hx_ref, info_ref,
              ue_ref, sum_refs, tri_ref, carry_ref, *, seq_rows, d_model):
    gd = d_model // len(POOL_WINDOWS)
    rs = pl.ds(r0, rows)
    rh = pl.ds(r0 + HALO, rows)

    rowf = (tile_row0 + r0 + lax.broadcasted_iota(jnp.int32, (rows, 1), 0)).astype(F32)
    lp = rowf - jnp.floor(rowf / float(seq_rows)) * float(seq_rows)
    is_pad = lp < float(PAD)
    seqpos1 = lp - float(PAD - 1)

    br_p = None
    for gi, w in enumerate(POOL_WINDOWS):
        cs = slice(gi * gd, (gi + 1) * gd)
        inv_cnt = 1.0 / jnp.clip(seqpos1, 1.0, float(w))
        pooled = sum_refs[gi][rh, 0:gd] * inv_cnt - ue_ref[rh, cs]
        mixed = _dot(pooled.astype(BF16), wgrp_ref[gi]) * pscale_ref[:, cs]
        part = _dot(mixed.astype(BF16), wbp_ref[cs, :])
        br_p = part if br_p is None else br_p + part

    dv = gnw_ref.shape[1]
    br_g = None
    for hd in range(GLA_HEADS):
        vs = slice(hd * dv, (hd + 1) * dv)
        o = ogla_ref[rs, vs].astype(F32)
        o = o * lax.rsqrt(jnp.mean(o * o, axis=-1, keepdims=True) + EPS) * gnw_ref[...]
        og = og_ref[rs, vs]
        ygla = o.astype(BF16) * (og * _sigmoid(og))
        part = _dot(ygla, wbg_ref[vs, :])
        br_g = part if br_g is None else br_g + part
    merged = (_sigmoid(g0_ref[rs, :]) * br_g.astype(BF16) + _sigmoid(g1_ref[rs, :]) * br_p.astype(BF16))
    delta = _dot(merged, wout_ref[...])
    h_new = h_ref[rs, :] + jnp.where(is_pad, 0.0, delta)
    hx_ref[rs, :] = h_new

    ms = jnp.mean(h_new * h_new, axis=-1, keepdims=True)
    xn = (h_new * lax.rsqrt(ms + EPS)) * n2_ref[...]
    xb = xn.astype(BF16)
    w_hi, w_lo = _split(wr_ref[...])
    lt = _dot_nt(w_hi, xb) + _dot_nt(w_lo, xb)

    lg = [lt[j:j + 1, :] for j in range(N_GROUPS)]
    m = jnp.maximum(jnp.maximum(lg[0], lg[1]), jnp.maximum(lg[2], lg[3]))
    gidx = jnp.where(lg[0] == m, 0, jnp.where(lg[1] == m, 1, jnp.where(lg[2] == m, 2, 3)))
    pg_top = 1.0 / (jnp.exp(lg[0] - m) + jnp.exp(lg[1] - m) + jnp.exp(lg[2] - m) + jnp.exp(lg[3] - m))
    sel = []
    for e in range(EXPERTS_PER_GROUP):
        cand = [lt[N_GROUPS + g * EXPERTS_PER_GROUP + e:N_GROUPS + g * EXPERTS_PER_GROUP + e + 1, :]
                for g in range(N_GROUPS)]
        sel.append(jnp.where(gidx == 0, cand[0], jnp.where(gidx == 1, cand[1],
                                                           jnp.where(gidx == 2, cand[2], cand[3]))))
    m1 = jnp.maximum(jnp.maximum(sel[0], sel[1]), jnp.maximum(sel[2], sel[3]))
    i1 = jnp.where(sel[0] == m1, 0, jnp.where(sel[1] == m1, 1, jnp.where(sel[2] == m1, 2, 3)))
    neg = jnp.float32(-jnp.inf)
    rest = [jnp.where(i1 == e, neg, sel[e]) for e in range(EXPERTS_PER_GROUP)]
    m2 = jnp.maximum(jnp.maximum(rest[0], rest[1]), jnp.maximum(rest[2], rest[3]))
    i2 = jnp.where(rest[0] == m2, 0, jnp.where(rest[1] == m2, 1, jnp.where(rest[2] == m2, 2, 3)))
    r21 = jnp.exp(m2 - m1)
    w1 = pg_top / (1.0 + r21)
    w2 = pg_top * r21 / (1.0 + r21)
    lo = jnp.minimum(i1, i2)
    hi = jnp.maximum(i1, i2)
    w_lo = jnp.where(i1 < i2, w1, w2)
    w_hi = jnp.where(i1 < i2, w2, w1)
    pidx = jnp.where(lo == 0, hi - 1, jnp.where(lo == 1, hi + 1, 5))
    cls = gidx * len(PAIRS) + pidx

    crow = lax.broadcasted_iota(jnp.int32, (CLASS_ROWS, rows), 0)
    onehot = jnp.where(crow == cls, 1.0, 0.0)
    prefix = _dot(onehot.astype(BF16), tri_ref[0:rows, 0:rows]) + carry_ref[:, 0:1]
    rank = jnp.sum(onehot * prefix, axis=0, keepdims=True)
    carry_ref[...] = carry_ref[...] + jnp.sum(onehot, axis=1, keepdims=True)

    irow = lax.broadcasted_iota(jnp.int32, (8, rows), 0)
    info_ref[:, rs] = jnp.where(irow == 0, cls.astype(F32), jnp.where(
        irow == 1, rank, jnp.where(irow == 2, w_lo, jnp.where(irow == 3, w_hi, 0.0))))


def _mix(ogla, z, h, layer_params, layer, *, tm, seq_rows):
    T, D = h.shape
    nblk = tm // HALO
    return pl.pallas_call(
        functools.partial(_mix_kernel, tm=tm, seq_rows=seq_rows, d_model=D),
        grid=(T // tm,),
        in_specs=[pl.BlockSpec((tm, D), lambda i: (i, 0)),
                  pl.BlockSpec((tm, D), lambda i: (i, 2)),
                  pl.BlockSpec((tm, D), lambda i: (i, 3)),
                  pl.BlockSpec((HALO, D), lambda i: (jnp.maximum(i * nblk - 1, 0), 3)),
                  pl.BlockSpec((tm, D), lambda i: (i, 4)),
                  pl.BlockSpec((tm, D), lambda i: (i, 5)),
                  pl.BlockSpec((tm, D), lambda i: (i, 0))]
                 + [_layer_block(p, layer) for p in layer_params],
        out_specs=[pl.BlockSpec((tm, D), lambda i: (i, 0)),
                   pl.BlockSpec((8, tm), lambda i: (0, i)),
                   pl.BlockSpec((CLASS_ROWS, LANES), lambda i: (0, 0))],
        out_shape=[jax.ShapeDtypeStruct((T, D), F32),
                   jax.ShapeDtypeStruct((8, T), F32),
                   jax.ShapeDtypeStruct((CLASS_ROWS, LANES), F32)],
        scratch_shapes=[pltpu.VMEM((tm + HALO, D), F32),
                        pltpu.VMEM((tm + HALO, D), F32),
                        pltpu.VMEM((tm + HALO, D - D // 4), F32),
                        pltpu.VMEM((tm + HALO, D - 2 * (D // 4)), F32),
                        pltpu.VMEM((tm + HALO, D // 4), F32),
                        pltpu.VMEM((tm, tm), BF16),
                        pltpu.VMEM((CLASS_ROWS, LANES), F32)],
        compiler_params=_cparams(("arbitrary",)),
        name="mix",
    )(ogla, z, z, z, z, z, h, *layer_params)


def _dispatch_kernel(tend_ref, nu_ref, pos_ref, h_ref, info_ref, xs_ref, row_ref, zero_ref, sem, zsem,
                     *, tm, tm_exp, n_tiles, d_model):
    def fill(j):
        return pltpu.make_async_copy(zero_ref, xs_ref.at[pl.ds(j * tm_exp, tm_exp)], zsem)

    @pl.when(pl.program_id(0) == 0)
    def _():
        zero_ref[...] = jnp.zeros_like(zero_ref)
        for wait in (False, True):
            for c in range(N_CLASSES):
                prev = tend_ref[c - 1] if c else 0

                @pl.when(tend_ref[c] > prev)
                def _():
                    cp = fill(tend_ref[c] - 1)
                    cp.wait() if wait else cp.start()

            def tail(j, carry):
                cp = fill(j)
                cp.wait() if wait else cp.start()
                return carry
            lax.fori_loop(nu_ref[0], n_tiles, tail, 0)

    i = pl.program_id(0)
    slot = i % 2
    rows = row_ref.at[slot]
    rows[:, 0:d_model] = h_ref[...]
    lane = lax.broadcasted_iota(jnp.int32, (ROUTE_LANES, tm), 0)
    rows[:, d_model:] = jnp.where(lane == 0, info_ref[2:3, :],
                                  jnp.where(lane == 1, info_ref[3:4, :], 0.0)).T
    for r in range(tm):
        pltpu.make_async_copy(rows.at[pl.ds(r, 1)], xs_ref.at[pl.ds(pos_ref[r], 1)],
                              sem.at[slot]).start(priority=r % DMA_THREADS)

    def wait_all(s):
        pltpu.make_async_copy(row_ref.at[s], xs_ref.at[pl.ds(0, tm)], sem.at[s]).wait()

    @pl.when(i > 0)
    def _():
        wait_all(1 - slot)

    @pl.when(i == pl.num_programs(0) - 1)
    def _():
        wait_all(slot)


def _dispatch(h, info, pos, tile_end, n_used, *, tm, tm_exp, n_sorted):
    T, D = h.shape
    W = D + ROUTE_LANES
    grid_spec = pltpu.PrefetchScalarGridSpec(
        num_scalar_prefetch=2,
        grid=(T // tm,),
        in_specs=[pl.BlockSpec((tm,), lambda i, te, nu: (i,), memory_space=pltpu.SMEM),
                  pl.BlockSpec((tm, D), lambda i, te, nu: (i, 0)),
                  pl.BlockSpec((8, tm), lambda i, te, nu: (0, i))],
        out_specs=pl.BlockSpec(memory_space=pl.ANY),
        scratch_shapes=[pltpu.VMEM((2, tm, W), F32), pltpu.VMEM((tm_exp, W), F32),
                        pltpu.SemaphoreType.DMA((2,)), pltpu.SemaphoreType.DMA(())],
    )
    return pl.pallas_call(
        functools.partial(_dispatch_kernel, tm=tm, tm_exp=tm_exp, n_tiles=n_sorted // tm_exp,
                          d_model=D),
        grid_spec=grid_spec,
        out_shape=jax.ShapeDtypeStruct((n_sorted, W), F32),
        compiler_params=_cparams(("arbitrary",)),
        name="dispatch",
    )(tile_end, n_used, pos, h, info)


def _collect_norm_kernel(pos0_ref, posn_ref, ys_ref, nw_ref, o_ref, buf_ref, sem, *, tm):
    i = pl.program_id(0)
    slot = i % 2

    @pl.when(i == 0)
    def _():
        _row_gather(ys_ref, pos0_ref, buf_ref.at[0], sem.at[0], tm)

    _row_gather(ys_ref, posn_ref, buf_ref.at[1 - slot], sem.at[1 - slot], tm)
    _row_gather_wait(ys_ref, buf_ref.at[slot], sem.at[slot], tm)
    x = buf_ref[slot]
    ms = jnp.mean(x * x, axis=-1, keepdims=True)
    o_ref[...] = (x * lax.rsqrt(ms + EPS)) * nw_ref[...]

    @pl.when(i == pl.num_programs(0) - 1)
    def _():
        _row_gather_wait(ys_ref, buf_ref.at[1 - slot], sem.at[1 - slot], tm)


def _collect_norm(ys, pos, norm_w, *, tm):
    n_out = pos.shape[0]
    D = ys.shape[1]
    n_steps = n_out // tm
    return pl.pallas_call(
        functools.partial(_collect_norm_kernel, tm=tm),
        grid=(n_steps,),
        in_specs=[pl.BlockSpec((tm,), lambda i: (0,), memory_space=pltpu.SMEM),
                  pl.BlockSpec((tm,), lambda i: (jnp.minimum(i + 1, n_steps - 1),),
                               memory_space=pltpu.SMEM),
                  pl.BlockSpec(memory_space=pl.ANY),
                  _resident(norm_w.shape)],
        out_specs=pl.BlockSpec((tm, D), lambda i: (i, 0)),
        out_shape=jax.ShapeDtypeStruct((n_out, D), F32),
        scratch_shapes=[pltpu.VMEM((2, tm, D), F32), pltpu.SemaphoreType.DMA((2,))],
        compiler_params=_cparams(("arbitrary",)),
        name="collect_norm",
    )(pos, pos, ys, norm_w)


def _moe_kernel(ea_ref, eb_ref, nu_ref, xs_ref, n2_ref, wga_ref, wua_ref, wda_ref,
                wgb_ref, wub_ref, wdb_ref, ys_ref, *, d_model):
    i = pl.program_id(0)

    @pl.when(i < nu_ref[0])
    def _():
        x = xs_ref[:, 0:d_model]
        ms = jnp.mean(x * x, axis=-1, keepdims=True)
        xn = ((x * lax.rsqrt(ms + EPS)) * n2_ref[...]).astype(BF16)
        experts = ((wga_ref, wua_ref, wda_ref), (wgb_ref, wub_ref, wdb_ref))
        pre = [(_dot(xn, wg[...]), _dot(xn, wu[...])) for wg, wu, _ in experts]
        hid = [(gate * _sigmoid(gate) * up).astype(BF16) for gate, up in pre]
        y = x
        for col, (_, _, wd) in enumerate(experts):
            y = y + xs_ref[:, d_model + col:d_model + col + 1] * _dot(hid[col], wd[...])
        ys_ref[...] = y

    @pl.when(i >= nu_ref[0])
    def _():
        ys_ref[...] = jnp.zeros_like(ys_ref)


def _moe(xs, tile_ea, tile_eb, n_used, n2, weg, weu, wed, layer, *, tm):
    n_sorted, W = xs.shape
    D = n2.shape[2]
    de = weg.shape[3]
    row = lambda i, ea, eb, nu: (jnp.minimum(i, nu[0] - 1), 0)
    wa = lambda i, ea, eb, nu: (layer, ea[i], 0, 0)
    wb = lambda i, ea, eb, nu: (layer, eb[i], 0, 0)
    grid_spec = pltpu.PrefetchScalarGridSpec(
        num_scalar_prefetch=3,
        grid=(n_sorted // tm,),
        in_specs=[pl.BlockSpec((tm, W), row),
                  pl.BlockSpec((None, 1, D), lambda i, ea, eb, nu: (layer, 0, 0)),
                  pl.BlockSpec((None, None, D, de), wa), pl.BlockSpec((None, None, D, de), wa),
                  pl.BlockSpec((None, None, de, D), wa),
                  pl.BlockSpec((None, None, D, de), wb), pl.BlockSpec((None, None, D, de), wb),
                  pl.BlockSpec((None, None, de, D), wb)],
        out_specs=pl.BlockSpec((tm, D), lambda i, ea, eb, nu: (i, 0)),
    )
    return pl.pallas_call(
        functools.partial(_moe_kernel, d_model=D),
        grid_spec=grid_spec,
        out_shape=jax.ShapeDtypeStruct((n_sorted, D), F32),
        compiler_params=_cparams(("arbitrary",)),
        name="experts",
    )(tile_ea, tile_eb, n_used, xs, n2, weg, weu, wed, weg, weu, wed)


def _row_tile(T, cap):
    t = cap
    while T % t:
        t //= 2
    return t


def kernel(x, meta_tokens, norm1_w, w_in, w_gate_up, b_gate, gla_norm_w, w_pool_grp, pool_scale,
           w_br_gla, w_br_pool, w_out, norm2_w, w_router_group, w_router_expert, w_exp_gate,
           w_exp_up, w_exp_down, final_norm_w):
    B, S, D = x.shape
    depth = w_in.shape[0]
    key = w_gate_up.shape[2]
    rank = w_gate_up.shape[1]
    dv = gla_norm_w.shape[1]
    val = GLA_HEADS * dv
    dk = key // GLA_HEADS
    assert S % CHUNK == 0 and key * 2 == D and val == D
    assert w_pool_grp.shape[2] * len(POOL_WINDOWS) == D
    LP = PAD + N_META + S
    n_chunks = LP // CHUNK
    T = B * LP
    tm_proj = tm_mix = _row_tile(T, ROW_TILE)
    tm_row = _row_tile(T, COPY_TILE)
    tm_exp = EXPERT_TILE
    n_sorted = -(-(T + N_CLASSES * (tm_exp - 1)) // tm_exp) * tm_exp
    n_tiles = n_sorted // tm_exp

    head = jnp.concatenate([jnp.zeros((PAD, D), F32), meta_tokens.astype(F32)], axis=0)
    lp = np.arange(T) % LP
    ys = x.astype(F32).reshape(B * S, D)
    pos = jnp.asarray(np.where(lp >= CHUNK, (np.arange(T) // LP) * S + lp - CHUNK, 0), jnp.int32)

    c_q = 0
    c_gl = 2 * key + 2 * val
    c_u = c_gl + rank

    w_a = w_in[:, :, c_q:c_gl].astype(BF16)
    w_b = w_in[:, :, c_u:].astype(BF16)
    w_g = jnp.concatenate([w_in[:, :, c_gl:c_u].astype(BF16),
                           jnp.zeros((depth, D, LANES - rank), BF16)], axis=2)
    wg_hi, wg_lo = _split(jnp.concatenate(
        [w_gate_up, jnp.zeros((depth, LANES - rank, key), F32)], axis=1))
    wr = jnp.swapaxes(jnp.concatenate(
        [w_router_group, w_router_expert,
         jnp.zeros((depth, D, CLASS_ROWS - N_GROUPS - N_EXPERTS), F32)], axis=2), 1, 2)
    row_vec = lambda p: p[:, None, :]
    mix_params = (row_vec(gla_norm_w), w_pool_grp.astype(BF16), row_vec(pool_scale),
                  w_br_gla.astype(BF16), w_br_pool.astype(BF16), w_out.astype(BF16),
                  row_vec(norm2_w), wr)
    weg, weu, wed = w_exp_gate.astype(BF16), w_exp_up.astype(BF16), w_exp_down.astype(BF16)

    for l in range(depth):
        z, zg, h = _inproj_gather(ys, pos, head, row_vec(norm1_w), w_a, w_b, w_g, l, tm=tm_proj,
                                  seq_rows=LP if l == 0 else 0, n_batch=B)
        ogla = _gla(z, zg, wg_hi, wg_lo, row_vec(b_gate), l, batch=B, n_chunks=n_chunks, dk=dk, dv=dv)
        hx, info, counts = _mix(ogla, z, h, mix_params, l, tm=tm_mix, seq_rows=LP)

        cnt = counts[:N_CLASSES, 0].astype(jnp.int32)
        seg_tiles = (cnt + tm_exp - 1) // tm_exp
        tile_end = jnp.cumsum(seg_tiles)
        start = (tile_end - seg_tiles) * tm_exp
        n_used = tile_end[-1:]
        tile_cls = jnp.minimum(
            jnp.sum(jnp.arange(n_tiles)[:, None] >= tile_end[None, :], axis=1), N_CLASSES - 1)
        tile_cls = jnp.where(jnp.arange(n_tiles) < n_used[0], tile_cls,
                             jnp.take(tile_cls, jnp.maximum(n_used[0] - 1, 0)))
        pair_lo = jnp.array([p[0] for p in PAIRS], jnp.int32)
        pair_hi = jnp.array([p[1] for p in PAIRS], jnp.int32)
        grp = tile_cls // len(PAIRS)
        tile_ea = (grp * EXPERTS_PER_GROUP + jnp.take(pair_lo, tile_cls % len(PAIRS))).astype(jnp.int32)
        tile_eb = (grp * EXPERTS_PER_GROUP + jnp.take(pair_hi, tile_cls % len(PAIRS))).astype(jnp.int32)
        cls_t = info[0].astype(jnp.int32)
        pos = info[1].astype(jnp.int32) + jnp.sum(
            jnp.where(cls_t[:, None] == jnp.arange(N_CLASSES)[None, :], start[None, :], 0), axis=1)

        xs = _dispatch(hx, info, pos, tile_end.astype(jnp.int32), n_used.astype(jnp.int32),
                       tm=tm_row, tm_exp=tm_exp, n_sorted=n_sorted)
        ys = _moe(xs, tile_ea, tile_eb, n_used.astype(jnp.int32), row_vec(norm2_w), weg, weu, wed, l,
                  tm=tm_exp)

    pos_out = pos.reshape(B, LP)[:, PAD + N_META:].reshape(B * S)
    out = _collect_norm(ys, pos_out, final_norm_w[None], tm=_row_tile(B * S, COPY_TILE))
    return out.reshape(B, S, D)
```

```python
import functools

import jax
import jax.numpy as jnp
import numpy as np
from jax import lax
from jax.experimental import pallas as pl
from jax.experimental.pallas import tpu as pltpu

F32 = jnp.float32
BF16 = jnp.bfloat16

EPS = 1e-6
CHUNK = 64
N_META = 16
PAD = CHUNK - N_META
GLA_HEADS = 4
GATE_NORMALIZER = 16.0
POOL_WINDOWS = (2, 4, 8, 16)
N_GROUPS = 4
EXPERTS_PER_GROUP = 4
N_EXPERTS = N_GROUPS * EXPERTS_PER_GROUP
PAIRS = ((0, 1), (0, 2), (0, 3), (1, 2), (1, 3), (2, 3))
N_CLASSES = N_GROUPS * len(PAIRS)
CLASS_ROWS = 32
LANES = 128
HALO = 32
ROUTE_LANES = 128
DMA_THREADS = 2

VMEM_LIMIT = 56 * 1024 * 1024
ROW_TILE = 512
COPY_TILE = 1024
EXPERT_TILE = 512
PROJ_COL_BLOCK = 512


def _cparams(sem):
    return pltpu.CompilerParams(dimension_semantics=sem, vmem_limit_bytes=VMEM_LIMIT)


def _resident(shape):
    nd = len(shape)
    return pl.BlockSpec(shape, lambda *_: (0,) * nd, pipeline_mode=pl.Buffered(1))


def _layer_block(arr, layer):
    nd = arr.ndim
    return pl.BlockSpec((None,) + arr.shape[1:], lambda *_: (layer,) + (0,) * (nd - 1),
                        pipeline_mode=pl.Buffered(1))


def _split(x):
    hi = x.astype(BF16)
    lo = (x - hi.astype(F32)).astype(BF16)
    return hi, lo


def _dot(a, b):
    return jnp.dot(a, b, preferred_element_type=F32)


def _dot_nt(a, b):
    return lax.dot_general(a, b, (((1,), (1,)), ((), ())), preferred_element_type=F32)


def _dot_tn(a, b):
    return lax.dot_general(a, b, (((0,), (0,)), ((), ())), preferred_element_type=F32)


def _sigmoid(x):
    return 0.5 * jnp.tanh(0.5 * x) + 0.5


def _row_gather(src_ref, pos_ref, dst_ref, sem, tm):
    for r in range(tm):
        pltpu.make_async_copy(src_ref.at[pl.ds(pos_ref[r], 1)], dst_ref.at[pl.ds(r, 1)],
                              sem).start(priority=r % DMA_THREADS)


def _row_gather_wait(src_ref, dst_ref, sem, tm):
    pltpu.make_async_copy(src_ref.at[pl.ds(0, tm)], dst_ref, sem).wait()


def _inproj_gather_kernel(pos0_ref, posn_ref, ys_ref, head_ref, nw_ref, wa_ref, wb_ref, wg_ref,
                          z_ref, zg_ref, h_ref, buf_ref, xn_ref, sem,
                          *, tm, col_blk, seq_rows, n_batch):
    i = pl.program_id(0)
    last = pl.num_programs(0) - 1
    slot = i % 2

    @pl.when(i == 0)
    def _():
        _row_gather(ys_ref, pos0_ref, buf_ref.at[0], sem.at[0], tm)

    _row_gather_wait(ys_ref, buf_ref.at[slot], sem.at[slot], tm)
    if seq_rows:
        b0 = (i * tm + seq_rows - 1) // seq_rows
        for k in range(tm // seq_rows + 1):
            off = (b0 + k) * seq_rows - i * tm

            @pl.when(jnp.logical_and(off < tm, b0 + k < n_batch))
            def _():
                buf_ref[slot, pl.ds(pl.multiple_of(off, CHUNK), CHUNK), :] = head_ref[...]

    x = buf_ref[slot]
    h_ref[...] = x
    ms = jnp.mean(x * x, axis=-1, keepdims=True)
    xn_ref[...] = ((x * lax.rsqrt(ms + EPS)) * nw_ref[...]).astype(BF16)
    _row_gather(ys_ref, posn_ref, buf_ref.at[1 - slot], sem.at[1 - slot], tm)
    col = 0
    for w_ref in (wa_ref, wb_ref):
        for j in range(w_ref.shape[1] // col_blk):
            sl = slice(j * col_blk, (j + 1) * col_blk)
            z_ref[:, col:col + col_blk] = _dot(xn_ref[...], w_ref[:, sl]).astype(BF16)
            col += col_blk
    zg_ref[...] = _dot(xn_ref[...], wg_ref[...])

    @pl.when(i == last)
    def _():
        _row_gather_wait(ys_ref, buf_ref.at[1 - slot], sem.at[1 - slot], tm)


def _inproj_gather(ys, pos, head, norm_w, w_a, w_b, w_g, layer, *, tm, seq_rows=0, n_batch=0):
    T = pos.shape[0]
    D = ys.shape[1]
    n_main = w_a.shape[2] + w_b.shape[2]
    n_all = n_main + w_g.shape[2]
    n_steps = T // tm
    assert tm % CHUNK == 0 and seq_rows % CHUNK == 0
    return pl.pallas_call(
        functools.partial(_inproj_gather_kernel, tm=tm, col_blk=PROJ_COL_BLOCK,
                          seq_rows=seq_rows, n_batch=n_batch),
        grid=(n_steps,),
        in_specs=[pl.BlockSpec((tm,), lambda i: (0,), memory_space=pltpu.SMEM),
                  pl.BlockSpec((tm,), lambda i: (jnp.minimum(i + 1, n_steps - 1),),
                               memory_space=pltpu.SMEM),
                  pl.BlockSpec(memory_space=pl.ANY),
                  _resident(head.shape),
                  _layer_block(norm_w, layer), _layer_block(w_a, layer),
                  _layer_block(w_b, layer), _layer_block(w_g, layer)],
        out_specs=[pl.BlockSpec((tm, n_main), lambda i: (i, 0)),
                   pl.BlockSpec((tm, n_all - n_main), lambda i: (i, 0)),
                   pl.BlockSpec((tm, D), lambda i: (i, 0))],
        out_shape=[jax.ShapeDtypeStruct((T, n_main), BF16),
                   jax.ShapeDtypeStruct((T, n_all - n_main), F32),
                   jax.ShapeDtypeStruct((T, D), F32)],
        scratch_shapes=[pltpu.VMEM((2, tm, D), F32), pltpu.VMEM((tm, D), BF16),
                        pltpu.SemaphoreType.DMA((2,))],
        compiler_params=_cparams(("arbitrary",)),
        name="inproj_gather",
    )(pos, pos, ys, head, norm_w, w_a, w_b, w_g)


def _gla_prep(q_ref, k_ref, zg_ref, wgh_ref, wgl_ref, bg_ref, scaled_ref, dec_ref, slot, first,
              *, bb, dk):
    rows = bb * CHUNK
    key = GLA_HEADS * dk
    gl = zg_ref[...].reshape(rows, LANES).astype(BF16)
    gpre = _dot(gl, wgh_ref[...]) + _dot(gl, wgl_ref[...]) + bg_ref[...]
    g = (jnp.minimum(gpre, 0.0) - jnp.log(1.0 + jnp.exp(-jnp.abs(gpre)))) * (1.0 / GATE_NORMALIZER)
    if first:
        row = lax.broadcasted_iota(jnp.int32, (rows, 1), 0)
        g = jnp.where((row & (CHUNK - 1)) < PAD, 0.0, g)

    ri = lax.broadcasted_iota(jnp.int32, (rows, rows), 0)
    ci = lax.broadcasted_iota(jnp.int32, (rows, rows), 1)
    tri = jnp.where(jnp.logical_and((ri // CHUNK) == (ci // CHUNK), ri >= ci), 1.0, 0.0).astype(BF16)
    g_hi, g_lo = _split(g)
    bcum = _dot(tri, g_hi) + _dot(tri, g_lo)
    gam = jnp.concatenate(
        [jnp.broadcast_to(bcum[(c + 1) * CHUNK - 1:(c + 1) * CHUNK, :], (CHUNK, key)) for c in range(bb)],
        axis=0)
    eb = jnp.exp(bcum)
    ieb = jnp.exp(-bcum)
    q = q_ref[...].reshape(rows, key).astype(F32) * (dk ** -0.5)
    k = k_ref[...].reshape(rows, key).astype(F32)
    scaled_ref[slot, 0] = (q * eb).astype(BF16)
    scaled_ref[slot, 1] = (q * ieb).astype(BF16)
    scaled_ref[slot, 2] = (k * eb).astype(BF16)
    scaled_ref[slot, 3] = (k * ieb).astype(BF16)
    scaled_ref[slot, 4] = (k * jnp.exp(gam - bcum)).astype(BF16)
    for c in range(bb):
        dec_ref[slot, c] = eb[(c + 1) * CHUNK - 1:(c + 1) * CHUNK, :]


def _gla_kernel(q0_ref, k0_ref, zg0_ref, qn_ref, kn_ref, zgn_ref, v_ref, wgh_ref, wgl_ref, bg_ref,
                o_ref, st_ref, scaled_ref, dec_ref, *, bb, dk, dv):
    n = pl.program_id(1)
    prep = functools.partial(_gla_prep, wgh_ref=wgh_ref, wgl_ref=wgl_ref, bg_ref=bg_ref,
                             scaled_ref=scaled_ref, dec_ref=dec_ref, bb=bb, dk=dk)

    @pl.when(n == 0)
    def _():
        st_ref[...] = jnp.zeros_like(st_ref)
        prep(q0_ref, k0_ref, zg0_ref, slot=0, first=True)

    li = lax.broadcasted_iota(jnp.int32, (CHUNK, CHUNK), 0)
    lj = lax.broadcasted_iota(jnp.int32, (CHUNK, CHUNK), 1)
    lower = li >= lj
    groups = [(c, hd) for c in range(bb) for hd in range(GLA_HEADS)]

    def step(slot):
        prep(qn_ref, kn_ref, zgn_ref, slot=1 - slot, first=False)
        attn = {}
        for c, hd in groups:
            rs = slice(c * CHUNK, (c + 1) * CHUNK)
            ks = slice(hd * dk, (hd + 1) * dk)
            a_lo = _dot_nt(scaled_ref[slot, 0, rs, ks], scaled_ref[slot, 3, rs, ks])
            a_up = _dot_nt(scaled_ref[slot, 1, rs, ks], scaled_ref[slot, 2, rs, ks])
            attn[c, hd] = jnp.where(lower, a_lo, a_up).astype(BF16)
        for c, hd in groups:
            rs = slice(c * CHUNK, (c + 1) * CHUNK)
            ks = slice(hd * dk, (hd + 1) * dk)
            vs = slice(hd * dv, (hd + 1) * dv)
            o_ref[c, :, vs] = (_dot(attn[c, hd], v_ref[c, :, vs])
                               + _dot(scaled_ref[slot, 0, rs, ks], st_ref[c, hd].astype(BF16))
                               ).astype(BF16)
        for c, hd in groups:
            rs = slice(c * CHUNK, (c + 1) * CHUNK)
            ks = slice(hd * dk, (hd + 1) * dk)
            vs = slice(hd * dv, (hd + 1) * dv)
            dec = jnp.broadcast_to(dec_ref[slot, c, :, ks], (dk, dk)).T
            st_ref[c, hd] = (st_ref[c, hd] * jnp.concatenate([dec] * (dv // dk), axis=1)
                             + _dot_tn(scaled_ref[slot, 4, rs, ks], v_ref[c, :, vs]))

    for slot in (0, 1):
        pl.when(n % 2 == slot)(functools.partial(step, slot))


def _gla(z, zg, wg_hi, wg_lo, bg, layer, *, batch, n_chunks, dk, dv):
    T = z.shape[0]
    key = GLA_HEADS * dk
    val = GLA_HEADS * dv
    bb = 8 if batch % 8 == 0 else (4 if batch % 4 == 0 else 1)
    z3 = z.reshape(batch, n_chunks * CHUNK, z.shape[1])
    zg3 = zg.reshape(batch, n_chunks * CHUNK, zg.shape[1])
    nxt = lambda n: jnp.minimum(n + 1, n_chunks - 1)
    o = pl.pallas_call(
        functools.partial(_gla_kernel, bb=bb, dk=dk, dv=dv),
        grid=(batch // bb, n_chunks),
        in_specs=[pl.BlockSpec((bb, CHUNK, key), lambda b, n: (b, 0, 0)),
                  pl.BlockSpec((bb, CHUNK, key), lambda b, n: (b, 0, 1)),
                  pl.BlockSpec((bb, CHUNK, LANES), lambda b, n: (b, 0, 0)),
                  pl.BlockSpec((bb, CHUNK, key), lambda b, n: (b, nxt(n), 0)),
                  pl.BlockSpec((bb, CHUNK, key), lambda b, n: (b, nxt(n), 1)),
                  pl.BlockSpec((bb, CHUNK, LANES), lambda b, n: (b, nxt(n), 0)),
                  pl.BlockSpec((bb, CHUNK, val), lambda b, n: (b, n, 1)),
                  _layer_block(wg_hi, layer), _layer_block(wg_lo, layer), _layer_block(bg, layer)],
        out_specs=pl.BlockSpec((bb, CHUNK, val), lambda b, n: (b, n, 0)),
        out_shape=jax.ShapeDtypeStruct((batch, n_chunks * CHUNK, val), BF16),
        scratch_shapes=[pltpu.VMEM((bb, GLA_HEADS, dk, dv), F32),
                        pltpu.VMEM((2, 5, bb * CHUNK, key), BF16),
                        pltpu.VMEM((2, bb, 1, key), F32)],
        compiler_params=_cparams(("parallel", "arbitrary")),
        name="gla",
    )(z3, z3, zg3, z3, z3, zg3, z3, wg_hi, wg_lo, bg)
    return o.reshape(T, val)


def _mix_kernel(ogla_ref, og_ref, u_ref, halo_ref, g0_ref, g1_ref, h_ref, gnw_ref,
                wgrp_ref, pscale_ref, wbg_ref, wbp_ref, wout_ref, n2_ref, wr_ref,
                hx_ref, info_ref, cnt_ref,
                ue_ref, p2_ref, p4_ref, p8_ref, p16_ref, tri_ref, carry_ref,
                *, tm, seq_rows, d_model):
    i = pl.program_id(0)
    gd = d_model // len(POOL_WINDOWS)

    @pl.when(i == 0)
    def _():
        r = lax.broadcasted_iota(jnp.int32, (tm, tm), 0)
        c = lax.broadcasted_iota(jnp.int32, (tm, tm), 1)
        tri_ref[...] = jnp.where(r < c, 1.0, 0.0).astype(BF16)
        carry_ref[...] = jnp.zeros_like(carry_ref)

    ue_ref[0:HALO, :] = halo_ref[...].astype(F32)
    ue_ref[HALO:, :] = u_ref[...].astype(F32)
    n8 = tm + HALO - 8
    p2_ref[8:, :] = ue_ref[8:, :] + ue_ref[pl.ds(7, n8), :]
    p4_ref[16:, :] = p2_ref[16:, gd:] + p2_ref[pl.ds(14, n8 - 8), gd:]
    p8_ref[24:, :] = p4_ref[24:, gd:] + p4_ref[pl.ds(20, n8 - 16), gd:]
    p16_ref[HALO:, :] = p8_ref[HALO:, gd:] + p8_ref[pl.ds(24, tm), gd:]
    sum_refs = (p2_ref, p4_ref, p8_ref, p16_ref)

    _mix_rows(0, tm, i * tm, ogla_ref, og_ref, g0_ref, g1_ref, h_ref, gnw_ref, wgrp_ref,
              pscale_ref, wbg_ref, wbp_ref, wout_ref, n2_ref, wr_ref, hx_ref, info_ref,
              ue_ref, sum_refs, tri_ref, carry_ref, seq_rows=seq_rows, d_model=d_model)
    cnt_ref[...] = carry_ref[...]


def _mix_rows(r0, rows, tile_row0, ogla_ref, og_ref, g0_ref, g1_ref, h_ref, gnw_ref, wgrp_ref,
              pscale_ref, wbg_ref, wbp_ref, wout_ref, n2_ref, wr_ref, hx_ref, info_ref,
              ue_ref, sum_refs, tri_ref, carry_ref, *, seq_rows, d_model):
    gd = d_model // len(POOL_WINDOWS)
    rs = pl.ds(r0, rows)
    rh = pl.ds(r0 + HALO, rows)

    rowf = (tile_row0 + r0 + lax.broadcasted_iota(jnp.int32, (rows, 1), 0)).astype(F32)
    lp = rowf - jnp.floor(rowf / float(seq_rows)) * float(seq_rows)
    is_pad = lp < float(PAD)
    seqpos1 = lp - float(PAD - 1)

    br_p = None
    for gi, w in enumerate(POOL_WINDOWS):
        cs = slice(gi * gd, (gi + 1) * gd)
        inv_cnt = 1.0 / jnp.clip(seqpos1, 1.0, float(w))
        pooled = sum_refs[gi][rh, 0:gd] * inv_cnt - ue_ref[rh, cs]
        mixed = _dot(pooled.astype(BF16), wgrp_ref[gi]) * pscale_ref[:, cs]
        part = _dot(mixed.astype(BF16), wbp_ref[cs, :])
        br_p = part if br_p is None else br_p + part

    dv = gnw_ref.shape[1]
    br_g = None
    for hd in range(GLA_HEADS):
        vs = slice(hd * dv, (hd + 1) * dv)
        o = ogla_ref[rs, vs].astype(F32)
        o = o * lax.rsqrt(jnp.mean(o * o, axis=-1, keepdims=True) + EPS) * gnw_ref[...]
        og = og_ref[rs, vs]
        ygla = o.astype(BF16) * (og * _sigmoid(og))
        part = _dot(ygla, wbg_ref[vs, :])
        br_g = part if br_g is None else br_g + part
    merged = (_sigmoid(g0_ref[rs, :]) * br_g.astype(BF16) + _sigmoid(g1_ref[rs, :]) * br_p.astype(BF16))
    delta = _dot(merged, wout_ref[...])
    h_new = h_ref[rs, :] + jnp.where(is_pad, 0.0, delta)
    hx_ref[rs, :] = h_new

    ms = jnp.mean(h_new * h_new, axis=-1, keepdims=True)
    xn = (h_new * lax.rsqrt(ms + EPS)) * n2_ref[...]
    xb = xn.astype(BF16)
    w_hi, w_lo = _split(wr_ref[...])
    lt = _dot_nt(w_hi, xb) + _dot_nt(w_lo, xb)

    lg = [lt[j:j + 1, :] for j in range(N_GROUPS)]
    m = jnp.maximum(jnp.maximum(lg[0], lg[1]), jnp.maximum(lg[2], lg[3]))
    gidx = jnp.where(lg[0] == m, 0, jnp.where(lg[1] == m, 1, jnp.where(lg[2] == m, 2, 3)))
    pg_top = 1.0 / (jnp.exp(lg[0] - m) + jnp.exp(lg[1] - m) + jnp.exp(lg[2] - m) + jnp.exp(lg[3] - m))
    sel = []
    for e in range(EXPERTS_PER_GROUP):
        cand = [lt[N_GROUPS + g * EXPERTS_PER_GROUP + e:N_GROUPS + g * EXPERTS_PER_GROUP + e + 1, :]
                for g in range(N_GROUPS)]
        sel.append(jnp.where(gidx == 0, cand[0], jnp.where(gidx == 1, cand[1],
                                                           jnp.where(gidx == 2, cand[2], cand[3]))))
    m1 = jnp.maximum(jnp.maximum(sel[0], sel[1]), jnp.maximum(sel[2], sel[3]))
    i1 = jnp.where(sel[0] == m1, 0, jnp.where(sel[1] == m1, 1, jnp.where(sel[2] == m1, 2, 3)))
    neg = jnp.float32(-jnp.inf)
    rest = [jnp.where(i1 == e, neg, sel[e]) for e in range(EXPERTS_PER_GROUP)]
    m2 = jnp.maximum(jnp.maximum(rest[0], rest[1]), jnp.maximum(rest[2], rest[3]))
    i2 = jnp.where(rest[0] == m2, 0, jnp.where(rest[1] == m2, 1, jnp.where(rest[2] == m2, 2, 3)))
    r21 = jnp.exp(m2 - m1)
    w1 = pg_top / (1.0 + r21)
    w2 = pg_top * r21 / (1.0 + r21)
    lo = jnp.minimum(i1, i2)
    hi = jnp.maximum(i1, i2)
    w_lo = jnp.where(i1 < i2, w1, w2)
    w_hi = jnp.where(i1 < i2, w2, w1)
    pidx = jnp.where(lo == 0, hi - 1, jnp.where(lo == 1, hi + 1, 5))
    cls = gidx * len(PAIRS) + pidx

    crow = lax.broadcasted_iota(jnp.int32, (CLASS_ROWS, rows), 0)
    onehot = jnp.where(crow == cls, 1.0, 0.0)
    prefix = _dot(onehot.astype(BF16), tri_ref[0:rows, 0:rows]) + carry_ref[:, 0:1]
    rank = jnp.sum(onehot * prefix, axis=0, keepdims=True)
    carry_ref[...] = carry_ref[...] + jnp.sum(onehot, axis=1, keepdims=True)

    irow = lax.broadcasted_iota(jnp.int32, (8, rows), 0)
    info_ref[:, rs] = jnp.where(irow == 0, cls.astype(F32), jnp.where(
        irow == 1, rank, jnp.where(irow == 2, w_lo, jnp.where(irow == 3, w_hi, 0.0))))


def _mix(ogla, z, h, layer_params, layer, *, tm, seq_rows):
    T, D = h.shape
    nblk = tm // HALO
    return pl.pallas_call(
        functools.partial(_mix_kernel, tm=tm, seq_rows=seq_rows, d_model=D),
        grid=(T // tm,),
        in_specs=[pl.BlockSpec((tm, D), lambda i: (i, 0)),
                  pl.BlockSpec((tm, D), lambda i: (i, 2)),
                  pl.BlockSpec((tm, D), lambda i: (i, 3)),
                  pl.BlockSpec((HALO, D), lambda i: (jnp.maximum(i * nblk - 1, 0), 3)),
                  pl.BlockSpec((tm, D), lambda i: (i, 4)),
                  pl.BlockSpec((tm, D), lambda i: (i, 5)),
                  pl.BlockSpec((tm, D), lambda i: (i, 0))]
                 + [_layer_block(p, layer) for p in layer_params],
        out_specs=[pl.BlockSpec((tm, D), lambda i: (i, 0)),
                   pl.BlockSpec((8, tm), lambda i: (0, i)),
                   pl.BlockSpec((CLASS_ROWS, LANES), lambda i: (0, 0))],
        out_shape=[jax.ShapeDtypeStruct((T, D), F32),
                   jax.ShapeDtypeStruct((8, T), F32),
                   jax.ShapeDtypeStruct((CLASS_ROWS, LANES), F32)],
        scratch_shapes=[pltpu.VMEM((tm + HALO, D), F32),
                        pltpu.VMEM((tm + HALO, D), F32),
                        pltpu.VMEM((tm + HALO, D - D // 4), F32),
                        pltpu.VMEM((tm + HALO, D - 2 * (D // 4)), F32),
                        pltpu.VMEM((tm + HALO, D // 4), F32),
                        pltpu.VMEM((tm, tm), BF16),
                        pltpu.VMEM((CLASS_ROWS, LANES), F32)],
        compiler_params=_cparams(("arbitrary",)),
        name="mix",
    )(ogla, z, z, z, z, z, h, *layer_params)


def _dispatch_kernel(tend_ref, nu_ref, pos_ref, h_ref, info_ref, xs_ref, row_ref, zero_ref, sem, zsem,
                     *, tm, tm_exp, n_tiles, d_model):
    def fill(j):
        return pltpu.make_async_copy(zero_ref, xs_ref.at[pl.ds(j * tm_exp, tm_exp)], zsem)

    @pl.when(pl.program_id(0) == 0)
    def _():
        zero_ref[...] = jnp.zeros_like(zero_ref)
        for wait in (False, True):
            for c in range(N_CLASSES):
                prev = tend_ref[c - 1] if c else 0

                @pl.when(tend_ref[c] > prev)
                def _():
                    cp = fill(tend_ref[c] - 1)
                    cp.wait() if wait else cp.start()

            def tail(j, carry):
                cp = fill(j)
                cp.wait() if wait else cp.start()
                return carry
            lax.fori_loop(nu_ref[0], n_tiles, tail, 0)

    i = pl.program_id(0)
    slot = i % 2
    rows = row_ref.at[slot]
    rows[:, 0:d_model] = h_ref[...]
    lane = lax.broadcasted_iota(jnp.int32, (ROUTE_LANES, tm), 0)
    rows[:, d_model:] = jnp.where(lane == 0, info_ref[2:3, :],
                                  jnp.where(lane == 1, info_ref[3:4, :], 0.0)).T
    for r in range(tm):
        pltpu.make_async_copy(rows.at[pl.ds(r, 1)], xs_ref.at[pl.ds(pos_ref[r], 1)],
                              sem.at[slot]).start(priority=r % DMA_THREADS)

    def wait_all(s):
        pltpu.make_async_copy(row_ref.at[s], xs_ref.at[pl.ds(0, tm)], sem.at[s]).wait()

    @pl.when(i > 0)
    def _():
        wait_all(1 - slot)

    @pl.when(i == pl.num_programs(0) - 1)
    def _():
        wait_all(slot)


def _dispatch(h, info, pos, tile_end, n_used, *, tm, tm_exp, n_sorted):
    T, D = h.shape
    W = D + ROUTE_LANES
    grid_spec = pltpu.PrefetchScalarGridSpec(
        num_scalar_prefetch=2,
        grid=(T // tm,),
        in_specs=[pl.BlockSpec((tm,), lambda i, te, nu: (i,), memory_space=pltpu.SMEM),
                  pl.BlockSpec((tm, D), lambda i, te, nu: (i, 0)),
                  pl.BlockSpec((8, tm), lambda i, te, nu: (0, i))],
        out_specs=pl.BlockSpec(memory_space=pl.ANY),
        scratch_shapes=[pltpu.VMEM((2, tm, W), F32), pltpu.VMEM((tm_exp, W), F32),
                        pltpu.SemaphoreType.DMA((2,)), pltpu.SemaphoreType.DMA(())],
    )
    return pl.pallas_call(
        functools.partial(_dispatch_kernel, tm=tm, tm_exp=tm_exp, n_tiles=n_sorted // tm_exp,
                          d_model=D),
        grid_spec=grid_spec,
        out_shape=jax.ShapeDtypeStruct((n_sorted, W), F32),
        compiler_params=_cparams(("arbitrary",)),
        name="dispatch",
    )(tile_end, n_used, pos, h, info)


def _collect_norm_kernel(pos0_ref, posn_ref, ys_ref, nw_ref, o_ref, buf_ref, sem, *, tm):
    i = pl.program_id(0)
    slot = i % 2

    @pl.when(i == 0)
    def _():
        _row_gather(ys_ref, pos0_ref, buf_ref.at[0], sem.at[0], tm)

    _row_gather(ys_ref, posn_ref, buf_ref.at[1 - slot], sem.at[1 - slot], tm)
    _row_gather_wait(ys_ref, buf_ref.at[slot], sem.at[slot], tm)
    x = buf_ref[slot]
    ms = jnp.mean(x * x, axis=-1, keepdims=True)
    o_ref[...] = (x * lax.rsqrt(ms + EPS)) * nw_ref[...]

    @pl.when(i == pl.num_programs(0) - 1)
    def _():
        _row_gather_wait(ys_ref, buf_ref.at[1 - slot], sem.at[1 - slot], tm)


def _collect_norm(ys, pos, norm_w, *, tm):
    n_out = pos.shape[0]
    D = ys.shape[1]
    n_steps = n_out // tm
    return pl.pallas_call(
        functools.partial(_collect_norm_kernel, tm=tm),
        grid=(n_steps,),
        in_specs=[pl.BlockSpec((tm,), lambda i: (0,), memory_space=pltpu.SMEM),
                  pl.BlockSpec((tm,), lambda i: (jnp.minimum(i + 1, n_steps - 1),),
                               memory_space=pltpu.SMEM),
                  pl.BlockSpec(memory_space=pl.ANY),
                  _resident(norm_w.shape)],
        out_specs=pl.BlockSpec((tm, D), lambda i: (i, 0)),
        out_shape=jax.ShapeDtypeStruct((n_out, D), F32),
        scratch_shapes=[pltpu.VMEM((2, tm, D), F32), pltpu.SemaphoreType.DMA((2,))],
        compiler_params=_cparams(("arbitrary",)),
        name="collect_norm",
    )(pos, pos, ys, norm_w)


def _moe_kernel(ea_ref, eb_ref, nu_ref, xs_ref, n2_ref, wga_ref, wua_ref, wda_ref,
                wgb_ref, wub_ref, wdb_ref, ys_ref, *, d_model):
    i = pl.program_id(0)

    @pl.when(i < nu_ref[0])
    def _():
        x = xs_ref[:, 0:d_model]
        ms = jnp.mean(x * x, axis=-1, keepdims=True)
        xn = ((x * lax.rsqrt(ms + EPS)) * n2_ref[...]).astype(BF16)
        experts = ((wga_ref, wua_ref, wda_ref), (wgb_ref, wub_ref, wdb_ref))
        pre = [(_dot(xn, wg[...]), _dot(xn, wu[...])) for wg, wu, _ in experts]
        hid = [(gate * _sigmoid(gate) * up).astype(BF16) for gate, up in pre]
        y = x
        for col, (_, _, wd) in enumerate(experts):
            y = y + xs_ref[:, d_model + col:d_model + col + 1] * _dot(hid[col], wd[...])
        ys_ref[...] = y

    @pl.when(i >= nu_ref[0])
    def _():
        ys_ref[...] = jnp.zeros_like(ys_ref)


def _moe(xs, tile_ea, tile_eb, n_used, n2, weg, weu, wed, layer, *, tm):
    n_sorted, W = xs.shape
    D = n2.shape[2]
    de = weg.shape[3]
    row = lambda i, ea, eb, nu: (jnp.minimum(i, nu[0] - 1), 0)
    wa = lambda i, ea, eb, nu: (layer, ea[i], 0, 0)
    wb = lambda i, ea, eb, nu: (layer, eb[i], 0, 0)
    grid_spec = pltpu.PrefetchScalarGridSpec(
        num_scalar_prefetch=3,
        grid=(n_sorted // tm,),
        in_specs=[pl.BlockSpec((tm, W), row),
                  pl.BlockSpec((None, 1, D), lambda i, ea, eb, nu: (layer, 0, 0)),
                  pl.BlockSpec((None, None, D, de), wa), pl.BlockSpec((None, None, D, de), wa),
                  pl.BlockSpec((None, None, de, D), wa),
                  pl.BlockSpec((None, None, D, de), wb), pl.BlockSpec((None, None, D, de), wb),
                  pl.BlockSpec((None, None, de, D), wb)],
        out_specs=pl.BlockSpec((tm, D), lambda i, ea, eb, nu: (i, 0)),
    )
    return pl.pallas_call(
        functools.partial(_moe_kernel, d_model=D),
        grid_spec=grid_spec,
        out_shape=jax.ShapeDtypeStruct((n_sorted, D), F32),
        compiler_params=_cparams(("arbitrary",)),
        name="experts",
    )(tile_ea, tile_eb, n_used, xs, n2, weg, weu, wed, weg, weu, wed)


def _row_tile(T, cap):
    t = cap
    while T % t:
        t //= 2
    return t


def kernel(x, meta_tokens, norm1_w, w_in, w_gate_up, b_gate, gla_norm_w, w_pool_grp, pool_scale,
           w_br_gla, w_br_pool, w_out, norm2_w, w_router_group, w_router_expert, w_exp_gate,
           w_exp_up, w_exp_down, final_norm_w):
    B, S, D = x.shape
    depth = w_in.shape[0]
    key = w_gate_up.shape[2]
    rank = w_gate_up.shape[1]
    dv = gla_norm_w.shape[1]
    val = GLA_HEADS * dv
    dk = key // GLA_HEADS
    assert S % CHUNK == 0 and key * 2 == D and val == D
    assert w_pool_grp.shape[2] * len(POOL_WINDOWS) == D
    LP = PAD + N_META + S
    n_chunks = LP // CHUNK
    T = B * LP
    tm_proj = tm_mix = _row_tile(T, ROW_TILE)
    tm_row = _row_tile(T, COPY_TILE)
    tm_exp = EXPERT_TILE
    n_sorted = -(-(T + N_CLASSES * (tm_exp - 1)) // tm_exp) * tm_exp
    n_tiles = n_sorted // tm_exp

    head = jnp.concatenate([jnp.zeros((PAD, D), F32), meta_tokens.astype(F32)], axis=0)
    lp = np.arange(T) % LP
    ys = x.astype(F32).reshape(B * S, D)
    pos = jnp.asarray(np.where(lp >= CHUNK, (np.arange(T) // LP) * S + lp - CHUNK, 0), jnp.int32)

    c_q = 0
    c_gl = 2 * key + 2 * val
    c_u = c_gl + rank

    w_a = w_in[:, :, c_q:c_gl].astype(BF16)
    w_b = w_in[:, :, c_u:].astype(BF16)
    w_g = jnp.concatenate([w_in[:, :, c_gl:c_u].astype(BF16),
                           jnp.zeros((depth, D, LANES - rank), BF16)], axis=2)
    wg_hi, wg_lo = _split(jnp.concatenate(
        [w_gate_up, jnp.zeros((depth, LANES - rank, key), F32)], axis=1))
    wr = jnp.swapaxes(jnp.concatenate(
        [w_router_group, w_router_expert,
         jnp.zeros((depth, D, CLASS_ROWS - N_GROUPS - N_EXPERTS), F32)], axis=2), 1, 2)
    row_vec = lambda p: p[:, None, :]
    mix_params = (row_vec(gla_norm_w), w_pool_grp.astype(BF16), row_vec(pool_scale),
                  w_br_gla.astype(BF16), w_br_pool.astype(BF16), w_out.astype(BF16),
                  row_vec(norm2_w), wr)
    weg, weu, wed = w_exp_gate.astype(BF16), w_exp_up.astype(BF16), w_exp_down.astype(BF16)

    for l in range(depth):
        z, zg, h = _inproj_gather(ys, pos, head, row_vec(norm1_w), w_a, w_b, w_g, l, tm=tm_proj,
                                  seq_rows=LP if l == 0 else 0, n_batch=B)
        ogla = _gla(z, zg, wg_hi, wg_lo, row_vec(b_gate), l, batch=B, n_chunks=n_chunks, dk=dk, dv=dv)
        hx, info, counts = _mix(ogla, z, h, mix_params, l, tm=tm_mix, seq_rows=LP)

        cnt = counts[:N_CLASSES, 0].astype(jnp.int32)
        seg_tiles = (cnt + tm_exp - 1) // tm_exp
        tile_end = jnp.cumsum(seg_tiles)
        start = (tile_end - seg_tiles) * tm_exp
        n_used = tile_end[-1:]
        tile_cls = jnp.minimum(
            jnp.sum(jnp.arange(n_tiles)[:, None] >= tile_end[None, :], axis=1), N_CLASSES - 1)
        tile_cls = jnp.where(jnp.arange(n_tiles) < n_used[0], tile_cls,
                             jnp.take(tile_cls, jnp.maximum(n_used[0] - 1, 0)))
        pair_lo = jnp.array([p[0] for p in PAIRS], jnp.int32)
        pair_hi = jnp.array([p[1] for p in PAIRS], jnp.int32)
        grp = tile_cls // len(PAIRS)
        tile_ea = (grp * EXPERTS_PER_GROUP + jnp.take(pair_lo, tile_cls % len(PAIRS))).astype(jnp.int32)
        tile_eb = (grp * EXPERTS_PER_GROUP + jnp.take(pair_hi, tile_cls % len(PAIRS))).astype(jnp.int32)
        cls_t = info[0].astype(jnp.int32)
        pos = info[1].astype(jnp.int32) + jnp.sum(
            jnp.where(cls_t[:, None] == jnp.arange(N_CLASSES)[None, :], start[None, :], 0), axis=1)

        xs = _dispatch(hx, info, pos, tile_end.astype(jnp.int32), n_used.astype(jnp.int32),
                       tm=tm_row, tm_exp=tm_exp, n_sorted=n_sorted)
        ys = _moe(xs, tile_ea, tile_eb, n_used.astype(jnp.int32), row_vec(norm2_w), weg, weu, wed, l,
                  tm=tm_exp)

    pos_out = pos.reshape(B, LP)[:, PAD + N_META:].reshape(B * S)
    out = _collect_norm(ys, pos_out, final_norm_w[None], tm=_row_tile(B * S, COPY_TILE))
    return out.reshape(B, S, D)
```

```python
import functools

import jax
import jax.numpy as jnp
import numpy as np
from jax import lax
from jax.experimental import pallas as pl
from jax.experimental.pallas import tpu as pltpu

F32 = jnp.float32
BF16 = jnp.bfloat16

EPS = 1e-6
CHUNK = 64
N_META = 16
PAD = CHUNK - N_META
GLA_HEADS = 4
GATE_NORMALIZER = 16.0
POOL_WINDOWS = (2, 4, 8, 16)
N_GROUPS = 4
EXPERTS_PER_GROUP = 4
N_EXPERTS = N_GROUPS * EXPERTS_PER_GROUP
PAIRS = ((0, 1), (0, 2), (0, 3), (1, 2), (1, 3), (2, 3))
N_CLASSES = N_GROUPS * len(PAIRS)
CLASS_ROWS = 32
LANES = 128
HALO = 32
ROUTE_LANES = 128
DMA_THREADS = 2
CUMSUM_ROWS = 256

VMEM_LIMIT = 56 * 1024 * 1024
ROW_TILE = 512
COPY_TILE = 1024
EXPERT_TILE = 512
PROJ_COL_BLOCK = 512


def _cparams(sem):
    return pltpu.CompilerParams(dimension_semantics=sem, vmem_limit_bytes=VMEM_LIMIT)


def _resident(shape):
    nd = len(shape)
    return pl.BlockSpec(shape, lambda *_: (0,) * nd, pipeline_mode=pl.Buffered(1))


def _layer_block(arr, layer):
    nd = arr.ndim
    return pl.BlockSpec((None,) + arr.shape[1:], lambda *_: (layer,) + (0,) * (nd - 1),
                        pipeline_mode=pl.Buffered(1))


def _split(x):
    hi = x.astype(BF16)
    lo = (x - hi.astype(F32)).astype(BF16)
    return hi, lo


def _dot(a, b):
    return jnp.dot(a, b, preferred_element_type=F32)


def _dot_nt(a, b):
    return lax.dot_general(a, b, (((1,), (1,)), ((), ())), preferred_element_type=F32)


def _dot_tn(a, b):
    return lax.dot_general(a, b, (((0,), (0,)), ((), ())), preferred_element_type=F32)


def _sigmoid(x):
    return 0.5 * jnp.tanh(0.5 * x) + 0.5


def _row_gather(src_ref, pos_ref, dst_ref, sem, tm):
    for r in range(tm):
        pltpu.make_async_copy(src_ref.at[pl.ds(pos_ref[r], 1)], dst_ref.at[pl.ds(r, 1)],
                              sem).start(priority=r % DMA_THREADS)


def _row_gather_wait(src_ref, dst_ref, sem, tm):
    pltpu.make_async_copy(src_ref.at[pl.ds(0, tm)], dst_ref, sem).wait()


def _inproj_gather_kernel(pos0_ref, posn_ref, ys_ref, head_ref, nw_ref, wa_ref, wb_ref, wg_ref,
                          z_ref, zg_ref, h_ref, buf_ref, xn_ref, sem,
                          *, tm, col_blk, seq_rows, n_batch):
    i = pl.program_id(0)
    last = pl.num_programs(0) - 1
    slot = i % 2

    @pl.when(i == 0)
    def _():
        _row_gather(ys_ref, pos0_ref, buf_ref.at[0], sem.at[0], tm)

    _row_gather_wait(ys_ref, buf_ref.at[slot], sem.at[slot], tm)
    if seq_rows:
        b0 = (i * tm + seq_rows - 1) // seq_rows
        for k in range(tm // seq_rows + 1):
            off = (b0 + k) * seq_rows - i * tm

            @pl.when(jnp.logical_and(off < tm, b0 + k < n_batch))
            def _():
                buf_ref[slot, pl.ds(pl.multiple_of(off, CHUNK), CHUNK), :] = head_ref[...]

    x = buf_ref[slot]
    h_ref[...] = x
    ms = jnp.mean(x * x, axis=-1, keepdims=True)
    xn_ref[...] = ((x * lax.rsqrt(ms + EPS)) * nw_ref[...]).astype(BF16)
    _row_gather(ys_ref, posn_ref, buf_ref.at[1 - slot], sem.at[1 - slot], tm)
    col = 0
    for w_ref in (wa_ref, wb_ref):
        for j in range(w_ref.shape[1] // col_blk):
            sl = slice(j * col_blk, (j + 1) * col_blk)
            z_ref[:, col:col + col_blk] = _dot(xn_ref[...], w_ref[:, sl]).astype(BF16)
            col += col_blk
    zg_ref[...] = _dot(xn_ref[...], wg_ref[...])

    @pl.when(i == last)
    def _():
        _row_gather_wait(ys_ref, buf_ref.at[1 - slot], sem.at[1 - slot], tm)


def _inproj_gather(ys, pos, head, norm_w, w_a, w_b, w_g, layer, *, tm, seq_rows=0, n_batch=0):
    T = pos.shape[0]
    D = ys.shape[1]
    n_main = w_a.shape[2] + w_b.shape[2]
    n_all = n_main + w_g.shape[2]
    n_steps = T // tm
    assert tm % CHUNK == 0 and seq_rows % CHUNK == 0
    return pl.pallas_call(
        functools.partial(_inproj_gather_kernel, tm=tm, col_blk=PROJ_COL_BLOCK,
                          seq_rows=seq_rows, n_batch=n_batch),
        grid=(n_steps,),
        in_specs=[pl.BlockSpec((tm,), lambda i: (0,), memory_space=pltpu.SMEM),
                  pl.BlockSpec((tm,), lambda i: (jnp.minimum(i + 1, n_steps - 1),),
                               memory_space=pltpu.SMEM),
                  pl.BlockSpec(memory_space=pl.ANY),
                  _resident(head.shape),
                  _layer_block(norm_w, layer), _layer_block(w_a, layer),
                  _layer_block(w_b, layer), _layer_block(w_g, layer)],
        out_specs=[pl.BlockSpec((tm, n_main), lambda i: (i, 0)),
                   pl.BlockSpec((tm, n_all - n_main), lambda i: (i, 0)),
                   pl.BlockSpec((tm, D), lambda i: (i, 0))],
        out_shape=[jax.ShapeDtypeStruct((T, n_main), BF16),
                   jax.ShapeDtypeStruct((T, n_all - n_main), F32),
                   jax.ShapeDtypeStruct((T, D), F32)],
        scratch_shapes=[pltpu.VMEM((2, tm, D), F32), pltpu.VMEM((tm, D), BF16),
                        pltpu.SemaphoreType.DMA((2,))],
        compiler_params=_cparams(("arbitrary",)),
        name="inproj_gather",
    )(pos, pos, ys, head, norm_w, w_a, w_b, w_g)


def _gla_prep(q_ref, k_ref, zg_ref, wgh_ref, wgl_ref, bg_ref, scaled_ref, dec_ref, slot, first,
              *, bb, dk):
    rows = bb * CHUNK
    key = GLA_HEADS * dk
    gl = zg_ref[...].reshape(rows, LANES).astype(BF16)
    gpre = _dot(gl, wgh_ref[...]) + _dot(gl, wgl_ref[...]) + bg_ref[...]
    g = (jnp.minimum(gpre, 0.0) - jnp.log(1.0 + jnp.exp(-jnp.abs(gpre)))) * (1.0 / GATE_NORMALIZER)
    if first:
        row = lax.broadcasted_iota(jnp.int32, (rows, 1), 0)
        g = jnp.where((row & (CHUNK - 1)) < PAD, 0.0, g)

    blk = min(rows, CUMSUM_ROWS)
    ri = lax.broadcasted_iota(jnp.int32, (blk, blk), 0)
    ci = lax.broadcasted_iota(jnp.int32, (blk, blk), 1)
    tri = jnp.where(jnp.logical_and((ri // CHUNK) == (ci // CHUNK), ri >= ci), 1.0, 0.0).astype(BF16)
    g_hi, g_lo = _split(g)
    bcum = jnp.concatenate(
        [_dot(tri, g_hi[r:r + blk]) + _dot(tri, g_lo[r:r + blk]) for r in range(0, rows, blk)], axis=0)
    gam = jnp.concatenate(
        [jnp.broadcast_to(bcum[(c + 1) * CHUNK - 1:(c + 1) * CHUNK, :], (CHUNK, key)) for c in range(bb)],
        axis=0)
    eb = jnp.exp(bcum)
    ieb = jnp.exp(-bcum)
    q = q_ref[...].reshape(rows, key).astype(F32) * (dk ** -0.5)
    k = k_ref[...].reshape(rows, key).astype(F32)
    scaled_ref[slot, 0] = (q * eb).astype(BF16)
    scaled_ref[slot, 1] = (q * ieb).astype(BF16)
    scaled_ref[slot, 2] = (k * eb).astype(BF16)
    scaled_ref[slot, 3] = (k * ieb).astype(BF16)
    scaled_ref[slot, 4] = (k * jnp.exp(gam - bcum)).astype(BF16)
    for c in range(bb):
        dec_ref[slot, c] = eb[(c + 1) * CHUNK - 1:(c + 1) * CHUNK, :]


def _gla_kernel(q0_ref, k0_ref, zg0_ref, qn_ref, kn_ref, zgn_ref, v_ref, wgh_ref, wgl_ref, bg_ref,
                o_ref, st_ref, scaled_ref, dec_ref, *, bb, dk, dv):
    n = pl.program_id(1)
    prep = functools.partial(_gla_prep, wgh_ref=wgh_ref, wgl_ref=wgl_ref, bg_ref=bg_ref,
                             scaled_ref=scaled_ref, dec_ref=dec_ref, bb=bb, dk=dk)

    @pl.when(n == 0)
    def _():
        st_ref[...] = jnp.zeros_like(st_ref)
        prep(q0_ref, k0_ref, zg0_ref, slot=0, first=True)

    li = lax.broadcasted_iota(jnp.int32, (CHUNK, CHUNK), 0)
    lj = lax.broadcasted_iota(jnp.int32, (CHUNK, CHUNK), 1)
    lower = li >= lj
    groups = [(c, hd) for c in range(bb) for hd in range(GLA_HEADS)]

    def step(slot):
        prep(qn_ref, kn_ref, zgn_ref, slot=1 - slot, first=False)
        attn = {}
        for c, hd in groups:
            rs = slice(c * CHUNK, (c + 1) * CHUNK)
            ks = slice(hd * dk, (hd + 1) * dk)
            a_lo = _dot_nt(scaled_ref[slot, 0, rs, ks], scaled_ref[slot, 3, rs, ks])
            a_up = _dot_nt(scaled_ref[slot, 1, rs, ks], scaled_ref[slot, 2, rs, ks])
            attn[c, hd] = jnp.where(lower, a_lo, a_up).astype(BF16)
        for c, hd in groups:
            rs = slice(c * CHUNK, (c + 1) * CHUNK)
            ks = slice(hd * dk, (hd + 1) * dk)
            vs = slice(hd * dv, (hd + 1) * dv)
            o_ref[c, :, vs] = (_dot(attn[c, hd], v_ref[c, :, vs])
                               + _dot(scaled_ref[slot, 0, rs, ks], st_ref[c, hd].astype(BF16))
                               ).astype(BF16)
        for c, hd in groups:
            rs = slice(c * CHUNK, (c + 1) * CHUNK)
            ks = slice(hd * dk, (hd + 1) * dk)
            vs = slice(hd * dv, (hd + 1) * dv)
            dec = jnp.broadcast_to(dec_ref[slot, c, :, ks], (dk, dk)).T
            st_ref[c, hd] = (st_ref[c, hd] * jnp.concatenate([dec] * (dv // dk), axis=1)
                             + _dot_tn(scaled_ref[slot, 4, rs, ks], v_ref[c, :, vs]))

    for slot in (0, 1):
        pl.when(n % 2 == slot)(functools.partial(step, slot))


def _gla(z, zg, wg_hi, wg_lo, bg, layer, *, batch, n_chunks, dk, dv):
    T = z.shape[0]
    key = GLA_HEADS * dk
    val = GLA_HEADS * dv
    bb = 8 if batch % 8 == 0 else (4 if batch % 4 == 0 else 1)
    z3 = z.reshape(batch, n_chunks * CHUNK, z.shape[1])
    zg3 = zg.reshape(batch, n_chunks * CHUNK, zg.shape[1])
    nxt = lambda n: jnp.minimum(n + 1, n_chunks - 1)
    o = pl.pallas_call(
        functools.partial(_gla_kernel, bb=bb, dk=dk, dv=dv),
        grid=(batch // bb, n_chunks),
        in_specs=[pl.BlockSpec((bb, CHUNK, key), lambda b, n: (b, 0, 0)),
                  pl.BlockSpec((bb, CHUNK, key), lambda b, n: (b, 0, 1)),
                  pl.BlockSpec((bb, CHUNK, LANES), lambda b, n: (b, 0, 0)),
                  pl.BlockSpec((bb, CHUNK, key), lambda b, n: (b, nxt(n), 0)),
                  pl.BlockSpec((bb, CHUNK, key), lambda b, n: (b, nxt(n), 1)),
                  pl.BlockSpec((bb, CHUNK, LANES), lambda b, n: (b, nxt(n), 0)),
                  pl.BlockSpec((bb, CHUNK, val), lambda b, n: (b, n, 1)),
                  _layer_block(wg_hi, layer), _layer_block(wg_lo, layer), _layer_block(bg, layer)],
        out_specs=pl.BlockSpec((bb, CHUNK, val), lambda b, n: (b, n, 0)),
        out_shape=jax.ShapeDtypeStruct((batch, n_chunks * CHUNK, val), BF16),
        scratch_shapes=[pltpu.VMEM((bb, GLA_HEADS, dk, dv), F32),
                        pltpu.VMEM((2, 5, bb * CHUNK, key), BF16),
                        pltpu.VMEM((2, bb, 1, key), F32)],
        compiler_params=_cparams(("parallel", "arbitrary")),
        name="gla",
    )(z3, z3, zg3, z3, z3, zg3, z3, wg_hi, wg_lo, bg)
    return o.reshape(T, val)


def _mix_kernel(ogla_ref, og_ref, u_ref, halo_ref, g0_ref, g1_ref, h_ref, gnw_ref,
                wgrp_ref, pscale_ref, wbg_ref, wbp_ref, wout_ref, n2_ref, wr_ref,
                hx_ref, info_ref, cnt_ref,
                ue_ref, p2_ref, p4_ref, p8_ref, p16_ref, tri_ref, carry_ref,
                *, tm, seq_rows, d_model):
    i = pl.program_id(0)
    gd = d_model // len(POOL_WINDOWS)

    @pl.when(i == 0)
    def _():
        r = lax.broadcasted_iota(jnp.int32, (tm, tm), 0)
        c = lax.broadcasted_iota(jnp.int32, (tm, tm), 1)
        tri_ref[...] = jnp.where(r < c, 1.0, 0.0).astype(BF16)
        carry_ref[...] = jnp.zeros_like(carry_ref)

    ue_ref[0:HALO, :] = halo_ref[...].astype(F32)
    ue_ref[HALO:, :] = u_ref[...].astype(F32)
    n8 = tm + HALO - 8
    p2_ref[8:, :] = ue_ref[8:, :] + ue_ref[pl.ds(7, n8), :]
    p4_ref[16:, :] = p2_ref[16:, gd:] + p2_ref[pl.ds(14, n8 - 8), gd:]
    p8_ref[24:, :] = p4_ref[24:, gd:] + p4_ref[pl.ds(20, n8 - 16), gd:]
    p16_ref[HALO:, :] = p8_ref[HALO:, gd:] + p8_ref[pl.ds(24, tm), gd:]
    sum_refs = (p2_ref, p4_ref, p8_ref, p16_ref)

    _mix_rows(0, tm, i * tm, ogla_ref, og_ref, g0_ref, g1_ref, h_ref, gnw_ref, wgrp_ref,
              pscale_ref, wbg_ref, wbp_ref, wout_ref, n2_ref, wr_ref, hx_ref, info_ref,
              ue_ref, sum_refs, tri_ref, carry_ref, seq_rows=seq_rows, d_model=d_model)
    cnt_ref[...] = carry_ref[...]


def _mix_rows(r0, rows, tile_row0, ogla_ref, og_ref, g0_ref, g1_ref, h_ref, gnw_ref, wgrp_ref,
              pscale_ref, wbg_ref, wbp_ref, wout_ref, n2_ref, wr_ref, hx_ref, info_ref,
              ue_ref, sum_refs, tri_ref, carry_ref, *, seq_rows, d_model):
    gd = d_model // len(POOL_WINDOWS)
    rs = pl.ds(r0, rows)
    rh = pl.ds(r0 + HALO, rows)

    rowf = (tile_row0 + r0 + lax.broadcasted_iota(jnp.int32, (rows, 1), 0)).astype(F32)
    lp = rowf - jnp.floor(rowf / float(seq_rows)) * float(seq_rows)
    is_pad = lp < float(PAD)
    seqpos1 = lp - float(PAD - 1)

    br_p = None
    for gi, w in enumerate(POOL_WINDOWS):
        cs = slice(gi * gd, (gi + 1) * gd)
        inv_cnt = 1.0 / jnp.clip(seqpos1, 1.0, float(w))
        pooled = sum_refs[gi][rh, 0:gd] * inv_cnt - ue_ref[rh, cs]
        mixed = _dot(pooled.astype(BF16), wgrp_ref[gi]) * pscale_ref[:, cs]
        part = _dot(mixed.astype(BF16), wbp_ref[cs, :])
        br_p = part if br_p is None else br_p + part

    dv = gnw_ref.shape[1]
    br_g = None
    for hd in range(GLA_HEADS):
        vs = slice(hd * dv, (hd + 1) * dv)
        o = ogla_ref[rs, vs].astype(F32)
        o = o * lax.rsqrt(jnp.mean(o * o, axis=-1, keepdims=True) + EPS) * gnw_ref[...]
        og = og_ref[rs, vs]
        ygla = o.astype(BF16) * (og * _sigmoid(og))
        part = _dot(ygla, wbg_ref[vs, :])
        br_g = part if br_g is None else br_g + part
    merged = (_sigmoid(g0_ref[rs, :]) * br_g.astype(BF16) + _sigmoid(g1_ref[rs, :]) * br_p.astype(BF16))
    delta = _dot(merged, wout_ref[...])
    h_new = h_ref[rs, :] + jnp.where(is_pad, 0.0, delta)
    hx_ref[rs, :] = h_new

    ms = jnp.mean(h_new * h_new, axis=-1, keepdims=True)
    xn = (h_new * lax.rsqrt(ms + EPS)) * n2_ref[...]
    xb = xn.astype(BF16)
    w_hi, w_lo = _split(wr_ref[...])
    lt = _dot_nt(w_hi, xb) + _dot_nt(w_lo, xb)

    lg = [lt[j:j + 1, :] for j in range(N_GROUPS)]
    m = jnp.maximum(jnp.maximum(lg[0], lg[1]), jnp.maximum(lg[2], lg[3]))
    gidx = jnp.where(lg[0] == m, 0, jnp.where(lg[1] == m, 1, jnp.where(lg[2] == m, 2, 3)))
    pg_top = 1.0 / (jnp.exp(lg[0] - m) + jnp.exp(lg[1] - m) + jnp.exp(lg[2] - m) + jnp.exp(lg[3] - m))
    sel = []
    for e in range(EXPERTS_PER_GROUP):
        cand = [lt[N_GROUPS + g * EXPERTS_PER_GROUP + e:N_GROUPS + g * EXPERTS_PER_GROUP + e + 1, :]
                for g in range(N_GROUPS)]
        sel.append(jnp.where(gidx == 0, cand[0], jnp.where(gidx == 1, cand[1],
                                                           jnp.where(gidx == 2, cand[2], cand[3]))))
    m1 = jnp.maximum(jnp.maximum(sel[0], sel[1]), jnp.maximum(sel[2], sel[3]))
    i1 = jnp.where(sel[0] == m1, 0, jnp.where(sel[1] == m1, 1, jnp.where(sel[2] == m1, 2, 3)))
    neg = jnp.float32(-jnp.inf)
    rest = [jnp.where(i1 == e, neg, sel[e]) for e in range(EXPERTS_PER_GROUP)]
    m2 = jnp.maximum(jnp.maximum(rest[0], rest[1]), jnp.maximum(rest[2], rest[3]))
    i2 = jnp.where(rest[0] == m2, 0, jnp.where(rest[1] == m2, 1, jnp.where(rest[2] == m2, 2, 3)))
    r21 = jnp.exp(m2 - m1)
    w1 = pg_top / (1.0 + r21)
    w2 = pg_top * r21 / (1.0 + r21)
    lo = jnp.minimum(i1, i2)
    hi = jnp.maximum(i1, i2)
    w_lo = jnp.where(i1 < i2, w1, w2)
    w_hi = jnp.where(i1 < i2, w2, w1)
    pidx = jnp.where(lo == 0, hi - 1, jnp.where(lo == 1, hi + 1, 5))
    cls = gidx * len(PAIRS) + pidx

    crow = lax.broadcasted_iota(jnp.int32, (CLASS_ROWS, rows), 0)
    onehot = jnp.where(crow == cls, 1.0, 0.0)
    prefix = _dot(onehot.astype(BF16), tri_ref[0:rows, 0:rows]) + carry_ref[:, 0:1]
    rank = jnp.sum(onehot * prefix, axis=0, keepdims=True)
    carry_ref[...] = carry_ref[...] + jnp.sum(onehot, axis=1, keepdims=True)

    irow = lax.broadcasted_iota(jnp.int32, (8, rows), 0)
    info_ref[:, rs] = jnp.where(irow == 0, cls.astype(F32), jnp.where(
        irow == 1, rank, jnp.where(irow == 2, w_lo, jnp.where(irow == 3, w_hi, 0.0))))


def _mix(ogla, z, h, layer_params, layer, *, tm, seq_rows):
    T, D = h.shape
    nblk = tm // HALO
    return pl.pallas_call(
        functools.partial(_mix_kernel, tm=tm, seq_rows=seq_rows, d_model=D),
        grid=(T // tm,),
        in_specs=[pl.BlockSpec((tm, D), lambda i: (i, 0)),
                  pl.BlockSpec((tm, D), lambda i: (i, 2)),
                  pl.BlockSpec((tm, D), lambda i: (i, 3)),
                  pl.BlockSpec((HALO, D), lambda i: (jnp.maximum(i * nblk - 1, 0), 3)),
                  pl.BlockSpec((tm, D), lambda i: (i, 4)),
                  pl.BlockSpec((tm, D), lambda i: (i, 5)),
                  pl.BlockSpec((tm, D), lambda i: (i, 0))]
                 + [_layer_block(p, layer) for p in layer_params],
        out_specs=[pl.BlockSpec((tm, D), lambda i: (i, 0)),
                   pl.BlockSpec((8, tm), lambda i: (0, i)),
                   pl.BlockSpec((CLASS_ROWS, LANES), lambda i: (0, 0))],
        out_shape=[jax.ShapeDtypeStruct((T, D), F32),
                   jax.ShapeDtypeStruct((8, T), F32),
                   jax.ShapeDtypeStruct((CLASS_ROWS, LANES), F32)],
        scratch_shapes=[pltpu.VMEM((tm + HALO, D), F32),
                        pltpu.VMEM((tm + HALO, D), F32),
                        pltpu.VMEM((tm + HALO, D - D // 4), F32),
                        pltpu.VMEM((tm + HALO, D - 2 * (D // 4)), F32),
                        pltpu.VMEM((tm + HALO, D // 4), F32),
                        pltpu.VMEM((tm, tm), BF16),
                        pltpu.VMEM((CLASS_ROWS, LANES), F32)],
        compiler_params=_cparams(("arbitrary",)),
        name="mix",
    )(ogla, z, z, z, z, z, h, *layer_params)


def _dispatch_kernel(tend_ref, nu_ref, pos_ref, h_ref, info_ref, xs_ref, row_ref, zero_ref, sem, zsem,
                     *, tm, tm_exp, n_tiles, d_model):
    def fill(j):
        return pltpu.make_async_copy(zero_ref, xs_ref.at[pl.ds(j * tm_exp, tm_exp)], zsem)

    @pl.when(pl.program_id(0) == 0)
    def _():
        zero_ref[...] = jnp.zeros_like(zero_ref)
        for wait in (False, True):
            for c in range(N_CLASSES):
                prev = tend_ref[c - 1] if c else 0

                @pl.when(tend_ref[c] > prev)
                def _():
                    cp = fill(tend_ref[c] - 1)
                    cp.wait() if wait else cp.start()

            def tail(j, carry):
                cp = fill(j)
                cp.wait() if wait else cp.start()
                return carry
            lax.fori_loop(nu_ref[0], n_tiles, tail, 0)

    i = pl.program_id(0)
    slot = i % 2
    rows = row_ref.at[slot]
    rows[:, 0:d_model] = h_ref[...]
    lane = lax.broadcasted_iota(jnp.int32, (ROUTE_LANES, tm), 0)
    rows[:, d_model:] = jnp.where(lane == 0, info_ref[2:3, :],
                                  jnp.where(lane == 1, info_ref[3:4, :], 0.0)).T
    for r in range(tm):
        pltpu.make_async_copy(rows.at[pl.ds(r, 1)], xs_ref.at[pl.ds(pos_ref[r], 1)],
                              sem.at[slot]).start(priority=r % DMA_THREADS)

    def wait_all(s):
        pltpu.make_async_copy(row_ref.at[s], xs_ref.at[pl.ds(0, tm)], sem.at[s]).wait()

    @pl.when(i > 0)
    def _():
        wait_all(1 - slot)

    @pl.when(i == pl.num_programs(0) - 1)
    def _():
        wait_all(slot)


def _dispatch(h, info, pos, tile_end, n_used, *, tm, tm_exp, n_sorted):
    T, D = h.shape
    W = D + ROUTE_LANES
    grid_spec = pltpu.PrefetchScalarGridSpec(
        num_scalar_prefetch=2,
        grid=(T // tm,),
        in_specs=[pl.BlockSpec((tm,), lambda i, te, nu: (i,), memory_space=pltpu.SMEM),
                  pl.BlockSpec((tm, D), lambda i, te, nu: (i, 0)),
                  pl.BlockSpec((8, tm), lambda i, te, nu: (0, i))],
        out_specs=pl.BlockSpec(memory_space=pl.ANY),
        scratch_shapes=[pltpu.VMEM((2, tm, W), F32), pltpu.VMEM((tm_exp, W), F32),
                        pltpu.SemaphoreType.DMA((2,)), pltpu.SemaphoreType.DMA(())],
    )
    return pl.pallas_call(
        functools.partial(_dispatch_kernel, tm=tm, tm_exp=tm_exp, n_tiles=n_sorted // tm_exp,
                          d_model=D),
        grid_spec=grid_spec,
        out_shape=jax.ShapeDtypeStruct((n_sorted, W), F32),
        compiler_params=_cparams(("arbitrary",)),
        name="dispatch",
    )(tile_end, n_used, pos, h, info)


def _collect_norm_kernel(pos0_ref, posn_ref, ys_ref, nw_ref, o_ref, buf_ref, sem, *, tm):
    i = pl.program_id(0)
    slot = i % 2

    @pl.when(i == 0)
    def _():
        _row_gather(ys_ref, pos0_ref, buf_ref.at[0], sem.at[0], tm)

    _row_gather(ys_ref, posn_ref, buf_ref.at[1 - slot], sem.at[1 - slot], tm)
    _row_gather_wait(ys_ref, buf_ref.at[slot], sem.at[slot], tm)
    x = buf_ref[slot]
    ms = jnp.mean(x * x, axis=-1, keepdims=True)
    o_ref[...] = (x * lax.rsqrt(ms + EPS)) * nw_ref[...]

    @pl.when(i == pl.num_programs(0) - 1)
    def _():
        _row_gather_wait(ys_ref, buf_ref.at[1 - slot], sem.at[1 - slot], tm)


def _collect_norm(ys, pos, norm_w, *, tm):
    n_out = pos.shape[0]
    D = ys.shape[1]
    n_steps = n_out // tm
    return pl.pallas_call(
        functools.partial(_collect_norm_kernel, tm=tm),
        grid=(n_steps,),
        in_specs=[pl.BlockSpec((tm,), lambda i: (0,), memory_space=pltpu.SMEM),
                  pl.BlockSpec((tm,), lambda i: (jnp.minimum(i + 1, n_steps - 1),),
                               memory_space=pltpu.SMEM),
                  pl.BlockSpec(memory_space=pl.ANY),
                  _resident(norm_w.shape)],
        out_specs=pl.BlockSpec((tm, D), lambda i: (i, 0)),
        out_shape=jax.ShapeDtypeStruct((n_out, D), F32),
        scratch_shapes=[pltpu.VMEM((2, tm, D), F32), pltpu.SemaphoreType.DMA((2,))],
        compiler_params=_cparams(("arbitrary",)),
        name="collect_norm",
    )(pos, pos, ys, norm_w)


def _moe_kernel(ea_ref, eb_ref, nu_ref, xs_ref, n2_ref, wga_ref, wua_ref, wda_ref,
                wgb_ref, wub_ref, wdb_ref, ys_ref, *, d_model):
    i = pl.program_id(0)

    @pl.when(i < nu_ref[0])
    def _():
        x = xs_ref[:, 0:d_model]
        ms = jnp.mean(x * x, axis=-1, keepdims=True)
        xn = ((x * lax.rsqrt(ms + EPS)) * n2_ref[...]).astype(BF16)
        experts = ((wga_ref, wua_ref, wda_ref), (wgb_ref, wub_ref, wdb_ref))
        pre = [(_dot(xn, wg[...]), _dot(xn, wu[...])) for wg, wu, _ in experts]
        hid = [(gate * _sigmoid(gate) * up).astype(BF16) for gate, up in pre]
        y = x
        for col, (_, _, wd) in enumerate(experts):
            y = y + xs_ref[:, d_model + col:d_model + col + 1] * _dot(hid[col], wd[...])
        ys_ref[...] = y

    @pl.when(i >= nu_ref[0])
    def _():
        ys_ref[...] = jnp.zeros_like(ys_ref)


def _moe(xs, tile_ea, tile_eb, n_used, n2, weg, weu, wed, layer, *, tm):
    n_sorted, W = xs.shape
    D = n2.shape[2]
    de = weg.shape[3]
    row = lambda i, ea, eb, nu: (jnp.minimum(i, nu[0] - 1), 0)
    wa = lambda i, ea, eb, nu: (layer, ea[i], 0, 0)
    wb = lambda i, ea, eb, nu: (layer, eb[i], 0, 0)
    grid_spec = pltpu.PrefetchScalarGridSpec(
        num_scalar_prefetch=3,
        grid=(n_sorted // tm,),
        in_specs=[pl.BlockSpec((tm, W), row),
                  pl.BlockSpec((None, 1, D), lambda i, ea, eb, nu: (layer, 0, 0)),
                  pl.BlockSpec((None, None, D, de), wa), pl.BlockSpec((None, None, D, de), wa),
                  pl.BlockSpec((None, None, de, D), wa),
                  pl.BlockSpec((None, None, D, de), wb), pl.BlockSpec((None, None, D, de), wb),
                  pl.BlockSpec((None, None, de, D), wb)],
        out_specs=pl.BlockSpec((tm, D), lambda i, ea, eb, nu: (i, 0)),
    )
    return pl.pallas_call(
        functools.partial(_moe_kernel, d_model=D),
        grid_spec=grid_spec,
        out_shape=jax.ShapeDtypeStruct((n_sorted, D), F32),
        compiler_params=_cparams(("arbitrary",)),
        name="experts",
    )(tile_ea, tile_eb, n_used, xs, n2, weg, weu, wed, weg, weu, wed)


def _row_tile(T, cap):
    t = cap
    while T % t:
        t //= 2
    return t


def kernel(x, meta_tokens, norm1_w, w_in, w_gate_up, b_gate, gla_norm_w, w_pool_grp, pool_scale,
           w_br_gla, w_br_pool, w_out, norm2_w, w_router_group, w_router_expert, w_exp_gate,
           w_exp_up, w_exp_down, final_norm_w):
    B, S, D = x.shape
    depth = w_in.shape[0]
    key = w_gate_up.shape[2]
    rank = w_gate_up.shape[1]
    dv = gla_norm_w.shape[1]
    val = GLA_HEADS * dv
    dk = key // GLA_HEADS
    assert S % CHUNK == 0 and key * 2 == D and val == D
    assert w_pool_grp.shape[2] * len(POOL_WINDOWS) == D
    LP = PAD + N_META + S
    n_chunks = LP // CHUNK
    T = B * LP
    tm_proj = tm_mix = _row_tile(T, ROW_TILE)
    tm_row = _row_tile(T, COPY_TILE)
    tm_exp = EXPERT_TILE
    n_sorted = -(-(T + N_CLASSES * (tm_exp - 1)) // tm_exp) * tm_exp
    n_tiles = n_sorted // tm_exp

    head = jnp.concatenate([jnp.zeros((PAD, D), F32), meta_tokens.astype(F32)], axis=0)
    lp = np.arange(T) % LP
    ys = x.astype(F32).reshape(B * S, D)
    pos = jnp.asarray(np.where(lp >= CHUNK, (np.arange(T) // LP) * S + lp - CHUNK, 0), jnp.int32)

    c_q = 0
    c_gl = 2 * key + 2 * val
    c_u = c_gl + rank

    w_a = w_in[:, :, c_q:c_gl].astype(BF16)
    w_b = w_in[:, :, c_u:].astype(BF16)
    w_g = jnp.concatenate([w_in[:, :, c_gl:c_u].astype(BF16),
                           jnp.zeros((depth, D, LANES - rank), BF16)], axis=2)
    wg_hi, wg_lo = _split(jnp.concatenate(
        [w_gate_up, jnp.zeros((depth, LANES - rank, key), F32)], axis=1))
    wr = jnp.swapaxes(jnp.concatenate(
        [w_router_group, w_router_expert,
         jnp.zeros((depth, D, CLASS_ROWS - N_GROUPS - N_EXPERTS), F32)], axis=2), 1, 2)
    row_vec = lambda p: p[:, None, :]
    mix_params = (row_vec(gla_norm_w), w_pool_grp.astype(BF16), row_vec(pool_scale),
                  w_br_gla.astype(BF16), w_br_pool.astype(BF16), w_out.astype(BF16),
                  row_vec(norm2_w), wr)
    weg, weu, wed = w_exp_gate.astype(BF16), w_exp_up.astype(BF16), w_exp_down.astype(BF16)

    for l in range(depth):
        z, zg, h = _inproj_gather(ys, pos, head, row_vec(norm1_w), w_a, w_b, w_g, l, tm=tm_proj,
                                  seq_rows=LP if l == 0 else 0, n_batch=B)
        ogla = _gla(z, zg, wg_hi, wg_lo, row_vec(b_gate), l, batch=B, n_chunks=n_chunks, dk=dk, dv=dv)
        hx, info, counts = _mix(ogla, z, h, mix_params, l, tm=tm_mix, seq_rows=LP)

        cnt = counts[:N_CLASSES, 0].astype(jnp.int32)
        seg_tiles = (cnt + tm_exp - 1) // tm_exp
        tile_end = jnp.cumsum(seg_tiles)
        start = (tile_end - seg_tiles) * tm_exp
        n_used = tile_end[-1:]
        tile_cls = jnp.minimum(
            jnp.sum(jnp.arange(n_tiles)[:, None] >= tile_end[None, :], axis=1), N_CLASSES - 1)
        tile_cls = jnp.where(jnp.arange(n_tiles) < n_used[0], tile_cls,
                             jnp.take(tile_cls, jnp.maximum(n_used[0] - 1, 0)))
        pair_lo = jnp.array([p[0] for p in PAIRS], jnp.int32)
        pair_hi = jnp.array([p[1] for p in PAIRS], jnp.int32)
        grp = tile_cls // len(PAIRS)
        tile_ea = (grp * EXPERTS_PER_GROUP + jnp.take(pair_lo, tile_cls % len(PAIRS))).astype(jnp.int32)
        tile_eb = (grp * EXPERTS_PER_GROUP + jnp.take(pair_hi, tile_cls % len(PAIRS))).astype(jnp.int32)
        cls_t = info[0].astype(jnp.int32)
        pos = info[1].astype(jnp.int32) + jnp.sum(
            jnp.where(cls_t[:, None] == jnp.arange(N_CLASSES)[None, :], start[None, :], 0), axis=1)

        xs = _dispatch(hx, info, pos, tile_end.astype(jnp.int32), n_used.astype(jnp.int32),
                       tm=tm_row, tm_exp=tm_exp, n_sorted=n_sorted)
        ys = _moe(xs, tile_ea, tile_eb, n_used.astype(jnp.int32), row_vec(norm2_w), weg, weu, wed, l,
                  tm=tm_exp)

    pos_out = pos.reshape(B, LP)[:, PAD + N_META:].reshape(B * S)
    out = _collect_norm(ys, pos_out, final_norm_w[None], tm=_row_tile(B * S, COPY_TILE))
    return out.reshape(B, S, D)
```
